```python
import math
import jax, jax.numpy as jnp
from jax import lax
import numpy as np

D_MODEL = 2048
BATCH = 4
SEQ = 2048
DEPTH = 2
DEC_BATCH = 128
DEC_SEQ = 1
PAST_LEN = 16384
PAGE_SIZE = 128

EPS = 1e-6
CONV_W = 4
HG_HEADS = 4
HG_DK = 128
HG_DV = 128
HG_WIDTH = HG_HEADS * HG_DV
HG_CHUNK = 32
RG_HEADS = 6
RG_HEAD_DIM = 128
RG_WIDTH = RG_HEADS * RG_HEAD_DIM
RG_C = 8.0
SSM_HEADS = 12
SSM_HEADDIM = 64
SSM_WIDTH = SSM_HEADS * SSM_HEADDIM
SSM_GROUPS = 2
HEADS_PER_GROUP = SSM_HEADS // SSM_GROUPS
SSM_DSTATE = 128
SSM_CHUNK = 64
SSM_CONV_DIM = SSM_WIDTH + 2 * SSM_GROUPS * SSM_DSTATE
D_MIX = HG_WIDTH + RG_WIDTH + SSM_WIDTH
IN_SIZES = (HG_HEADS * HG_DK, HG_HEADS * HG_DK, HG_WIDTH, HG_WIDTH, RG_WIDTH, RG_WIDTH, SSM_WIDTH, SSM_CONV_DIM, SSM_HEADS)
D_IN_PROJ = 2 * HG_HEADS * HG_DK + 2 * HG_WIDTH + 2 * RG_WIDTH + SSM_WIDTH + SSM_CONV_DIM + SSM_HEADS
D_FF = 5632

kernel_name = 'hybrid_hgrn2_rglru_ssd_decode_step'


def rmsnorm(x, g):
    xf = x.astype(jnp.float32)
    y = xf * lax.rsqrt(jnp.mean(xf * xf, axis=-1, keepdims=True) + EPS)
    return (y * g.astype(jnp.float32)).astype(x.dtype)


def swiglu(x, w_gate, w_up, w_down):
    return (jax.nn.silu(x @ w_gate) * (x @ w_up)) @ w_down


def split_cols(t, sizes):
    return jnp.split(t, np.cumsum(sizes)[:-1].tolist(), axis=-1)


def chunk_len(L, c):
    return c if L % c == 0 else L


def causal_dwconv(x, prefix, w, b):
    L = x.shape[1]
    xp = jnp.concatenate([prefix.astype(x.dtype), x], axis=1)
    y = b
    for k in range(CONV_W):
        y = y + w[k] * xp[:, k:k + L]
    return y, xp[:, L:]


def hgrn2_mix(q, fz, v_in, g, lb, s0, norm_w):
    f32 = jnp.float32
    Bn, L, _ = q.shape
    C = chunk_len(L, HG_CHUNK)
    N = L // C

    def heads(t, d):
        return t.astype(f32).reshape(Bn, N, C, HG_HEADS, d).transpose(0, 3, 1, 2, 4)

    logf = heads(jnp.log(lb + (1.0 - lb) * jax.nn.sigmoid(fz.astype(f32))), HG_DK)
    k = -jnp.expm1(logf)
    qh = heads(jax.nn.silu(q.astype(f32)), HG_DK)
    v = heads(v_in, HG_DV)
    b = jnp.cumsum(logf, axis=3)
    b_ref = b[:, :, :, C // 2:C // 2 + 1]
    b_last = b[:, :, :, C - 1:]
    q_in = qh * jnp.exp(b - b_ref)
    k_in = k * jnp.exp(b_ref - b)
    causal = jnp.tril(jnp.ones((C, C), dtype=bool))
    a_intra = jnp.where(causal, jnp.einsum('bhnik,bhnjk->bhnij', q_in, k_in), 0.0)
    o = jnp.einsum('bhnij,bhnjv->bhniv', a_intra, v)
    ds = jnp.einsum('bhnjk,bhnjv->nbhkv', k * jnp.exp(b_last - b), v)
    decay = jnp.exp(b_last[:, :, :, 0]).transpose(2, 0, 1, 3)

    def step(s, inp):
        d, dsn = inp
        return d[..., None] * s + dsn, s

    s_final, s_prev = lax.scan(step, s0.astype(f32), (decay, ds))
    o = o + jnp.einsum('bhnik,nbhkv->bhniv', qh * jnp.exp(b), s_prev)
    o = o.transpose(0, 2, 3, 1, 4).reshape(Bn, L, HG_HEADS, HG_DV)
    o = o * lax.rsqrt(jnp.mean(o * o, axis=-1, keepdims=True) + EPS)
    o = o.reshape(Bn, L, HG_WIDTH) * norm_w.astype(f32) * jax.nn.silu(g.astype(f32))
    return o.astype(q.dtype), s_final


def rglru_mix(x_in, gate, prefix, h0, conv_w, conv_b, wa, ba, wx, bx, lam, reset_first):
    f32 = jnp.float32
    xc, new_prefix = causal_dwconv(x_in, prefix, conv_w, conv_b)
    Bn, L, _ = xc.shape
    xb = xc.reshape(Bn, L, RG_HEADS, RG_HEAD_DIM)
    r = jax.nn.sigmoid((jnp.einsum('blhi,hij->blhj', xb, wa) + ba).astype(f32)).reshape(Bn, L, RG_WIDTH)
    ig = jax.nn.sigmoid((jnp.einsum('blhi,hij->blhj', xb, wx) + bx).astype(f32)).reshape(Bn, L, RG_WIDTH)
    log_a = -RG_C * r * jax.nn.softplus(-lam.astype(f32))
    a = jnp.exp(log_a)
    mult = jnp.sqrt(-jnp.expm1(2.0 * log_a))
    if reset_first:
        mult = mult.at[:, 0].set(1.0)
    u = mult * ig * xc.astype(f32)
    u = u.at[:, 0].add(a[:, 0] * h0.astype(f32))

    def combine(lhs, rhs):
        a1, b1 = lhs
        a2, b2 = rhs
        return a1 * a2, a2 * b1 + b2

    _, h = lax.associative_scan(combine, (a, u), axis=1)
    y = h * jax.nn.gelu(gate.astype(f32))
    return y.astype(x_in.dtype), h[:, -1], new_prefix


def ssd_mix(z, xbc_in, dt_raw, prefix, s0, conv_w, conv_b, dt_bias, a_log, d_skip, norm_w):
    f32 = jnp.float32
    xbc, new_prefix = causal_dwconv(xbc_in, prefix, conv_w, conv_b)
    xbc = jax.nn.silu(xbc.astype(f32))
    Bn, L, _ = xbc.shape
    C = chunk_len(L, SSM_CHUNK)
    N = L // C
    xs, bm, cm = split_cols(xbc, (SSM_WIDTH, SSM_GROUPS * SSM_DSTATE, SSM_GROUPS * SSM_DSTATE))
    xh = xs.reshape(Bn, N, C, SSM_HEADS, SSM_HEADDIM)
    bh = jnp.repeat(bm.reshape(Bn, N, C, SSM_GROUPS, SSM_DSTATE), HEADS_PER_GROUP, axis=3)
    ch = jnp.repeat(cm.reshape(Bn, N, C, SSM_GROUPS, SSM_DSTATE), HEADS_PER_GROUP, axis=3)
    dt = jax.nn.softplus(dt_raw.astype(f32) + dt_bias.astype(f32)).reshape(Bn, N, C, SSM_HEADS)
    a = (dt * -jnp.exp(a_log.astype(f32))).transpose(0, 3, 1, 2)
    cs = jnp.cumsum(a, axis=-1)
    causal = jnp.tril(jnp.ones((C, C), dtype=bool))
    seg = jnp.where(causal, cs[..., :, None] - cs[..., None, :], 0.0)
    lmat = jnp.where(causal, jnp.exp(seg), 0.0)
    xdt = xh * dt[..., None]
    scores = jnp.einsum('bnihs,bnjhs->bhnij', ch, bh) * lmat
    y = jnp.einsum('bhnij,bnjhp->bnihp', scores, xdt)
    states = jnp.einsum('bnjhs,bhnj,bnjhp->nbhps', bh, jnp.exp(cs[..., -1:] - cs), xdt)
    decay = jnp.exp(cs[..., -1]).transpose(2, 0, 1)

    def step(s, inp):
        d, st = inp
        return d[:, :, None, None] * s + st, s

    s_final, s_prev = lax.scan(step, s0.astype(f32), (decay, states))
    y = y + jnp.einsum('bnihs,nbhps,bhni->bnihp', ch, s_prev, jnp.exp(cs))
    y = y + d_skip.astype(f32)[:, None] * xh
    y = y.reshape(Bn, L, SSM_WIDTH) * jax.nn.silu(z.astype(f32))
    yg = y.reshape(Bn, L, SSM_GROUPS, SSM_WIDTH // SSM_GROUPS)
    yg = yg * lax.rsqrt(jnp.mean(yg * yg, axis=-1, keepdims=True) + EPS)
    y = yg.reshape(Bn, L, SSM_WIDTH) * norm_w.astype(f32)
    return y.astype(z.dtype), s_final, new_prefix


def token_mixer(h, st, p, lb, reset_first):
    hg_s, rg_h, rg_c, ssm_s, ssm_c = st
    proj = h @ p['w_in']
    q, fz, hi, hg, rx, rgate, sz, sxbc, sdt = split_cols(proj, IN_SIZES)
    o_hg, hg_new = hgrn2_mix(q, fz, hi, hg, lb, hg_s, p['hg_norm_w'])
    o_rg, rg_h_new, rg_c_new = rglru_mix(rx, rgate, rg_c, rg_h, p['rg_conv_w'], p['rg_conv_b'],
                                         p['rg_wa'], p['rg_ba'], p['rg_wx'], p['rg_bx'],
                                         p['rg_lambda'], reset_first)
    o_ssm, ssm_new, ssm_c_new = ssd_mix(sz, sxbc, sdt, ssm_c, ssm_s, p['ssm_conv_w'], p['ssm_conv_b'],
                                        p['ssm_dt_bias'], p['ssm_a_log'], p['ssm_d'], p['ssm_norm_w'])
    o = jnp.concatenate([o_hg, o_rg, o_ssm], axis=-1) @ p['w_out']
    return o, (hg_new, rg_h_new, rg_c_new, ssm_new, ssm_c_new)


def run_trunk(x, init_states, lw, lower_bounds, reset_first):
    new_states = ([], [], [], [], [])
    for l in range(DEPTH):
        p = {name: arr[l] for name, arr in lw.items()}
        g = p['norm_g']
        h = swiglu(rmsnorm(x, g[0]), p['ffn1_w_gate'], p['ffn1_w_up'], p['ffn1_w_down'])
        x = x + 0.5 * rmsnorm(h, g[1])
        m, st = token_mixer(rmsnorm(x, g[2]), tuple(s[l] for s in init_states), p, lower_bounds[l], reset_first)
        x = x + rmsnorm(m, g[3])
        h = swiglu(rmsnorm(x, g[4]), p['ffn2_w_gate'], p['ffn2_w_up'], p['ffn2_w_down'])
        x = x + 0.5 * rmsnorm(h, g[5])
        for acc, s in zip(new_states, st):
            acc.append(s)
    return x, tuple(jnp.stack(acc) for acc in new_states)


def setup_inputs(seed: int = 0) -> dict:
    key = jax.random.key(seed)
    ks = jax.random.split(key, 32)
    f32 = jnp.float32

    def nrm(i, shape, scale):
        return scale * jax.random.normal(ks[i], shape, f32)

    a0 = jax.random.uniform(ks[24], (DEPTH, RG_WIDTH), f32, minval=0.9, maxval=0.999)
    dt0 = jnp.exp(jax.random.uniform(ks[27], (DEPTH, SSM_HEADS), f32, minval=math.log(1e-3), maxval=math.log(1e-1)))
    return {
        'x_prompt': nrm(0, (BATCH, SEQ, D_MODEL), 1.0),
        'x_sample': nrm(1, (DEC_BATCH, DEC_SEQ, D_MODEL), 1.0),
        'state_hgrn': nrm(2, (DEPTH, DEC_BATCH, HG_HEADS, HG_DK, HG_DV), 0.5),
        'state_rglru': nrm(3, (DEPTH, DEC_BATCH, RG_WIDTH), 0.5),
        'state_rglru_conv': nrm(4, (DEPTH, DEC_BATCH, CONV_W - 1, RG_WIDTH), 1.0),
        'state_ssm': nrm(5, (DEPTH, DEC_BATCH, SSM_HEADS, SSM_HEADDIM, SSM_DSTATE), 0.1),
        'state_ssm_conv': nrm(6, (DEPTH, DEC_BATCH, CONV_W - 1, SSM_CONV_DIM), 1.0),
        'norm_g': 1.0 + nrm(7, (DEPTH, 6, D_MODEL), 0.02),
        'ffn1_w_gate': nrm(8, (DEPTH, D_MODEL, D_FF), D_MODEL ** -0.5),
        'ffn1_w_up': nrm(9, (DEPTH, D_MODEL, D_FF), D_MODEL ** -0.5),
        'ffn1_w_down': nrm(10, (DEPTH, D_FF, D_MODEL), D_FF ** -0.5),
        'ffn2_w_gate': nrm(11, (DEPTH, D_MODEL, D_FF), D_MODEL ** -0.5),
        'ffn2_w_up': nrm(12, (DEPTH, D_MODEL, D_FF), D_MODEL ** -0.5),
        'ffn2_w_down': nrm(13, (DEPTH, D_FF, D_MODEL), D_FF ** -0.5),
        'w_in': nrm(14, (DEPTH, D_MODEL, D_IN_PROJ), D_MODEL ** -0.5),
        'w_out': nrm(15, (DEPTH, D_MIX, D_MODEL), D_MIX ** -0.5),
        'hg_lb_logits': nrm(16, (DEPTH, HG_HEADS * HG_DK), 1.0),
        'hg_norm_w': 1.0 + nrm(17, (DEPTH, HG_WIDTH), 0.02),
        'rg_conv_w': nrm(18, (DEPTH, CONV_W, RG_WIDTH), CONV_W ** -0.5),
        'rg_conv_b': nrm(19, (DEPTH, RG_WIDTH), 0.01),
        'rg_wa': nrm(20, (DEPTH, RG_HEADS, RG_HEAD_DIM, RG_HEAD_DIM), RG_HEAD_DIM ** -0.5),
        'rg_ba': nrm(21, (DEPTH, RG_HEADS, RG_HEAD_DIM), 0.01),
        'rg_wx': nrm(22, (DEPTH, RG_HEADS, RG_HEAD_DIM, RG_HEAD_DIM), RG_HEAD_DIM ** -0.5),
        'rg_bx': nrm(23, (DEPTH, RG_HEADS, RG_HEAD_DIM), 0.01),
        'rg_lambda': jnp.log(a0) - jnp.log1p(-a0),
        'ssm_conv_w': nrm(25, (DEPTH, CONV_W, SSM_CONV_DIM), CONV_W ** -0.5),
        'ssm_conv_b': nrm(26, (DEPTH, SSM_CONV_DIM), 0.01),
        'ssm_dt_bias': dt0 + jnp.log(-jnp.expm1(-dt0)),
        'ssm_a_log': jnp.log(jax.random.uniform(ks[28], (DEPTH, SSM_HEADS), f32, minval=1.0, maxval=16.0)),
        'ssm_d': 1.0 + nrm(29, (DEPTH, SSM_HEADS), 0.1),
        'ssm_norm_w': 1.0 + nrm(30, (DEPTH, SSM_WIDTH), 0.02),
    }


def reference(x_prompt, x_sample, state_hgrn, state_rglru, state_rglru_conv, state_ssm, state_ssm_conv,
              norm_g, ffn1_w_gate, ffn1_w_up, ffn1_w_down, ffn2_w_gate, ffn2_w_up, ffn2_w_down,
              w_in, w_out, hg_lb_logits, hg_norm_w, rg_conv_w, rg_conv_b, rg_wa, rg_ba, rg_wx, rg_bx,
              rg_lambda, ssm_conv_w, ssm_conv_b, ssm_dt_bias, ssm_a_log, ssm_d, ssm_norm_w):
    f32 = jnp.float32
    layer_w = {
        'norm_g': norm_g, 'ffn1_w_gate': ffn1_w_gate, 'ffn1_w_up': ffn1_w_up, 'ffn1_w_down': ffn1_w_down,
        'ffn2_w_gate': ffn2_w_gate, 'ffn2_w_up': ffn2_w_up, 'ffn2_w_down': ffn2_w_down,
        'w_in': w_in, 'w_out': w_out, 'hg_norm_w': hg_norm_w,
        'rg_conv_w': rg_conv_w, 'rg_conv_b': rg_conv_b, 'rg_wa': rg_wa, 'rg_ba': rg_ba,
        'rg_wx': rg_wx, 'rg_bx': rg_bx, 'rg_lambda': rg_lambda,
        'ssm_conv_w': ssm_conv_w, 'ssm_conv_b': ssm_conv_b, 'ssm_dt_bias': ssm_dt_bias,
        'ssm_a_log': ssm_a_log, 'ssm_d': ssm_d, 'ssm_norm_w': ssm_norm_w,
    }
    lb_cum = jnp.cumsum(jax.nn.softmax(hg_lb_logits.astype(f32), axis=0), axis=0)
    lower_bounds = lb_cum - lb_cum[:1]
    bp = x_prompt.shape[0]
    prompt_init = (jnp.zeros((DEPTH, bp, HG_HEADS, HG_DK, HG_DV), f32),
                   jnp.zeros((DEPTH, bp, RG_WIDTH), f32),
                   jnp.zeros((DEPTH, bp, CONV_W - 1, RG_WIDTH), x_prompt.dtype),
                   jnp.zeros((DEPTH, bp, SSM_HEADS, SSM_HEADDIM, SSM_DSTATE), f32),
                   jnp.zeros((DEPTH, bp, CONV_W - 1, SSM_CONV_DIM), x_prompt.dtype))
    sample_init = (state_hgrn, state_rglru, state_rglru_conv, state_ssm, state_ssm_conv)
    y_prompt, (hg_p, rg_p, rgc_p, ssm_p, ssmc_p) = run_trunk(x_prompt, prompt_init, layer_w, lower_bounds, True)
    y_sample, (hg_s, rg_s, rgc_s, ssm_s, ssmc_s) = run_trunk(x_sample, sample_init, layer_w, lower_bounds, False)
    return (y_prompt, y_sample, hg_p, hg_s, rg_p, rg_s, rgc_p, rgc_s, ssm_p, ssm_s, ssmc_p, ssmc_s)
```

```python
import functools
import math

import jax
import jax.numpy as jnp
import numpy as np
from jax import lax
from jax.experimental import pallas as pl
from jax.experimental.pallas import tpu as pltpu

F32 = jnp.float32
BF16 = jnp.bfloat16

D_MODEL = 2048
DEPTH = 2
EPS = 1e-6
CONV_W = 4
HG_HEADS = 4
HG_DK = 128
HG_DV = 128
HG_WIDTH = HG_HEADS * HG_DV
HG_CHUNK = 32
RG_HEADS = 6
RG_HEAD_DIM = 128
RG_WIDTH = RG_HEADS * RG_HEAD_DIM
RG_C = 8.0
SSM_HEADS = 12
SSM_HEADDIM = 64
SSM_WIDTH = SSM_HEADS * SSM_HEADDIM
SSM_GROUPS = 2
HEADS_PER_GROUP = SSM_HEADS // SSM_GROUPS
SSM_DSTATE = 128
SSM_CHUNK = 64
SSM_CONV_DIM = SSM_WIDTH + 2 * SSM_GROUPS * SSM_DSTATE
D_MIX = HG_WIDTH + RG_WIDTH + SSM_WIDTH
IN_SIZES = (HG_HEADS * HG_DK, HG_HEADS * HG_DK, HG_WIDTH, HG_WIDTH, RG_WIDTH, RG_WIDTH,
            SSM_WIDTH, SSM_CONV_DIM, SSM_HEADS)
D_IN_PROJ = sum(IN_SIZES)
D_FF = 5632

LANES = 128
D_IN_PAD = -(-D_IN_PROJ // LANES) * LANES
ROW_TILE = 640
FF_TILE = 512
IN_TILE = 1152
VMEM_LIMIT = 56 * 1024 * 1024


def _rms(x, g):
    return x * lax.rsqrt(jnp.mean(x * x, axis=-1, keepdims=True) + EPS) * g


def _ffn_kernel(x_ref, gin_ref, gout_ref, wg_ref, wu_ref, wd_ref, o_ref, xn_ref):
    j = pl.program_id(1)

    @pl.when(j == 0)
    def _():
        xn_ref[...] = _rms(x_ref[...], gin_ref[...]).astype(BF16)
        o_ref[...] = jnp.zeros_like(o_ref)

    xn = xn_ref[...]
    g = jnp.dot(xn, wg_ref[...], preferred_element_type=F32)
    u = jnp.dot(xn, wu_ref[...], preferred_element_type=F32)
    h = (g * jax.nn.sigmoid(g) * u).astype(BF16)
    o_ref[...] += jnp.dot(h, wd_ref[...], preferred_element_type=F32)

    @pl.when(j == pl.num_programs(1) - 1)
    def _():
        o_ref[...] = x_ref[...] + 0.5 * _rms(o_ref[...], gout_ref[...])


def _ffn(x, g_in, g_out, wg, wu, wd, layer):
    m = x.shape[0]
    return pl.pallas_call(
        _ffn_kernel,
        grid=(m // ROW_TILE, D_FF // FF_TILE),
        in_specs=[
            pl.BlockSpec((ROW_TILE, D_MODEL), lambda i, j: (i, 0)),
            pl.BlockSpec((1, D_MODEL), lambda i, j: (0, 0)),
            pl.BlockSpec((1, D_MODEL), lambda i, j: (0, 0)),
            pl.BlockSpec((None, D_MODEL, FF_TILE), lambda i, j: (layer, 0, j)),
            pl.BlockSpec((None, D_MODEL, FF_TILE), lambda i, j: (layer, 0, j)),
            pl.BlockSpec((None, FF_TILE, D_MODEL), lambda i, j: (layer, j, 0)),
        ],
        out_specs=pl.BlockSpec((ROW_TILE, D_MODEL), lambda i, j: (i, 0)),
        out_shape=jax.ShapeDtypeStruct((m, D_MODEL), F32),
        scratch_shapes=[pltpu.VMEM((ROW_TILE, D_MODEL), BF16)],
        compiler_params=pltpu.CompilerParams(
            dimension_semantics=("parallel", "arbitrary"), vmem_limit_bytes=VMEM_LIMIT),
        name="ffn",
    )(x, g_in, g_out, wg, wu, wd)


def _inproj_kernel(x_ref, g_ref, w_ref, o_ref, xn_ref):
    @pl.when(pl.program_id(1) == 0)
    def _():
        xn_ref[...] = _rms(x_ref[...], g_ref[...]).astype(BF16)

    o_ref[...] = jnp.dot(xn_ref[...], w_ref[...], preferred_element_type=F32)


def _inproj(x, g, w_in, layer):
    m = x.shape[0]
    return pl.pallas_call(
        _inproj_kernel,
        grid=(m // ROW_TILE, D_IN_PAD // IN_TILE),
        in_specs=[
            pl.BlockSpec((ROW_TILE, D_MODEL), lambda i, j: (i, 0)),
            pl.BlockSpec((1, D_MODEL), lambda i, j: (0, 0)),
            pl.BlockSpec((None, D_MODEL, IN_TILE), lambda i, j: (layer, 0, j)),
        ],
        out_specs=pl.BlockSpec((ROW_TILE, IN_TILE), lambda i, j: (i, j)),
        out_shape=jax.ShapeDtypeStruct((m, D_IN_PAD), F32),
        scratch_shapes=[pltpu.VMEM((ROW_TILE, D_MODEL), BF16)],
        compiler_params=pltpu.CompilerParams(
            dimension_semantics=("parallel", "arbitrary"), vmem_limit_bytes=VMEM_LIMIT),
        name="inproj",
    )(x, g, w_in)


def _outproj_kernel(x_ref, o_ref, g_ref, w_ref, y_ref):
    m = jnp.dot(o_ref[...], w_ref[...], preferred_element_type=F32)
    y_ref[...] = x_ref[...] + _rms(m, g_ref[...])


def _outproj(x, o, g, w_out, layer):
    m = x.shape[0]
    return pl.pallas_call(
        _outproj_kernel,
        grid=(m // ROW_TILE,),
        in_specs=[
            pl.BlockSpec((ROW_TILE, D_MODEL), lambda i: (i, 0)),
            pl.BlockSpec((ROW_TILE, D_MIX), lambda i: (i, 0)),
            pl.BlockSpec((1, D_MODEL), lambda i: (0, 0)),
            pl.BlockSpec((None, D_MIX, D_MODEL), lambda i: (layer, 0, 0)),
        ],
        out_specs=pl.BlockSpec((ROW_TILE, D_MODEL), lambda i: (i, 0)),
        out_shape=jax.ShapeDtypeStruct((m, D_MODEL), F32),
        compiler_params=pltpu.CompilerParams(
            dimension_semantics=("parallel",), vmem_limit_bytes=VMEM_LIMIT),
        name="outproj",
    )(x, o, g, w_out)


def _split_cols(t, sizes):
    return jnp.split(t, np.cumsum(sizes)[:-1].tolist(), axis=-1)


def _chunk_len(L, c):
    return c if L % c == 0 else L


def _causal_dwconv(x, prefix, w, b):
    L = x.shape[1]
    xp = jnp.concatenate([prefix.astype(x.dtype), x], axis=1)
    y = b
    for k in range(CONV_W):
        y = y + w[k] * xp[:, k:k + L]
    return y, xp[:, L:]


def _hgrn2_mix(q, fz, v_in, g, lb, s0, norm_w):
    Bn, L, _ = q.shape
    C = _chunk_len(L, HG_CHUNK)
    N = L // C

    def heads(t, d):
        return t.astype(F32).reshape(Bn, N, C, HG_HEADS, d).transpose(0, 3, 1, 2, 4)

    logf = heads(jnp.log(lb + (1.0 - lb) * jax.nn.sigmoid(fz.astype(F32))), HG_DK)
    k = -jnp.expm1(logf)
    qh = heads(jax.nn.silu(q.astype(F32)), HG_DK)
    v = heads(v_in, HG_DV)
    b = jnp.cumsum(logf, axis=3)
    b_ref = b[:, :, :, C // 2:C // 2 + 1]
    b_last = b[:, :, :, C - 1:]
    q_in = qh * jnp.exp(b - b_ref)
    k_in = k * jnp.exp(b_ref - b)
    causal = jnp.tril(jnp.ones((C, C), dtype=bool))
    a_intra = jnp.where(causal, jnp.einsum('bhnik,bhnjk->bhnij', q_in, k_in), 0.0)
    o = jnp.einsum('bhnij,bhnjv->bhniv', a_intra, v)
    ds = jnp.einsum('bhnjk,bhnjv->nbhkv', k * jnp.exp(b_last - b), v)
    decay = jnp.exp(b_last[:, :, :, 0]).transpose(2, 0, 1, 3)

    def step(s, inp):
        d, dsn = inp
        return d[..., None] * s + dsn, s

    s_final, s_prev = lax.scan(step, s0.astype(F32), (decay, ds))
    o = o + jnp.einsum('bhnik,nbhkv->bhniv', qh * jnp.exp(b), s_prev)
    o = o.transpose(0, 2, 3, 1, 4).reshape(Bn, L, HG_HEADS, HG_DV)
    o = o * lax.rsqrt(jnp.mean(o * o, axis=-1, keepdims=True) + EPS)
    o = o.reshape(Bn, L, HG_WIDTH) * norm_w.astype(F32) * jax.nn.silu(g.astype(F32))
    return o, s_final


def _rglru_mix(x_in, gate, prefix, h0, conv_w, conv_b, wa, ba, wx, bx, lam, reset_first):
    xc, new_prefix = _causal_dwconv(x_in, prefix, conv_w, conv_b)
    Bn, L, _ = xc.shape
    xb = xc.reshape(Bn, L, RG_HEADS, RG_HEAD_DIM)
    r = jax.nn.sigmoid((jnp.einsum('blhi,hij->blhj', xb, wa) + ba).astype(F32)).reshape(Bn, L, RG_WIDTH)
    ig = jax.nn.sigmoid((jnp.einsum('blhi,hij->blhj', xb, wx) + bx).astype(F32)).reshape(Bn, L, RG_WIDTH)
    log_a = -RG_C * r * jax.nn.softplus(-lam.astype(F32))
    a = jnp.exp(log_a)
    mult = jnp.sqrt(-jnp.expm1(2.0 * log_a))
    if reset_first:
        mult = mult.at[:, 0].set(1.0)
    u = mult * ig * xc.astype(F32)
    u = u.at[:, 0].add(a[:, 0] * h0.astype(F32))

    def combine(lhs, rhs):
        a1, b1 = lhs
        a2, b2 = rhs
        return a1 * a2, a2 * b1 + b2

    _, h = lax.associative_scan(combine, (a, u), axis=1)
    y = h * jax.nn.gelu(gate.astype(F32))
    return y, h[:, -1], new_prefix


def _ssd_mix(z, xbc_in, dt_raw, prefix, s0, conv_w, conv_b, dt_bias, a_log, d_skip, norm_w):
    xbc, new_prefix = _causal_dwconv(xbc_in, prefix, conv_w, conv_b)
    xbc = jax.nn.silu(xbc.astype(F32))
    Bn, L, _ = xbc.shape
    C = _chunk_len(L, SSM_CHUNK)
    N = L // C
    xs, bm, cm = _split_cols(xbc, (SSM_WIDTH, SSM_GROUPS * SSM_DSTATE, SSM_GROUPS * SSM_DSTATE))
    xh = xs.reshape(Bn, N, C, SSM_HEADS, SSM_HEADDIM)
    bh = jnp.repeat(bm.reshape(Bn, N, C, SSM_GROUPS, SSM_DSTATE), HEADS_PER_GROUP, axis=3)
    ch = jnp.repeat(cm.reshape(Bn, N, C, SSM_GROUPS, SSM_DSTATE), HEADS_PER_GROUP, axis=3)
    dt = jax.nn.softplus(dt_raw.astype(F32) + dt_bias.astype(F32)).reshape(Bn, N, C, SSM_HEADS)
    a = (dt * -jnp.exp(a_log.astype(F32))).transpose(0, 3, 1, 2)
    cs = jnp.cumsum(a, axis=-1)
    causal = jnp.tril(jnp.ones((C, C), dtype=bool))
    seg = jnp.where(causal, cs[..., :, None] - cs[..., None, :], 0.0)
    lmat = jnp.where(causal, jnp.exp(seg), 0.0)
    xdt = xh * dt[..., None]
    scores = jnp.einsum('bnihs,bnjhs->bhnij', ch, bh) * lmat
    y = jnp.einsum('bhnij,bnjhp->bnihp', scores, xdt)
    states = jnp.einsum('bnjhs,bhnj,bnjhp->nbhps', bh, jnp.exp(cs[..., -1:] - cs), xdt)
    decay = jnp.exp(cs[..., -1]).transpose(2, 0, 1)

    def step(s, inp):
        d, st = inp
        return d[:, :, None, None] * s + st, s

    s_final, s_prev = lax.scan(step, s0.astype(F32), (decay, states))
    y = y + jnp.einsum('bnihs,nbhps,bhni->bnihp', ch, s_prev, jnp.exp(cs))
    y = y + d_skip.astype(F32)[:, None] * xh
    y = y.reshape(Bn, L, SSM_WIDTH) * jax.nn.silu(z.astype(F32))
    yg = y.reshape(Bn, L, SSM_GROUPS, SSM_WIDTH // SSM_GROUPS)
    yg = yg * lax.rsqrt(jnp.mean(yg * yg, axis=-1, keepdims=True) + EPS)
    y = yg.reshape(Bn, L, SSM_WIDTH) * norm_w.astype(F32)
    return y, s_final, new_prefix


def _mix_group(proj, st, p, lb, reset_first):
    hg_s, rg_h, rg_c, ssm_s, ssm_c = st
    q, fz, hi, hg, rx, rgate, sz, sxbc, sdt = _split_cols(proj[..., :D_IN_PROJ], IN_SIZES)
    o_hg, hg_new = _hgrn2_mix(q, fz, hi, hg, lb, hg_s, p['hg_norm_w'])
    o_rg, rg_h_new, rg_c_new = _rglru_mix(rx, rgate, rg_c, rg_h, p['rg_conv_w'], p['rg_conv_b'],
                                          p['rg_wa'], p['rg_ba'], p['rg_wx'], p['rg_bx'],
                                          p['rg_lambda'], reset_first)
    o_ssm, ssm_new, ssm_c_new = _ssd_mix(sz, sxbc, sdt, ssm_c, ssm_s, p['ssm_conv_w'], p['ssm_conv_b'],
                                         p['ssm_dt_bias'], p['ssm_a_log'], p['ssm_d'], p['ssm_norm_w'])
    o = jnp.concatenate([o_hg, o_rg, o_ssm], axis=-1)
    return o, (hg_new, rg_h_new, rg_c_new, ssm_new, ssm_c_new)


def kernel(x_prompt, x_sample, state_hgrn, state_rglru, state_rglru_conv, state_ssm, state_ssm_conv, norm_g, ffn1_w_gate, ffn1_w_up, ffn1_w_down, ffn2_w_gate, ffn2_w_up, ffn2_w_down, w_in, w_out, hg_lb_logits, hg_norm_w, rg_conv_w, rg_conv_b, rg_wa, rg_ba, rg_wx, rg_bx, rg_lambda, ssm_conv_w, ssm_conv_b, ssm_dt_bias, ssm_a_log, ssm_d, ssm_norm_w):
    bp, lp, _ = x_prompt.shape
    bs, ls, _ = x_sample.shape
    mp = bp * lp
    ms = bs * ls
    assert (mp + ms) % ROW_TILE == 0 and D_FF % FF_TILE == 0 and D_IN_PAD % IN_TILE == 0

    lw = {
        'hg_norm_w': hg_norm_w, 'rg_conv_w': rg_conv_w, 'rg_conv_b': rg_conv_b, 'rg_wa': rg_wa,
        'rg_ba': rg_ba, 'rg_wx': rg_wx, 'rg_bx': rg_bx, 'rg_lambda': rg_lambda,
        'ssm_conv_w': ssm_conv_w, 'ssm_conv_b': ssm_conv_b, 'ssm_dt_bias': ssm_dt_bias,
        'ssm_a_log': ssm_a_log, 'ssm_d': ssm_d, 'ssm_norm_w': ssm_norm_w,
    }
    lb_cum = jnp.cumsum(jax.nn.softmax(hg_lb_logits.astype(F32), axis=0), axis=0)
    lower_bounds = lb_cum - lb_cum[:1]

    w1g, w1u, w1d = (w.astype(BF16) for w in (ffn1_w_gate, ffn1_w_up, ffn1_w_down))
    w2g, w2u, w2d = (w.astype(BF16) for w in (ffn2_w_gate, ffn2_w_up, ffn2_w_down))
    w_in_b = jnp.pad(w_in.astype(BF16), ((0, 0), (0, 0), (0, D_IN_PAD - D_IN_PROJ)))
    w_out_b = w_out.astype(BF16)

    prompt_init = (jnp.zeros((DEPTH, bp, HG_HEADS, HG_DK, HG_DV), F32),
                   jnp.zeros((DEPTH, bp, RG_WIDTH), F32),
                   jnp.zeros((DEPTH, bp, CONV_W - 1, RG_WIDTH), F32),
                   jnp.zeros((DEPTH, bp, SSM_HEADS, SSM_HEADDIM, SSM_DSTATE), F32),
                   jnp.zeros((DEPTH, bp, CONV_W - 1, SSM_CONV_DIM), F32))
    sample_init = (state_hgrn, state_rglru, state_rglru_conv, state_ssm, state_ssm_conv)

    x = jnp.concatenate([x_prompt.reshape(mp, D_MODEL), x_sample.reshape(ms, D_MODEL)], axis=0)
    new_p = ([], [], [], [], [])
    new_s = ([], [], [], [], [])
    for l in range(DEPTH):
        g = norm_g[l].reshape(6, 1, D_MODEL)
        p = {name: arr[l] for name, arr in lw.items()}
        x = _ffn(x, g[0], g[1], w1g, w1u, w1d, l)
        proj = _inproj(x, g[2], w_in_b, l)
        o_p, st_p = _mix_group(proj[:mp].reshape(bp, lp, D_IN_PAD), tuple(s[l] for s in prompt_init),
                               p, lower_bounds[l], True)
        o_s, st_s = _mix_group(proj[mp:].reshape(bs, ls, D_IN_PAD), tuple(s[l] for s in sample_init),
                               p, lower_bounds[l], False)
        o = jnp.concatenate([o_p.reshape(mp, D_MIX), o_s.reshape(ms, D_MIX)], axis=0).astype(BF16)
        x = _outproj(x, o, g[3], w_out_b, l)
        x = _ffn(x, g[4], g[5], w2g, w2u, w2d, l)
        for acc, s in zip(new_p, st_p):
            acc.append(s)
        for acc, s in zip(new_s, st_s):
            acc.append(s)
    hg_p, rg_p, rgc_p, ssm_p, ssmc_p = (jnp.stack(a) for a in new_p)
    hg_s, rg_s, rgc_s, ssm_s, ssmc_s = (jnp.stack(a) for a in new_s)
    y_prompt = x[:mp].reshape(bp, lp, D_MODEL)
    y_sample = x[mp:].reshape(bs, ls, D_MODEL)
    return (y_prompt, y_sample, hg_p, hg_s, rg_p, rg_s, rgc_p, rgc_s, ssm_p, ssm_s, ssmc_p, ssmc_s)
```

```python
import functools
import math

import jax
import jax.numpy as jnp
import numpy as np
from jax import lax
from jax.experimental import pallas as pl
from jax.experimental.pallas import tpu as pltpu

F32 = jnp.float32
BF16 = jnp.bfloat16

D_MODEL = 2048
DEPTH = 2
EPS = 1e-6
CONV_W = 4
HG_HEADS = 4
HG_DK = 128
HG_DV = 128
HG_WIDTH = HG_HEADS * HG_DV
HG_CHUNK = 32
RG_HEADS = 6
RG_HEAD_DIM = 128
RG_WIDTH = RG_HEADS * RG_HEAD_DIM
RG_C = 8.0
SSM_HEADS = 12
SSM_HEADDIM = 64
SSM_WIDTH = SSM_HEADS * SSM_HEADDIM
SSM_GROUPS = 2
HEADS_PER_GROUP = SSM_HEADS // SSM_GROUPS
SSM_DSTATE = 128
SSM_CHUNK = 64
SSM_CONV_DIM = SSM_WIDTH + 2 * SSM_GROUPS * SSM_DSTATE
D_MIX = HG_WIDTH + RG_WIDTH + SSM_WIDTH
IN_SIZES = (HG_HEADS * HG_DK, HG_HEADS * HG_DK, HG_WIDTH, HG_WIDTH, RG_WIDTH, RG_WIDTH,
            SSM_WIDTH, SSM_CONV_DIM, SSM_HEADS)
D_IN_PROJ = sum(IN_SIZES)
D_FF = 5632

LANES = 128
D_IN_PAD = -(-D_IN_PROJ // LANES) * LANES
ROW_TILE = 640
FF_TILE = 512
IN_TILE = 1152
VMEM_LIMIT = 56 * 1024 * 1024


def _rms(x, g):
    return x * lax.rsqrt(jnp.mean(x * x, axis=-1, keepdims=True) + EPS) * g


def _ffn_kernel(x_ref, gin_ref, gout_ref, wg_ref, wu_ref, wd_ref, o_ref, xn_ref):
    j = pl.program_id(1)

    @pl.when(j == 0)
    def _():
        xn_ref[...] = _rms(x_ref[...], gin_ref[...]).astype(BF16)
        o_ref[...] = jnp.zeros_like(o_ref)

    xn = xn_ref[...]
    g = jnp.dot(xn, wg_ref[...], preferred_element_type=F32)
    u = jnp.dot(xn, wu_ref[...], preferred_element_type=F32)
    h = (g * jax.nn.sigmoid(g) * u).astype(BF16)
    o_ref[...] += jnp.dot(h, wd_ref[...], preferred_element_type=F32)

    @pl.when(j == pl.num_programs(1) - 1)
    def _():
        o_ref[...] = x_ref[...] + 0.5 * _rms(o_ref[...], gout_ref[...])


def _ffn(x, g_in, g_out, wg, wu, wd, layer):
    m = x.shape[0]
    return pl.pallas_call(
        _ffn_kernel,
        grid=(m // ROW_TILE, D_FF // FF_TILE),
        in_specs=[
            pl.BlockSpec((ROW_TILE, D_MODEL), lambda i, j: (i, 0)),
            pl.BlockSpec((1, D_MODEL), lambda i, j: (0, 0)),
            pl.BlockSpec((1, D_MODEL), lambda i, j: (0, 0)),
            pl.BlockSpec((None, D_MODEL, FF_TILE), lambda i, j: (layer, 0, j)),
            pl.BlockSpec((None, D_MODEL, FF_TILE), lambda i, j: (layer, 0, j)),
            pl.BlockSpec((None, FF_TILE, D_MODEL), lambda i, j: (layer, j, 0)),
        ],
        out_specs=pl.BlockSpec((ROW_TILE, D_MODEL), lambda i, j: (i, 0)),
        out_shape=jax.ShapeDtypeStruct((m, D_MODEL), F32),
        scratch_shapes=[pltpu.VMEM((ROW_TILE, D_MODEL), BF16)],
        compiler_params=pltpu.CompilerParams(
            dimension_semantics=("parallel", "arbitrary"), vmem_limit_bytes=VMEM_LIMIT),
        name="ffn",
    )(x, g_in, g_out, wg, wu, wd)


def _inproj_kernel(x_ref, g_ref, w_ref, o_ref, xn_ref):
    @pl.when(pl.program_id(1) == 0)
    def _():
        xn_ref[...] = _rms(x_ref[...], g_ref[...]).astype(BF16)

    o_ref[...] = jnp.dot(xn_ref[...], w_ref[...], preferred_element_type=F32)


def _inproj(x, g, w_in, layer):
    m = x.shape[0]
    return pl.pallas_call(
        _inproj_kernel,
        grid=(m // ROW_TILE, D_IN_PAD // IN_TILE),
        in_specs=[
            pl.BlockSpec((ROW_TILE, D_MODEL), lambda i, j: (i, 0)),
            pl.BlockSpec((1, D_MODEL), lambda i, j: (0, 0)),
            pl.BlockSpec((None, D_MODEL, IN_TILE), lambda i, j: (layer, 0, j)),
        ],
        out_specs=pl.BlockSpec((ROW_TILE, IN_TILE), lambda i, j: (i, j)),
        out_shape=jax.ShapeDtypeStruct((m, D_IN_PAD), F32),
        scratch_shapes=[pltpu.VMEM((ROW_TILE, D_MODEL), BF16)],
        compiler_params=pltpu.CompilerParams(
            dimension_semantics=("parallel", "arbitrary"), vmem_limit_bytes=VMEM_LIMIT),
        name="inproj",
    )(x, g, w_in)


def _outproj_kernel(x_ref, ohg_ref, org_ref, ossm_ref, g_ref, w_ref, y_ref):
    r0, r1 = HG_WIDTH, HG_WIDTH + RG_WIDTH
    m = jnp.dot(ohg_ref[...], w_ref[0:r0, :], preferred_element_type=F32)
    m += jnp.dot(org_ref[...], w_ref[r0:r1, :], preferred_element_type=F32)
    m += jnp.dot(ossm_ref[...], w_ref[r1:, :], preferred_element_type=F32)
    y_ref[...] = x_ref[...] + _rms(m, g_ref[...])


def _outproj(x, o_hg, o_rg, o_ssm, g, w_out, layer):
    m = x.shape[0]
    return pl.pallas_call(
        _outproj_kernel,
        grid=(m // ROW_TILE,),
        in_specs=[
            pl.BlockSpec((ROW_TILE, D_MODEL), lambda i: (i, 0)),
            pl.BlockSpec((ROW_TILE, HG_WIDTH), lambda i: (i, 0)),
            pl.BlockSpec((ROW_TILE, RG_WIDTH), lambda i: (i, 0)),
            pl.BlockSpec((ROW_TILE, SSM_WIDTH), lambda i: (i, 0)),
            pl.BlockSpec((1, D_MODEL), lambda i: (0, 0)),
            pl.BlockSpec((None, D_MIX, D_MODEL), lambda i: (layer, 0, 0)),
        ],
        out_specs=pl.BlockSpec((ROW_TILE, D_MODEL), lambda i: (i, 0)),
        out_shape=jax.ShapeDtypeStruct((m, D_MODEL), F32),
        compiler_params=pltpu.CompilerParams(
            dimension_semantics=("parallel",), vmem_limit_bytes=VMEM_LIMIT),
        name="outproj",
    )(x, o_hg, o_rg, o_ssm, g, w_out)


SSM_OFF = 0
SSM_COLS = SSM_WIDTH + SSM_CONV_DIM + LANES
HG0_OFF = SSM_OFF + SSM_COLS
RG_OFF = HG0_OFF + LANES
HG1_OFF = RG_OFF + 2 * RG_WIDTH
assert RG_OFF % RG_WIDTH == 0 and HG1_OFF % LANES == 0
assert HG1_OFF + 4 * HG_WIDTH - LANES == D_IN_PAD


def _hg_col_block(kind, head):
    idx = kind * HG_HEADS + head
    return HG0_OFF // LANES if idx == 0 else HG1_OFF // LANES + idx - 1


def _permute_w_in(w):
    c = np.cumsum((0,) + IN_SIZES)
    hg, rg, ssm = w[..., c[0]:c[4]], w[..., c[4]:c[6]], w[..., c[6]:c[9]]
    pad = jnp.zeros(w.shape[:-1] + (LANES - SSM_HEADS,), w.dtype)
    return jnp.concatenate([ssm, pad, hg[..., :LANES], rg, hg[..., LANES:]], axis=-1).astype(BF16)

SUBLANES = 8
SSD_T = 256
NEG_BIG = -1e30


def _split3(x):
    hi = x.astype(BF16)
    r = x - hi.astype(F32)
    mid = r.astype(BF16)
    lo = (r - mid.astype(F32)).astype(BF16)
    return hi, mid, lo


def _cumsum_rows(tri, x):
    w = x.shape[1]
    parts = jnp.concatenate(_split3(x), axis=1)
    r = jnp.dot(tri, parts, preferred_element_type=F32)
    return r[:, :w] + r[:, w:2 * w] + r[:, 2 * w:]


def _silu(x):
    return x * jax.nn.sigmoid(x)


def _neg_expm1(x):
    series = -x * (1.0 + x * (1 / 2 + x * (1 / 6 + x * (1 / 24 + x * (1 / 120 + x * (1 / 720 + x / 5040))))))
    return jnp.where(x > -0.25, series, 1.0 - jnp.exp(x))


def _softplus(x):
    return jnp.maximum(x, 0.0) + jnp.log(1.0 + jnp.exp(-jnp.abs(x)))


def _conv4(xp_ref, tail_ref, x, w_ref, b_ref, first):
    t = x.shape[0]

    @pl.when(first)
    def _():
        tail_ref[...] = jnp.zeros_like(tail_ref)

    xp_ref[0:SUBLANES, :] = tail_ref[...]
    xp_ref[SUBLANES:, :] = x
    tail_ref[...] = x[t - SUBLANES:, :]
    y = b_ref[...] + w_ref[CONV_W - 1:CONV_W, :] * x
    for k in range(CONV_W - 1):
        off = SUBLANES - (CONV_W - 1) + k
        y = y + w_ref[k:k + 1, :] * xp_ref[off:off + t, :]
    return y


def _ssd_prompt_kernel(p_ref, tri_ref, cw_ref, cb_ref, dtb_ref, alog_ref, dexp_ref, nw_ref,
                       o_ref, st_ref, cst_ref, xp_ref, tail_ref, s_ref, y_ref):
    n = pl.program_id(1)
    t = SSD_T
    z = p_ref[:, 0:SSM_WIDTH]
    xbc_raw = p_ref[:, SSM_WIDTH:SSM_WIDTH + SSM_CONV_DIM]
    dt_raw = p_ref[:, SSM_WIDTH + SSM_CONV_DIM:]

    @pl.when(n == 0)
    def _():
        s_ref[...] = jnp.zeros_like(s_ref)

    xbc = _silu(_conv4(xp_ref, tail_ref, xbc_raw, cw_ref, cb_ref, n == 0))
    cst_ref[...] = p_ref[t - (CONV_W - 1):t, SSM_WIDTH:SSM_WIDTH + SSM_CONV_DIM]
    xs = xbc[:, :SSM_WIDTH]
    gs = SSM_GROUPS * SSM_DSTATE
    bm = xbc[:, SSM_WIDTH:SSM_WIDTH + gs].astype(BF16)
    cm = xbc[:, SSM_WIDTH + gs:].astype(BF16)

    dt = _softplus(dt_raw + dtb_ref[...])
    a = dt * -jnp.exp(alog_ref[...])
    cs = _cumsum_rows(tri_ref[...], a)
    cs_t = cs.T
    cs_last = cs[t - 1:t, :]
    e_cs = jnp.exp(cs)
    e_rem = jnp.exp(cs_last - cs)
    e_last = jnp.exp(cs_last)
    row = lax.broadcasted_iota(jnp.int32, (t, t), 0)
    col = lax.broadcasted_iota(jnp.int32, (t, t), 1)
    causal = row >= col

    for g in range(SSM_GROUPS):
        bg = bm[:, g * SSM_DSTATE:(g + 1) * SSM_DSTATE]
        cg = cm[:, g * SSM_DSTATE:(g + 1) * SSM_DSTATE]
        cb = lax.dot_general(cg, bg, (((1,), (1,)), ((), ())), preferred_element_type=F32)
        for hh in range(HEADS_PER_GROUP):
            h = g * HEADS_PER_GROUP + hh
            sl = slice(h * SSM_HEADDIM, (h + 1) * SSM_HEADDIM)
            seg = jnp.where(causal, cs[:, h:h + 1] - cs_t[h:h + 1, :], NEG_BIG)
            scores = (cb * jnp.exp(seg)).astype(BF16)
            xdt = xs[:, sl] * dt[:, h:h + 1]
            s_prev = s_ref[h]
            y = jnp.dot(scores, xdt.astype(BF16), preferred_element_type=F32)
            y_in = lax.dot_general(cg, s_prev.astype(BF16), (((1,), (1,)), ((), ())),
                                   preferred_element_type=F32)
            y_ref[:, sl] = y + e_cs[:, h:h + 1] * y_in
            w = (xdt * e_rem[:, h:h + 1]).astype(BF16)
            st = lax.dot_general(w, bg, (((0,), (0,)), ((), ())), preferred_element_type=F32)
            s_ref[h] = e_last[:, h:h + 1] * s_prev + st

    y = (y_ref[...] + dexp_ref[...] * xs) * _silu(z)
    gw = SSM_WIDTH // SSM_GROUPS
    for g in range(SSM_GROUPS):
        yg = y[:, g * gw:(g + 1) * gw]
        yg = yg * lax.rsqrt(jnp.mean(yg * yg, axis=-1, keepdims=True) + EPS)
        o_ref[:, g * gw:(g + 1) * gw] = (yg * nw_ref[:, g * gw:(g + 1) * gw]).astype(BF16)

    @pl.when(n == pl.num_programs(1) - 1)
    def _():
        st_ref[...] = s_ref[...]


def _tri(t):
    return jnp.tril(jnp.ones((t, t), F32)).astype(BF16)


def _ssd_prompt(proj, p, bp, lp, m_all):
    nc = lp // SSD_T
    full = lambda b, n: (0, 0)
    dtb = jnp.pad(p['ssm_dt_bias'].reshape(1, SSM_HEADS), ((0, 0), (0, LANES - SSM_HEADS)))
    alog = jnp.pad(p['ssm_a_log'].reshape(1, SSM_HEADS), ((0, 0), (0, LANES - SSM_HEADS)))
    dexp = jnp.repeat(p['ssm_d'], SSM_HEADDIM).reshape(1, SSM_WIDTH)
    return pl.pallas_call(
        _ssd_prompt_kernel,
        grid=(bp, nc),
        in_specs=[
            pl.BlockSpec((SSD_T, SSM_COLS), lambda b, n: (b * nc + n, SSM_OFF // SSM_COLS)),
            pl.BlockSpec((SSD_T, SSD_T), full),
            pl.BlockSpec((CONV_W, SSM_CONV_DIM), full),
            pl.BlockSpec((1, SSM_CONV_DIM), full),
            pl.BlockSpec((1, LANES), full),
            pl.BlockSpec((1, LANES), full),
            pl.BlockSpec((1, SSM_WIDTH), full),
            pl.BlockSpec((1, SSM_WIDTH), full),
        ],
        out_specs=[
            pl.BlockSpec((SSD_T, SSM_WIDTH), lambda b, n: (b * nc + n, 0)),
            pl.BlockSpec((None, SSM_HEADS, SSM_HEADDIM, SSM_DSTATE), lambda b, n: (b, 0, 0, 0)),
            pl.BlockSpec((None, CONV_W - 1, SSM_CONV_DIM), lambda b, n: (b, 0, 0)),
        ],
        out_shape=[
            jax.ShapeDtypeStruct((m_all, SSM_WIDTH), BF16),
            jax.ShapeDtypeStruct((bp, SSM_HEADS, SSM_HEADDIM, SSM_DSTATE), F32),
            jax.ShapeDtypeStruct((bp, CONV_W - 1, SSM_CONV_DIM), F32),
        ],
        scratch_shapes=[
            pltpu.VMEM((SSD_T + SUBLANES, SSM_CONV_DIM), F32),
            pltpu.VMEM((SUBLANES, SSM_CONV_DIM), F32),
            pltpu.VMEM((SSM_HEADS, SSM_HEADDIM, SSM_DSTATE), F32),
            pltpu.VMEM((SSD_T, SSM_WIDTH), F32),
        ],
        compiler_params=pltpu.CompilerParams(
            dimension_semantics=("parallel", "arbitrary"), vmem_limit_bytes=VMEM_LIMIT),
        name="ssd_prompt",
    )(proj, _tri(SSD_T), p['ssm_conv_w'], p['ssm_conv_b'].reshape(1, SSM_CONV_DIM), dtb, alog, dexp,
      p['ssm_norm_w'].reshape(1, SSM_WIDTH))


HG_T = 256

_NT = (((1,), (1,)), ((), ()))
_TN = (((0,), (0,)), ((), ()))


def _hgrn_prompt_kernel(q_ref, f_ref, v_ref, g_ref, lb_ref, nw_ref, tri_ref, o_ref, st_ref, s_ref):
    n = pl.program_id(2)
    t, c = HG_T, HG_CHUNK
    nch = t // c

    @pl.when(n == 0)
    def _():
        s_ref[...] = jnp.zeros_like(s_ref)

    lb = lb_ref[...]
    fz = f_ref[...]
    logf = jnp.log(lb + (1.0 - lb) * jax.nn.sigmoid(fz))
    kk = (1.0 - lb) * jax.nn.sigmoid(-fz)
    qh = _silu(q_ref[...])
    v = v_ref[...].astype(BF16)
    b = _cumsum_rows(tri_ref[...], logf)
    b3 = b.reshape(nch, c, HG_DK)
    b_mid = b3[:, c // 2:c // 2 + 1, :]
    b_last = b3[:, c - 1:c, :]
    qh3 = qh.reshape(nch, c, HG_DK)
    kk3 = kk.reshape(nch, c, HG_DK)
    q_in = (qh3 * jnp.exp(b3 - b_mid)).reshape(t, HG_DK).astype(BF16)
    k_in = (kk3 * jnp.exp(b_mid - b3)).reshape(t, HG_DK).astype(BF16)
    row = lax.broadcasted_iota(jnp.int32, (t, t), 0)
    col = lax.broadcasted_iota(jnp.int32, (t, t), 1)
    keep = (row >= col) & (row // c == col // c)
    a = jnp.where(keep, lax.dot_general(q_in, k_in, _NT, preferred_element_type=F32), 0.0)
    o_intra = jnp.dot(a.astype(BF16), v, preferred_element_type=F32)
    kd = (kk3 * jnp.exp(b_last - b3)).astype(BF16)
    qe = (qh3 * jnp.exp(b3)).astype(BF16)
    decay = jnp.exp(b_last)

    outs = []
    for ci in range(nch):
        s_t = s_ref[...]
        o_inter = lax.dot_general(qe[ci], s_t.astype(BF16), _NT, preferred_element_type=F32)
        outs.append(o_intra[ci * c:(ci + 1) * c, :] + o_inter)
        ds_t = lax.dot_general(v[ci * c:(ci + 1) * c, :], kd[ci], _TN, preferred_element_type=F32)
        s_ref[...] = s_t * decay[ci] + ds_t
    o = jnp.concatenate(outs, axis=0)
    o = o * lax.rsqrt(jnp.mean(o * o, axis=-1, keepdims=True) + EPS)
    o_ref[...] = (o * nw_ref[...] * _silu(g_ref[...])).astype(BF16)

    @pl.when(n == pl.num_programs(2) - 1)
    def _():
        st_ref[...] = s_ref[...].T


def _tri_chunks(t, c):
    r = np.arange(t)
    return jnp.asarray((r[:, None] >= r[None, :]) & (r[:, None] // c == r[None, :] // c), BF16)


def _hgrn_prompt(proj, p, lb, bp, lp, m_all):
    nc = lp // HG_T
    def stream(kind):
        def imap(b, h, n):
            idx = kind * HG_HEADS + h
            return b * nc + n, jnp.where(idx == 0, HG0_OFF // LANES, HG1_OFF // LANES + idx - 1)
        return pl.BlockSpec((HG_T, LANES), imap)

    per_head = pl.BlockSpec((1, LANES), lambda b, h, n: (0, h))
    return pl.pallas_call(
        _hgrn_prompt_kernel,
        grid=(bp, HG_HEADS, nc),
        in_specs=[stream(0), stream(1), stream(2), stream(3), per_head, per_head,
                  pl.BlockSpec((HG_T, HG_T), lambda b, h, n: (0, 0))],
        out_specs=[
            pl.BlockSpec((HG_T, LANES), lambda b, h, n: (b * nc + n, h)),
            pl.BlockSpec((None, None, HG_DK, HG_DV), lambda b, h, n: (b, h, 0, 0)),
        ],
        out_shape=[
            jax.ShapeDtypeStruct((m_all, HG_WIDTH), BF16),
            jax.ShapeDtypeStruct((bp, HG_HEADS, HG_DK, HG_DV), F32),
        ],
        scratch_shapes=[pltpu.VMEM((HG_DV, HG_DK), F32)],
        compiler_params=pltpu.CompilerParams(
            dimension_semantics=("parallel", "parallel", "arbitrary"), vmem_limit_bytes=VMEM_LIMIT),
        name="hgrn_prompt",
    )(proj, proj, proj, proj, lb.reshape(1, HG_WIDTH), p['hg_norm_w'].reshape(1, HG_WIDTH),
      _tri_chunks(HG_T, HG_CHUNK))


RG_T = 256


def _scan_rows(a, u):
    t = a.shape[0]
    rows = lax.broadcasted_iota(jnp.int32, a.shape, 0)
    s = 1
    while s < t:
        keep = rows >= s
        a_sh = jnp.where(keep, pltpu.roll(a, s, axis=0), 1.0)
        u_sh = jnp.where(keep, pltpu.roll(u, s, axis=0), 0.0)
        u = u + a * u_sh
        a = a * a_sh
        s *= 2
    return a, u


def _rg_gates(xc, wa_ref, ba_ref, wx_ref, bx_ref, lam_ref):
    xb = xc.astype(BF16)
    ra, ri = [], []
    for h in range(RG_HEADS):
        sl = slice(h * RG_HEAD_DIM, (h + 1) * RG_HEAD_DIM)
        ra.append(jnp.dot(xb[:, sl], wa_ref[h], preferred_element_type=F32))
        ri.append(jnp.dot(xb[:, sl], wx_ref[h], preferred_element_type=F32))
    r = jax.nn.sigmoid(jnp.concatenate(ra, axis=1) + ba_ref[...])
    ig = jax.nn.sigmoid(jnp.concatenate(ri, axis=1) + bx_ref[...])
    log_a = -RG_C * r * _softplus(-lam_ref[...])
    a = jnp.exp(log_a)
    mult = jnp.sqrt(_neg_expm1(2.0 * log_a))
    return a, mult, ig


def _gelu_tanh(x):
    return 0.5 * x * (1.0 + jnp.tanh(math.sqrt(2.0 / math.pi) * (x + 0.044715 * (x * x * x))))


def _rglru_prompt_kernel(x_ref, gate_ref, cw_ref, cb_ref, wa_ref, ba_ref, wx_ref, bx_ref, lam_ref,
                         o_ref, h_ref, cst_ref, xp_ref, tail_ref, hprev_ref):
    n = pl.program_id(1)
    t = RG_T

    @pl.when(n == 0)
    def _():
        hprev_ref[...] = jnp.zeros_like(hprev_ref)

    x = x_ref[...]
    xc = _conv4(xp_ref, tail_ref, x, cw_ref, cb_ref, n == 0)
    cst_ref[...] = x_ref[t - (CONV_W - 1):t, :]
    a, mult, ig = _rg_gates(xc, wa_ref, ba_ref, wx_ref, bx_ref, lam_ref)
    rows = lax.broadcasted_iota(jnp.int32, a.shape, 0)
    mult = jnp.where((rows == 0) & (n == 0), 1.0, mult)
    a_cum, h = _scan_rows(a, mult * ig * xc)
    h = h + a_cum * hprev_ref[...]
    hprev_ref[...] = h[t - 1:t, :]
    h_ref[...] = h[t - 1:t, :]
    o_ref[...] = (h * _gelu_tanh(gate_ref[...])).astype(BF16)


def _rglru_prompt(proj, p, bp, lp, m_all):
    nc = lp // RG_T
    full2 = lambda b, n: (0, 0)
    full3 = lambda b, n: (0, 0, 0)
    row = lambda a: a.reshape(1, RG_WIDTH)
    return pl.pallas_call(
        _rglru_prompt_kernel,
        grid=(bp, nc),
        in_specs=[
            pl.BlockSpec((RG_T, RG_WIDTH), lambda b, n: (b * nc + n, RG_OFF // RG_WIDTH)),
            pl.BlockSpec((RG_T, RG_WIDTH), lambda b, n: (b * nc + n, RG_OFF // RG_WIDTH + 1)),
            pl.BlockSpec((CONV_W, RG_WIDTH), full2),
            pl.BlockSpec((1, RG_WIDTH), full2),
            pl.BlockSpec((RG_HEADS, RG_HEAD_DIM, RG_HEAD_DIM), full3),
            pl.BlockSpec((1, RG_WIDTH), full2),
            pl.BlockSpec((RG_HEADS, RG_HEAD_DIM, RG_HEAD_DIM), full3),
            pl.BlockSpec((1, RG_WIDTH), full2),
            pl.BlockSpec((1, RG_WIDTH), full2),
        ],
        out_specs=[
            pl.BlockSpec((RG_T, RG_WIDTH), lambda b, n: (b * nc + n, 0)),
            pl.BlockSpec((None, 1, RG_WIDTH), lambda b, n: (b, 0, 0)),
            pl.BlockSpec((None, CONV_W - 1, RG_WIDTH), lambda b, n: (b, 0, 0)),
        ],
        out_shape=[
            jax.ShapeDtypeStruct((m_all, RG_WIDTH), BF16),
            jax.ShapeDtypeStruct((bp, 1, RG_WIDTH), F32),
            jax.ShapeDtypeStruct((bp, CONV_W - 1, RG_WIDTH), F32),
        ],
        scratch_shapes=[
            pltpu.VMEM((RG_T + SUBLANES, RG_WIDTH), F32),
            pltpu.VMEM((SUBLANES, RG_WIDTH), F32),
            pltpu.VMEM((1, RG_WIDTH), F32),
        ],
        compiler_params=pltpu.CompilerParams(
            dimension_semantics=("parallel", "arbitrary"), vmem_limit_bytes=VMEM_LIMIT),
        name="rglru_prompt",
    )(proj, proj, p['rg_conv_w'], row(p['rg_conv_b']), p['rg_wa'].astype(BF16), row(p['rg_ba']),
      p['rg_wx'].astype(BF16), row(p['rg_bx']), row(p['rg_lambda']))


def _split_cols(t, sizes):
    return jnp.split(t, np.cumsum(sizes)[:-1].tolist(), axis=-1)


def _chunk_len(L, c):
    return c if L % c == 0 else L


def _causal_dwconv(x, prefix, w, b):
    L = x.shape[1]
    xp = jnp.concatenate([prefix.astype(x.dtype), x], axis=1)
    y = b
    for k in range(CONV_W):
        y = y + w[k] * xp[:, k:k + L]
    return y, xp[:, L:]


def _hgrn2_mix(q, fz, v_in, g, lb, s0, norm_w):
    Bn, L, _ = q.shape
    C = _chunk_len(L, HG_CHUNK)
    N = L // C

    def heads(t, d):
        return t.astype(F32).reshape(Bn, N, C, HG_HEADS, d).transpose(0, 3, 1, 2, 4)

    logf = heads(jnp.log(lb + (1.0 - lb) * jax.nn.sigmoid(fz.astype(F32))), HG_DK)
    k = -jnp.expm1(logf)
    qh = heads(jax.nn.silu(q.astype(F32)), HG_DK)
    v = heads(v_in, HG_DV)
    b = jnp.cumsum(logf, axis=3)
    b_ref = b[:, :, :, C // 2:C // 2 + 1]
    b_last = b[:, :, :, C - 1:]
    q_in = qh * jnp.exp(b - b_ref)
    k_in = k * jnp.exp(b_ref - b)
    causal = jnp.tril(jnp.ones((C, C), dtype=bool))
    a_intra = jnp.where(causal, jnp.einsum('bhnik,bhnjk->bhnij', q_in, k_in), 0.0)
    o = jnp.einsum('bhnij,bhnjv->bhniv', a_intra, v)
    ds = jnp.einsum('bhnjk,bhnjv->nbhkv', k * jnp.exp(b_last - b), v)
    decay = jnp.exp(b_last[:, :, :, 0]).transpose(2, 0, 1, 3)

    def step(s, inp):
        d, dsn = inp
        return d[..., None] * s + dsn, s

    s_final, s_prev = lax.scan(step, s0.astype(F32), (decay, ds))
    o = o + jnp.einsum('bhnik,nbhkv->bhniv', qh * jnp.exp(b), s_prev)
    o = o.transpose(0, 2, 3, 1, 4).reshape(Bn, L, HG_HEADS, HG_DV)
    o = o * lax.rsqrt(jnp.mean(o * o, axis=-1, keepdims=True) + EPS)
    o = o.reshape(Bn, L, HG_WIDTH) * norm_w.astype(F32) * jax.nn.silu(g.astype(F32))
    return o, s_final


def _rglru_mix(x_in, gate, prefix, h0, conv_w, conv_b, wa, ba, wx, bx, lam, reset_first):
    xc, new_prefix = _causal_dwconv(x_in, prefix, conv_w, conv_b)
    Bn, L, _ = xc.shape
    xb = xc.reshape(Bn, L, RG_HEADS, RG_HEAD_DIM)
    r = jax.nn.sigmoid((jnp.einsum('blhi,hij->blhj', xb, wa) + ba).astype(F32)).reshape(Bn, L, RG_WIDTH)
    ig = jax.nn.sigmoid((jnp.einsum('blhi,hij->blhj', xb, wx) + bx).astype(F32)).reshape(Bn, L, RG_WIDTH)
    log_a = -RG_C * r * jax.nn.softplus(-lam.astype(F32))
    a = jnp.exp(log_a)
    mult = jnp.sqrt(-jnp.expm1(2.0 * log_a))
    if reset_first:
        mult = mult.at[:, 0].set(1.0)
    u = mult * ig * xc.astype(F32)
    u = u.at[:, 0].add(a[:, 0] * h0.astype(F32))

    def combine(lhs, rhs):
        a1, b1 = lhs
        a2, b2 = rhs
        return a1 * a2, a2 * b1 + b2

    _, h = lax.associative_scan(combine, (a, u), axis=1)
    y = h * jax.nn.gelu(gate.astype(F32))
    return y, h[:, -1], new_prefix


def _ssd_mix(z, xbc_in, dt_raw, prefix, s0, conv_w, conv_b, dt_bias, a_log, d_skip, norm_w):
    xbc, new_prefix = _causal_dwconv(xbc_in, prefix, conv_w, conv_b)
    xbc = jax.nn.silu(xbc.astype(F32))
    Bn, L, _ = xbc.shape
    C = _chunk_len(L, SSM_CHUNK)
    N = L // C
    xs, bm, cm = _split_cols(xbc, (SSM_WIDTH, SSM_GROUPS * SSM_DSTATE, SSM_GROUPS * SSM_DSTATE))
    xh = xs.reshape(Bn, N, C, SSM_HEADS, SSM_HEADDIM)
    bh = jnp.repeat(bm.reshape(Bn, N, C, SSM_GROUPS, SSM_DSTATE), HEADS_PER_GROUP, axis=3)
    ch = jnp.repeat(cm.reshape(Bn, N, C, SSM_GROUPS, SSM_DSTATE), HEADS_PER_GROUP, axis=3)
    dt = jax.nn.softplus(dt_raw.astype(F32) + dt_bias.astype(F32)).reshape(Bn, N, C, SSM_HEADS)
    a = (dt * -jnp.exp(a_log.astype(F32))).transpose(0, 3, 1, 2)
    cs = jnp.cumsum(a, axis=-1)
    causal = jnp.tril(jnp.ones((C, C), dtype=bool))
    seg = jnp.where(causal, cs[..., :, None] - cs[..., None, :], 0.0)
    lmat = jnp.where(causal, jnp.exp(seg), 0.0)
    xdt = xh * dt[..., None]
    scores = jnp.einsum('bnihs,bnjhs->bhnij', ch, bh) * lmat
    y = jnp.einsum('bhnij,bnjhp->bnihp', scores, xdt)
    states = jnp.einsum('bnjhs,bhnj,bnjhp->nbhps', bh, jnp.exp(cs[..., -1:] - cs), xdt)
    decay = jnp.exp(cs[..., -1]).transpose(2, 0, 1)

    def step(s, inp):
        d, st = inp
        return d[:, :, None, None] * s + st, s

    s_final, s_prev = lax.scan(step, s0.astype(F32), (decay, states))
    y = y + jnp.einsum('bnihs,nbhps,bhni->bnihp', ch, s_prev, jnp.exp(cs))
    y = y + d_skip.astype(F32)[:, None] * xh
    y = y.reshape(Bn, L, SSM_WIDTH) * jax.nn.silu(z.astype(F32))
    yg = y.reshape(Bn, L, SSM_GROUPS, SSM_WIDTH // SSM_GROUPS)
    yg = yg * lax.rsqrt(jnp.mean(yg * yg, axis=-1, keepdims=True) + EPS)
    y = yg.reshape(Bn, L, SSM_WIDTH) * norm_w.astype(F32)
    return y, s_final, new_prefix


def _split_proj(proj):
    ssm = _split_cols(proj[..., SSM_OFF:SSM_OFF + SSM_COLS], (SSM_WIDTH, SSM_CONV_DIM, SSM_HEADS, LANES - SSM_HEADS))
    hgp = _split_cols(jnp.concatenate([proj[..., HG0_OFF:RG_OFF], proj[..., HG1_OFF:]], axis=-1), (HG_WIDTH,) * 4)
    rgp = _split_cols(proj[..., RG_OFF:HG1_OFF], (RG_WIDTH,) * 2)
    return dict(sz=ssm[0], sxbc=ssm[1], sdt=ssm[2], q=hgp[0], fz=hgp[1], hi=hgp[2], hg=hgp[3],
                rx=rgp[0], rgate=rgp[1])


def _mix_group(proj, st, p, lb, reset_first):
    hg_s, rg_h, rg_c, ssm_s, ssm_c = st
    sp = _split_proj(proj)
    q, fz, hi, hg, rx, rgate, sz, sxbc, sdt = (sp[k] for k in ('q', 'fz', 'hi', 'hg', 'rx', 'rgate', 'sz', 'sxbc', 'sdt'))
    o_hg, hg_new = _hgrn2_mix(q, fz, hi, hg, lb, hg_s, p['hg_norm_w'])
    o_rg, rg_h_new, rg_c_new = _rglru_mix(rx, rgate, rg_c, rg_h, p['rg_conv_w'], p['rg_conv_b'],
                                          p['rg_wa'], p['rg_ba'], p['rg_wx'], p['rg_bx'],
                                          p['rg_lambda'], reset_first)
    o_ssm, ssm_new, ssm_c_new = _ssd_mix(sz, sxbc, sdt, ssm_c, ssm_s, p['ssm_conv_w'], p['ssm_conv_b'],
                                         p['ssm_dt_bias'], p['ssm_a_log'], p['ssm_d'], p['ssm_norm_w'])
    o = jnp.concatenate([o_hg, o_rg, o_ssm], axis=-1)
    return o, (hg_new, rg_h_new, rg_c_new, ssm_new, ssm_c_new)


def kernel(x_prompt, x_sample, state_hgrn, state_rglru, state_rglru_conv, state_ssm, state_ssm_conv, norm_g, ffn1_w_gate, ffn1_w_up, ffn1_w_down, ffn2_w_gate, ffn2_w_up, ffn2_w_down, w_in, w_out, hg_lb_logits, hg_norm_w, rg_conv_w, rg_conv_b, rg_wa, rg_ba, rg_wx, rg_bx, rg_lambda, ssm_conv_w, ssm_conv_b, ssm_dt_bias, ssm_a_log, ssm_d, ssm_norm_w):
    bp, lp, _ = x_prompt.shape
    bs, ls, _ = x_sample.shape
    mp = bp * lp
    ms = bs * ls
    assert (mp + ms) % ROW_TILE == 0 and D_FF % FF_TILE == 0 and D_IN_PAD % IN_TILE == 0

    lw = {
        'hg_norm_w': hg_norm_w, 'rg_conv_w': rg_conv_w, 'rg_conv_b': rg_conv_b, 'rg_wa': rg_wa,
        'rg_ba': rg_ba, 'rg_wx': rg_wx, 'rg_bx': rg_bx, 'rg_lambda': rg_lambda,
        'ssm_conv_w': ssm_conv_w, 'ssm_conv_b': ssm_conv_b, 'ssm_dt_bias': ssm_dt_bias,
        'ssm_a_log': ssm_a_log, 'ssm_d': ssm_d, 'ssm_norm_w': ssm_norm_w,
    }
    lb_cum = jnp.cumsum(jax.nn.softmax(hg_lb_logits.astype(F32), axis=0), axis=0)
    lower_bounds = lb_cum - lb_cum[:1]

    w1g, w1u, w1d = (w.astype(BF16) for w in (ffn1_w_gate, ffn1_w_up, ffn1_w_down))
    w2g, w2u, w2d = (w.astype(BF16) for w in (ffn2_w_gate, ffn2_w_up, ffn2_w_down))
    w_in_b = _permute_w_in(w_in)
    w_out_b = w_out.astype(BF16)

    sample_init = (state_hgrn, state_rglru, state_rglru_conv, state_ssm, state_ssm_conv)

    x = jnp.concatenate([x_prompt.reshape(mp, D_MODEL), x_sample.reshape(ms, D_MODEL)], axis=0)
    new_p = ([], [], [], [], [])
    new_s = ([], [], [], [], [])
    for l in range(DEPTH):
        g = norm_g[l].reshape(6, 1, D_MODEL)
        p = {name: arr[l] for name, arr in lw.items()}
        x = _ffn(x, g[0], g[1], w1g, w1u, w1d, l)
        proj = _inproj(x, g[2], w_in_b, l)
        o_hg, hg_new = _hgrn_prompt(proj, p, lower_bounds[l], bp, lp, mp + ms)
        o_rg, rg_h_new, rg_c_new = _rglru_prompt(proj, p, bp, lp, mp + ms)
        o_ssm, ssm_new, ssm_c_new = _ssd_prompt(proj, p, bp, lp, mp + ms)
        st_p = (hg_new, rg_h_new.reshape(bp, RG_WIDTH), rg_c_new, ssm_new, ssm_c_new)
        o_s, st_s = _mix_group(proj[mp:].reshape(bs, ls, D_IN_PAD), tuple(s[l] for s in sample_init),
                               p, lower_bounds[l], False)
        o_s = o_s.reshape(ms, D_MIX).astype(BF16)
        o_hg = o_hg.at[mp:].set(o_s[:, :HG_WIDTH])
        o_rg = o_rg.at[mp:].set(o_s[:, HG_WIDTH:HG_WIDTH + RG_WIDTH])
        o_ssm = o_ssm.at[mp:].set(o_s[:, HG_WIDTH + RG_WIDTH:])
        x = _outproj(x, o_hg, o_rg, o_ssm, g[3], w_out_b, l)
        x = _ffn(x, g[4], g[5], w2g, w2u, w2d, l)
        for acc, s in zip(new_p, st_p):
            acc.append(s)
        for acc, s in zip(new_s, st_s):
            acc.append(s)
    hg_p, rg_p, rgc_p, ssm_p, ssmc_p = (jnp.stack(a) for a in new_p)
    hg_s, rg_s, rgc_s, ssm_s, ssmc_s = (jnp.stack(a) for a in new_s)
    y_prompt = x[:mp].reshape(bp, lp, D_MODEL)
    y_sample = x[mp:].reshape(bs, ls, D_MODEL)
    return (y_prompt, y_sample, hg_p, hg_s, rg_p, rg_s, rgc_p, rgc_s, ssm_p, ssm_s, ssmc_p, ssmc_s)
```

```python
import functools
import math

import jax
import jax.numpy as jnp
import numpy as np
from jax import lax
from jax.experimental import pallas as pl
from jax.experimental.pallas import tpu as pltpu

F32 = jnp.float32
BF16 = jnp.bfloat16

D_MODEL = 2048
DEPTH = 2
EPS = 1e-6
CONV_W = 4
HG_HEADS = 4
HG_DK = 128
HG_DV = 128
HG_WIDTH = HG_HEADS * HG_DV
HG_CHUNK = 32
RG_HEADS = 6
RG_HEAD_DIM = 128
RG_WIDTH = RG_HEADS * RG_HEAD_DIM
RG_C = 8.0
SSM_HEADS = 12
SSM_HEADDIM = 64
SSM_WIDTH = SSM_HEADS * SSM_HEADDIM
SSM_GROUPS = 2
HEADS_PER_GROUP = SSM_HEADS // SSM_GROUPS
SSM_DSTATE = 128
SSM_CHUNK = 64
SSM_CONV_DIM = SSM_WIDTH + 2 * SSM_GROUPS * SSM_DSTATE
D_MIX = HG_WIDTH + RG_WIDTH + SSM_WIDTH
IN_SIZES = (HG_HEADS * HG_DK, HG_HEADS * HG_DK, HG_WIDTH, HG_WIDTH, RG_WIDTH, RG_WIDTH,
            SSM_WIDTH, SSM_CONV_DIM, SSM_HEADS)
D_IN_PROJ = sum(IN_SIZES)
D_FF = 5632

LANES = 128
D_IN_PAD = -(-D_IN_PROJ // LANES) * LANES
ROW_TILE = 640
FF_TILE = 512
IN_TILE = 1152
VMEM_LIMIT = 56 * 1024 * 1024


def _rms(x, g):
    return x * lax.rsqrt(jnp.mean(x * x, axis=-1, keepdims=True) + EPS) * g


def _ffn_kernel(x_ref, gin_ref, gout_ref, wg_ref, wu_ref, wd_ref, o_ref, xn_ref):
    j = pl.program_id(1)

    @pl.when(j == 0)
    def _():
        xn_ref[...] = _rms(x_ref[...], gin_ref[...]).astype(BF16)
        o_ref[...] = jnp.zeros_like(o_ref)

    xn = xn_ref[...]
    g = jnp.dot(xn, wg_ref[...], preferred_element_type=F32)
    u = jnp.dot(xn, wu_ref[...], preferred_element_type=F32)
    h = (g * jax.nn.sigmoid(g) * u).astype(BF16)
    o_ref[...] += jnp.dot(h, wd_ref[...], preferred_element_type=F32)

    @pl.when(j == pl.num_programs(1) - 1)
    def _():
        o_ref[...] = x_ref[...] + 0.5 * _rms(o_ref[...], gout_ref[...])


def _ffn(x, g_in, g_out, wg, wu, wd, layer):
    m = x.shape[0]
    return pl.pallas_call(
        _ffn_kernel,
        grid=(m // ROW_TILE, D_FF // FF_TILE),
        in_specs=[
            pl.BlockSpec((ROW_TILE, D_MODEL), lambda i, j: (i, 0)),
            pl.BlockSpec((1, D_MODEL), lambda i, j: (0, 0)),
            pl.BlockSpec((1, D_MODEL), lambda i, j: (0, 0)),
            pl.BlockSpec((None, D_MODEL, FF_TILE), lambda i, j: (layer, 0, j)),
            pl.BlockSpec((None, D_MODEL, FF_TILE), lambda i, j: (layer, 0, j)),
            pl.BlockSpec((None, FF_TILE, D_MODEL), lambda i, j: (layer, j, 0)),
        ],
        out_specs=pl.BlockSpec((ROW_TILE, D_MODEL), lambda i, j: (i, 0)),
        out_shape=jax.ShapeDtypeStruct((m, D_MODEL), F32),
        scratch_shapes=[pltpu.VMEM((ROW_TILE, D_MODEL), BF16)],
        compiler_params=pltpu.CompilerParams(
            dimension_semantics=("parallel", "arbitrary"), vmem_limit_bytes=VMEM_LIMIT),
        name="ffn",
    )(x, g_in, g_out, wg, wu, wd)


def _inproj_kernel(x_ref, g_ref, w_ref, o_ref, xn_ref):
    @pl.when(pl.program_id(1) == 0)
    def _():
        xn_ref[...] = _rms(x_ref[...], g_ref[...]).astype(BF16)

    o_ref[...] = jnp.dot(xn_ref[...], w_ref[...], preferred_element_type=F32)


def _inproj(x, g, w_in, layer):
    m = x.shape[0]
    return pl.pallas_call(
        _inproj_kernel,
        grid=(m // ROW_TILE, D_IN_PAD // IN_TILE),
        in_specs=[
            pl.BlockSpec((ROW_TILE, D_MODEL), lambda i, j: (i, 0)),
            pl.BlockSpec((1, D_MODEL), lambda i, j: (0, 0)),
            pl.BlockSpec((None, D_MODEL, IN_TILE), lambda i, j: (layer, 0, j)),
        ],
        out_specs=pl.BlockSpec((ROW_TILE, IN_TILE), lambda i, j: (i, j)),
        out_shape=jax.ShapeDtypeStruct((m, D_IN_PAD), F32),
        scratch_shapes=[pltpu.VMEM((ROW_TILE, D_MODEL), BF16)],
        compiler_params=pltpu.CompilerParams(
            dimension_semantics=("parallel", "arbitrary"), vmem_limit_bytes=VMEM_LIMIT),
        name="inproj",
    )(x, g, w_in)


def _outproj_kernel(x_ref, ohg_ref, org_ref, ossm_ref, g_ref, w_ref, y_ref):
    r0, r1 = HG_WIDTH, HG_WIDTH + RG_WIDTH
    m = jnp.dot(ohg_ref[...], w_ref[0:r0, :], preferred_element_type=F32)
    m += jnp.dot(org_ref[...], w_ref[r0:r1, :], preferred_element_type=F32)
    m += jnp.dot(ossm_ref[...], w_ref[r1:, :], preferred_element_type=F32)
    y_ref[...] = x_ref[...] + _rms(m, g_ref[...])


def _outproj(x, o_hg, o_rg, o_ssm, g, w_out, layer):
    m = x.shape[0]
    return pl.pallas_call(
        _outproj_kernel,
        grid=(m // ROW_TILE,),
        in_specs=[
            pl.BlockSpec((ROW_TILE, D_MODEL), lambda i: (i, 0)),
            pl.BlockSpec((ROW_TILE, HG_WIDTH), lambda i: (i, 0)),
            pl.BlockSpec((ROW_TILE, RG_WIDTH), lambda i: (i, 0)),
            pl.BlockSpec((ROW_TILE, SSM_WIDTH), lambda i: (i, 0)),
            pl.BlockSpec((1, D_MODEL), lambda i: (0, 0)),
            pl.BlockSpec((None, D_MIX, D_MODEL), lambda i: (layer, 0, 0)),
        ],
        out_specs=pl.BlockSpec((ROW_TILE, D_MODEL), lambda i: (i, 0)),
        out_shape=jax.ShapeDtypeStruct((m, D_MODEL), F32),
        compiler_params=pltpu.CompilerParams(
            dimension_semantics=("parallel",), vmem_limit_bytes=VMEM_LIMIT),
        name="outproj",
    )(x, o_hg, o_rg, o_ssm, g, w_out)


SSM_OFF = 0
SSM_COLS = SSM_WIDTH + SSM_CONV_DIM + LANES
HG0_OFF = SSM_OFF + SSM_COLS
RG_OFF = HG0_OFF + LANES
HG1_OFF = RG_OFF + 2 * RG_WIDTH
assert RG_OFF % RG_WIDTH == 0 and HG1_OFF % LANES == 0
assert HG1_OFF + 4 * HG_WIDTH - LANES == D_IN_PAD


def _permute_w_in(w):
    c = np.cumsum((0,) + IN_SIZES)
    hg, rg, ssm = w[..., c[0]:c[4]], w[..., c[4]:c[6]], w[..., c[6]:c[9]]
    pad = jnp.zeros(w.shape[:-1] + (LANES - SSM_HEADS,), w.dtype)
    return jnp.concatenate([ssm, pad, hg[..., :LANES], rg, hg[..., LANES:]], axis=-1).astype(BF16)

SUBLANES = 8
SSD_T = 256
NEG_BIG = -1e30


def _split3(x):
    hi = x.astype(BF16)
    r = x - hi.astype(F32)
    mid = r.astype(BF16)
    lo = (r - mid.astype(F32)).astype(BF16)
    return hi, mid, lo


def _cumsum_rows(tri, x):
    w = x.shape[1]
    parts = jnp.concatenate(_split3(x), axis=1)
    r = jnp.dot(tri, parts, preferred_element_type=F32)
    return r[:, :w] + r[:, w:2 * w] + r[:, 2 * w:]


def _silu(x):
    return x * jax.nn.sigmoid(x)


def _neg_expm1(x):
    series = -x * (1.0 + x * (1 / 2 + x * (1 / 6 + x * (1 / 24 + x * (1 / 120 + x * (1 / 720 + x / 5040))))))
    return jnp.where(x > -0.25, series, 1.0 - jnp.exp(x))


def _softplus(x):
    return jnp.maximum(x, 0.0) + jnp.log(1.0 + jnp.exp(-jnp.abs(x)))


def _conv4(xp_ref, tail_ref, x, w_ref, b_ref, first):
    t = x.shape[0]

    @pl.when(first)
    def _():
        tail_ref[...] = jnp.zeros_like(tail_ref)

    xp_ref[0:SUBLANES, :] = tail_ref[...]
    xp_ref[SUBLANES:, :] = x
    tail_ref[...] = x[t - SUBLANES:, :]
    y = b_ref[...] + w_ref[CONV_W - 1:CONV_W, :] * x
    for k in range(CONV_W - 1):
        off = SUBLANES - (CONV_W - 1) + k
        y = y + w_ref[k:k + 1, :] * xp_ref[off:off + t, :]
    return y


def _ssd_prompt_kernel(p_ref, tri_ref, cw_ref, cb_ref, dtb_ref, alog_ref, dexp_ref, nw_ref,
                       o_ref, st_ref, cst_ref, xp_ref, tail_ref, s_ref, y_ref):
    n = pl.program_id(1)
    t = SSD_T
    z = p_ref[:, 0:SSM_WIDTH]
    xbc_raw = p_ref[:, SSM_WIDTH:SSM_WIDTH + SSM_CONV_DIM]
    dt_raw = p_ref[:, SSM_WIDTH + SSM_CONV_DIM:]

    @pl.when(n == 0)
    def _():
        s_ref[...] = jnp.zeros_like(s_ref)

    xbc = _silu(_conv4(xp_ref, tail_ref, xbc_raw, cw_ref, cb_ref, n == 0))
    cst_ref[...] = p_ref[t - (CONV_W - 1):t, SSM_WIDTH:SSM_WIDTH + SSM_CONV_DIM]
    xs = xbc[:, :SSM_WIDTH]
    gs = SSM_GROUPS * SSM_DSTATE
    bm = xbc[:, SSM_WIDTH:SSM_WIDTH + gs].astype(BF16)
    cm = xbc[:, SSM_WIDTH + gs:].astype(BF16)

    dt = _softplus(dt_raw + dtb_ref[...])
    a = dt * -jnp.exp(alog_ref[...])
    cs = _cumsum_rows(tri_ref[...], a)
    cs_t = cs.T
    cs_last = cs[t - 1:t, :]
    e_cs = jnp.exp(cs)
    e_rem = jnp.exp(cs_last - cs)
    e_last = jnp.exp(cs_last)
    row = lax.broadcasted_iota(jnp.int32, (t, t), 0)
    col = lax.broadcasted_iota(jnp.int32, (t, t), 1)
    causal = row >= col

    for g in range(SSM_GROUPS):
        bg = bm[:, g * SSM_DSTATE:(g + 1) * SSM_DSTATE]
        cg = cm[:, g * SSM_DSTATE:(g + 1) * SSM_DSTATE]
        cb = lax.dot_general(cg, bg, (((1,), (1,)), ((), ())), preferred_element_type=F32)
        for hh in range(HEADS_PER_GROUP):
            h = g * HEADS_PER_GROUP + hh
            sl = slice(h * SSM_HEADDIM, (h + 1) * SSM_HEADDIM)
            seg = jnp.where(causal, cs[:, h:h + 1] - cs_t[h:h + 1, :], NEG_BIG)
            scores = (cb * jnp.exp(seg)).astype(BF16)
            xdt = xs[:, sl] * dt[:, h:h + 1]
            s_prev = s_ref[h]
            y = jnp.dot(scores, xdt.astype(BF16), preferred_element_type=F32)
            y_in = lax.dot_general(cg, s_prev.astype(BF16), (((1,), (1,)), ((), ())),
                                   preferred_element_type=F32)
            y_ref[:, sl] = y + e_cs[:, h:h + 1] * y_in
            w = (xdt * e_rem[:, h:h + 1]).astype(BF16)
            st = lax.dot_general(w, bg, (((0,), (0,)), ((), ())), preferred_element_type=F32)
            s_ref[h] = e_last[:, h:h + 1] * s_prev + st

    y = (y_ref[...] + dexp_ref[...] * xs) * _silu(z)
    gw = SSM_WIDTH // SSM_GROUPS
    for g in range(SSM_GROUPS):
        yg = y[:, g * gw:(g + 1) * gw]
        yg = yg * lax.rsqrt(jnp.mean(yg * yg, axis=-1, keepdims=True) + EPS)
        o_ref[:, g * gw:(g + 1) * gw] = (yg * nw_ref[:, g * gw:(g + 1) * gw]).astype(BF16)

    @pl.when(n == pl.num_programs(1) - 1)
    def _():
        st_ref[...] = s_ref[...]


def _tri(t):
    return jnp.tril(jnp.ones((t, t), F32)).astype(BF16)


def _ssd_prompt(proj, p, bp, lp, m_all):
    nc = lp // SSD_T
    full = lambda b, n: (0, 0)
    dtb = jnp.pad(p['ssm_dt_bias'].reshape(1, SSM_HEADS), ((0, 0), (0, LANES - SSM_HEADS)))
    alog = jnp.pad(p['ssm_a_log'].reshape(1, SSM_HEADS), ((0, 0), (0, LANES - SSM_HEADS)))
    dexp = jnp.repeat(p['ssm_d'], SSM_HEADDIM).reshape(1, SSM_WIDTH)
    return pl.pallas_call(
        _ssd_prompt_kernel,
        grid=(bp, nc),
        in_specs=[
            pl.BlockSpec((SSD_T, SSM_COLS), lambda b, n: (b * nc + n, SSM_OFF // SSM_COLS)),
            pl.BlockSpec((SSD_T, SSD_T), full),
            pl.BlockSpec((CONV_W, SSM_CONV_DIM), full),
            pl.BlockSpec((1, SSM_CONV_DIM), full),
            pl.BlockSpec((1, LANES), full),
            pl.BlockSpec((1, LANES), full),
            pl.BlockSpec((1, SSM_WIDTH), full),
            pl.BlockSpec((1, SSM_WIDTH), full),
        ],
        out_specs=[
            pl.BlockSpec((SSD_T, SSM_WIDTH), lambda b, n: (b * nc + n, 0)),
            pl.BlockSpec((None, SSM_HEADS, SSM_HEADDIM, SSM_DSTATE), lambda b, n: (b, 0, 0, 0)),
            pl.BlockSpec((None, CONV_W - 1, SSM_CONV_DIM), lambda b, n: (b, 0, 0)),
        ],
        out_shape=[
            jax.ShapeDtypeStruct((m_all, SSM_WIDTH), BF16),
            jax.ShapeDtypeStruct((bp, SSM_HEADS, SSM_HEADDIM, SSM_DSTATE), F32),
            jax.ShapeDtypeStruct((bp, CONV_W - 1, SSM_CONV_DIM), F32),
        ],
        scratch_shapes=[
            pltpu.VMEM((SSD_T + SUBLANES, SSM_CONV_DIM), F32),
            pltpu.VMEM((SUBLANES, SSM_CONV_DIM), F32),
            pltpu.VMEM((SSM_HEADS, SSM_HEADDIM, SSM_DSTATE), F32),
            pltpu.VMEM((SSD_T, SSM_WIDTH), F32),
        ],
        compiler_params=pltpu.CompilerParams(
            dimension_semantics=("parallel", "arbitrary"), vmem_limit_bytes=VMEM_LIMIT),
        name="ssd_prompt",
    )(proj, _tri(SSD_T), p['ssm_conv_w'], p['ssm_conv_b'].reshape(1, SSM_CONV_DIM), dtb, alog, dexp,
      p['ssm_norm_w'].reshape(1, SSM_WIDTH))


HG_T = 256

_NT = (((1,), (1,)), ((), ()))
_TN = (((0,), (0,)), ((), ()))


def _hgrn_prompt_kernel(q_ref, f_ref, v_ref, g_ref, lb_ref, nw_ref, tri_ref, o_ref, st_ref, s_ref):
    n = pl.program_id(2)
    t, c = HG_T, HG_CHUNK
    nch = t // c

    @pl.when(n == 0)
    def _():
        s_ref[...] = jnp.zeros_like(s_ref)

    lb = lb_ref[...]
    fz = f_ref[...]
    logf = jnp.log(lb + (1.0 - lb) * jax.nn.sigmoid(fz))
    kk = (1.0 - lb) * jax.nn.sigmoid(-fz)
    qh = _silu(q_ref[...])
    v = v_ref[...].astype(BF16)
    b = _cumsum_rows(tri_ref[...], logf)
    b3 = b.reshape(nch, c, HG_DK)
    b_mid = b3[:, c // 2:c // 2 + 1, :]
    b_last = b3[:, c - 1:c, :]
    qh3 = qh.reshape(nch, c, HG_DK)
    kk3 = kk.reshape(nch, c, HG_DK)
    q_in = (qh3 * jnp.exp(b3 - b_mid)).reshape(t, HG_DK).astype(BF16)
    k_in = (kk3 * jnp.exp(b_mid - b3)).reshape(t, HG_DK).astype(BF16)
    row = lax.broadcasted_iota(jnp.int32, (t, t), 0)
    col = lax.broadcasted_iota(jnp.int32, (t, t), 1)
    keep = (row >= col) & (row // c == col // c)
    a = jnp.where(keep, lax.dot_general(q_in, k_in, _NT, preferred_element_type=F32), 0.0)
    o_intra = jnp.dot(a.astype(BF16), v, preferred_element_type=F32)
    kd = (kk3 * jnp.exp(b_last - b3)).astype(BF16)
    qe = (qh3 * jnp.exp(b3)).astype(BF16)
    decay = jnp.exp(b_last)

    outs = []
    for ci in range(nch):
        s_t = s_ref[...]
        o_inter = lax.dot_general(qe[ci], s_t.astype(BF16), _NT, preferred_element_type=F32)
        outs.append(o_intra[ci * c:(ci + 1) * c, :] + o_inter)
        ds_t = lax.dot_general(v[ci * c:(ci + 1) * c, :], kd[ci], _TN, preferred_element_type=F32)
        s_ref[...] = s_t * decay[ci] + ds_t
    o = jnp.concatenate(outs, axis=0)
    o = o * lax.rsqrt(jnp.mean(o * o, axis=-1, keepdims=True) + EPS)
    o_ref[...] = (o * nw_ref[...] * _silu(g_ref[...])).astype(BF16)

    @pl.when(n == pl.num_programs(2) - 1)
    def _():
        st_ref[...] = s_ref[...].T


def _tri_chunks(t, c):
    r = np.arange(t)
    return jnp.asarray((r[:, None] >= r[None, :]) & (r[:, None] // c == r[None, :] // c), BF16)


def _hgrn_prompt(proj, p, lb, bp, lp, m_all):
    nc = lp // HG_T
    def stream(kind):
        def imap(b, h, n):
            idx = kind * HG_HEADS + h
            return b * nc + n, jnp.where(idx == 0, HG0_OFF // LANES, HG1_OFF // LANES + idx - 1)
        return pl.BlockSpec((HG_T, LANES), imap)

    per_head = pl.BlockSpec((1, LANES), lambda b, h, n: (0, h))
    return pl.pallas_call(
        _hgrn_prompt_kernel,
        grid=(bp, HG_HEADS, nc),
        in_specs=[stream(0), stream(1), stream(2), stream(3), per_head, per_head,
                  pl.BlockSpec((HG_T, HG_T), lambda b, h, n: (0, 0))],
        out_specs=[
            pl.BlockSpec((HG_T, LANES), lambda b, h, n: (b * nc + n, h)),
            pl.BlockSpec((None, None, HG_DK, HG_DV), lambda b, h, n: (b, h, 0, 0)),
        ],
        out_shape=[
            jax.ShapeDtypeStruct((m_all, HG_WIDTH), BF16),
            jax.ShapeDtypeStruct((bp, HG_HEADS, HG_DK, HG_DV), F32),
        ],
        scratch_shapes=[pltpu.VMEM((HG_DV, HG_DK), F32)],
        compiler_params=pltpu.CompilerParams(
            dimension_semantics=("parallel", "parallel", "arbitrary"), vmem_limit_bytes=VMEM_LIMIT),
        name="hgrn_prompt",
    )(proj, proj, proj, proj, lb.reshape(1, HG_WIDTH), p['hg_norm_w'].reshape(1, HG_WIDTH),
      _tri_chunks(HG_T, HG_CHUNK))


RG_T = 256


def _scan_rows(a, u):
    t = a.shape[0]
    rows = lax.broadcasted_iota(jnp.int32, a.shape, 0)
    s = 1
    while s < t:
        keep = rows >= s
        a_sh = jnp.where(keep, pltpu.roll(a, s, axis=0), 1.0)
        u_sh = jnp.where(keep, pltpu.roll(u, s, axis=0), 0.0)
        u = u + a * u_sh
        a = a * a_sh
        s *= 2
    return a, u


def _rg_gates(xc, wa_ref, ba_ref, wx_ref, bx_ref, lam_ref):
    xb = xc.astype(BF16)
    ra, ri = [], []
    for h in range(RG_HEADS):
        sl = slice(h * RG_HEAD_DIM, (h + 1) * RG_HEAD_DIM)
        ra.append(jnp.dot(xb[:, sl], wa_ref[h], preferred_element_type=F32))
        ri.append(jnp.dot(xb[:, sl], wx_ref[h], preferred_element_type=F32))
    r = jax.nn.sigmoid(jnp.concatenate(ra, axis=1) + ba_ref[...])
    ig = jax.nn.sigmoid(jnp.concatenate(ri, axis=1) + bx_ref[...])
    log_a = -RG_C * r * _softplus(-lam_ref[...])
    a = jnp.exp(log_a)
    mult = jnp.sqrt(_neg_expm1(2.0 * log_a))
    return a, mult, ig


def _gelu_tanh(x):
    return 0.5 * x * (1.0 + jnp.tanh(math.sqrt(2.0 / math.pi) * (x + 0.044715 * (x * x * x))))


def _rglru_prompt_kernel(x_ref, gate_ref, cw_ref, cb_ref, wa_ref, ba_ref, wx_ref, bx_ref, lam_ref,
                         o_ref, h_ref, cst_ref, xp_ref, tail_ref, hprev_ref):
    n = pl.program_id(1)
    t = RG_T

    @pl.when(n == 0)
    def _():
        hprev_ref[...] = jnp.zeros_like(hprev_ref)

    x = x_ref[...]
    xc = _conv4(xp_ref, tail_ref, x, cw_ref, cb_ref, n == 0)
    cst_ref[...] = x_ref[t - (CONV_W - 1):t, :]
    a, mult, ig = _rg_gates(xc, wa_ref, ba_ref, wx_ref, bx_ref, lam_ref)
    rows = lax.broadcasted_iota(jnp.int32, a.shape, 0)
    mult = jnp.where((rows == 0) & (n == 0), 1.0, mult)
    a_cum, h = _scan_rows(a, mult * ig * xc)
    h = h + a_cum * hprev_ref[...]
    hprev_ref[...] = h[t - 1:t, :]
    h_ref[...] = h[t - 1:t, :]
    o_ref[...] = (h * _gelu_tanh(gate_ref[...])).astype(BF16)


def _rglru_prompt(proj, p, bp, lp, m_all):
    nc = lp // RG_T
    full2 = lambda b, n: (0, 0)
    full3 = lambda b, n: (0, 0, 0)
    row = lambda a: a.reshape(1, RG_WIDTH)
    return pl.pallas_call(
        _rglru_prompt_kernel,
        grid=(bp, nc),
        in_specs=[
            pl.BlockSpec((RG_T, RG_WIDTH), lambda b, n: (b * nc + n, RG_OFF // RG_WIDTH)),
            pl.BlockSpec((RG_T, RG_WIDTH), lambda b, n: (b * nc + n, RG_OFF // RG_WIDTH + 1)),
            pl.BlockSpec((CONV_W, RG_WIDTH), full2),
            pl.BlockSpec((1, RG_WIDTH), full2),
            pl.BlockSpec((RG_HEADS, RG_HEAD_DIM, RG_HEAD_DIM), full3),
            pl.BlockSpec((1, RG_WIDTH), full2),
            pl.BlockSpec((RG_HEADS, RG_HEAD_DIM, RG_HEAD_DIM), full3),
            pl.BlockSpec((1, RG_WIDTH), full2),
            pl.BlockSpec((1, RG_WIDTH), full2),
        ],
        out_specs=[
            pl.BlockSpec((RG_T, RG_WIDTH), lambda b, n: (b * nc + n, 0)),
            pl.BlockSpec((None, 1, RG_WIDTH), lambda b, n: (b, 0, 0)),
            pl.BlockSpec((None, CONV_W - 1, RG_WIDTH), lambda b, n: (b, 0, 0)),
        ],
        out_shape=[
            jax.ShapeDtypeStruct((m_all, RG_WIDTH), BF16),
            jax.ShapeDtypeStruct((bp, 1, RG_WIDTH), F32),
            jax.ShapeDtypeStruct((bp, CONV_W - 1, RG_WIDTH), F32),
        ],
        scratch_shapes=[
            pltpu.VMEM((RG_T + SUBLANES, RG_WIDTH), F32),
            pltpu.VMEM((SUBLANES, RG_WIDTH), F32),
            pltpu.VMEM((1, RG_WIDTH), F32),
        ],
        compiler_params=pltpu.CompilerParams(
            dimension_semantics=("parallel", "arbitrary"), vmem_limit_bytes=VMEM_LIMIT),
        name="rglru_prompt",
    )(proj, proj, p['rg_conv_w'], row(p['rg_conv_b']), p['rg_wa'].astype(BF16), row(p['rg_ba']),
      p['rg_wx'].astype(BF16), row(p['rg_bx']), row(p['rg_lambda']))


SB = 8
HG_REST = 4 * HG_WIDTH - LANES


def _col_bcast(row):
    return jnp.broadcast_to(row, (LANES, LANES)).T


def _sample_kernel(ssm_ref, hg0_ref, hg1_ref, rx_ref, rgate_ref,
                   hs_ref, rh_ref, rc_ref, ss_ref, sc_ref,
                   lb_ref, hnw_ref,
                   rcw_ref, rcb_ref, wa_ref, ba_ref, wx_ref, bx_ref, lam_ref,
                   scw_ref, scb_ref, dtb_ref, aexp_ref, dexp_ref, snw_ref, expand_ref,
                   ohg_in, org_in, ossm_in,
                   ohg_ref, org_ref, ossm_ref, hs_out, rh_out, rc_out, ss_out, sc_out,
                   f_sc, kk_sc, q_sc, o_sc, adec_sc, xdt_sc, b_sc, c_sc, y_sc):
    del ohg_in, org_in, ossm_in

    x = rx_ref[...]
    xc = rcb_ref[...] + rcw_ref[CONV_W - 1:CONV_W, :] * x
    for k in range(CONV_W - 1):
        xc = xc + rcw_ref[k:k + 1, :] * rc_ref[k]
        rc_out[k] = x if k == CONV_W - 2 else rc_ref[k + 1]
    a, mult, ig = _rg_gates(xc, wa_ref, ba_ref, wx_ref, bx_ref, lam_ref)
    h = a * rh_ref[...] + mult * ig * xc
    rh_out[...] = h
    org_ref[...] = (h * _gelu_tanh(rgate_ref[...])).astype(BF16)

    z = ssm_ref[:, 0:SSM_WIDTH]
    xbc_raw = ssm_ref[:, SSM_WIDTH:SSM_WIDTH + SSM_CONV_DIM]
    dt_raw = ssm_ref[:, SSM_WIDTH + SSM_CONV_DIM:]
    xbc = scb_ref[...] + scw_ref[CONV_W - 1:CONV_W, :] * xbc_raw
    for k in range(CONV_W - 1):
        xbc = xbc + scw_ref[k:k + 1, :] * sc_ref[k]
        sc_out[k] = xbc_raw if k == CONV_W - 2 else sc_ref[k + 1]
    xbc = _silu(xbc)
    xs = xbc[:, :SSM_WIDTH]
    gs = SSM_GROUPS * SSM_DSTATE
    b_sc[...] = xbc[:, SSM_WIDTH:SSM_WIDTH + gs]
    c_sc[...] = xbc[:, SSM_WIDTH + gs:]
    dt = _softplus(dt_raw + dtb_ref[...])
    parts = jnp.concatenate(_split3(dt), axis=0)
    r = jnp.dot(parts, expand_ref[...], preferred_element_type=F32)
    dt_exp = r[0:SB] + r[SB:2 * SB] + r[2 * SB:]
    xdt_sc[...] = xs * dt_exp
    adec_sc[...] = jnp.exp(dt_exp * aexp_ref[...])

    def hg_tile(kind, head):
        idx = kind * HG_HEADS + head
        if idx == 0:
            return hg0_ref[...]
        return hg1_ref[:, (idx - 1) * LANES:idx * LANES]

    for hd in range(HG_HEADS):
        sl = slice(hd * LANES, (hd + 1) * LANES)
        lb = lb_ref[:, sl]
        fz = hg_tile(1, hd)
        f_sc[:, sl] = lb + (1.0 - lb) * jax.nn.sigmoid(fz)
        kk_sc[:, sl] = (1.0 - lb) * jax.nn.sigmoid(-fz)
        q_sc[:, sl] = _silu(hg_tile(0, hd))

    for j in range(SB):
        row = slice(j, j + 1)
        for hd in range(HG_HEADS):
            sl = slice(hd * LANES, (hd + 1) * LANES)
            v_row = hg1_ref[row, (2 * HG_HEADS + hd - 1) * LANES:(2 * HG_HEADS + hd) * LANES]
            s_new = _col_bcast(f_sc[row, sl]) * hs_ref[j, hd] + _col_bcast(kk_sc[row, sl]) * v_row
            hs_out[j, hd] = s_new
            q8 = jnp.broadcast_to(q_sc[row, sl], (SB, LANES)).astype(BF16)
            o_sc[row, sl] = jnp.dot(q8, s_new.astype(BF16), preferred_element_type=F32)[0:1]
        for hp in range(SSM_HEADS // 2):
            sl = slice(hp * LANES, (hp + 1) * LANES)
            g = (2 * hp) // HEADS_PER_GROUP
            gsl = slice(g * SSM_DSTATE, (g + 1) * SSM_DSTATE)
            s_old = ss_ref[j, 2 * hp:2 * hp + 2].reshape(LANES, SSM_DSTATE)
            s_new = _col_bcast(adec_sc[row, sl]) * s_old + _col_bcast(xdt_sc[row, sl]) * b_sc[row, gsl]
            ss_out[j, 2 * hp:2 * hp + 2] = s_new.reshape(2, SSM_HEADDIM, SSM_DSTATE)
            c8 = jnp.broadcast_to(c_sc[row, gsl], (SB, SSM_DSTATE)).astype(BF16)
            y_sc[row, sl] = lax.dot_general(c8, s_new.astype(BF16), _NT, preferred_element_type=F32)[0:1]

    for hd in range(HG_HEADS):
        sl = slice(hd * LANES, (hd + 1) * LANES)
        o = o_sc[:, sl]
        o = o * lax.rsqrt(jnp.mean(o * o, axis=-1, keepdims=True) + EPS)
        ohg_ref[:, sl] = (o * hnw_ref[:, sl] * _silu(hg_tile(3, hd))).astype(BF16)
    y = (y_sc[...] + dexp_ref[...] * xs) * _silu(z)
    gw = SSM_WIDTH // SSM_GROUPS
    for g in range(SSM_GROUPS):
        yg = y[:, g * gw:(g + 1) * gw]
        yg = yg * lax.rsqrt(jnp.mean(yg * yg, axis=-1, keepdims=True) + EPS)
        ossm_ref[:, g * gw:(g + 1) * gw] = (yg * snw_ref[:, g * gw:(g + 1) * gw]).astype(BF16)


def _sample_mix(proj, st, p, lb, mp, o_bufs):
    hg_s, rg_h, rg_c, ssm_s, ssm_c = st
    bs = hg_s.shape[0]
    m_all = mp + bs
    if o_bufs is None:
        o_bufs = tuple(jnp.zeros((m_all, w), BF16) for w in (HG_WIDTH, RG_WIDTH, SSM_WIDTH))
    rb = mp // SB
    const2 = lambda i: (0, 0)
    const3 = lambda i: (0, 0, 0)
    row = lambda a, w: a.reshape(1, w)
    any_spec = pl.BlockSpec(memory_space=pl.ANY)
    dtb = jnp.pad(p['ssm_dt_bias'].reshape(1, SSM_HEADS), ((0, 0), (0, LANES - SSM_HEADS)))
    aexp = jnp.repeat(-jnp.exp(p['ssm_a_log']), SSM_HEADDIM).reshape(1, SSM_WIDTH)
    dexp = jnp.repeat(p['ssm_d'], SSM_HEADDIM).reshape(1, SSM_WIDTH)
    expand = jnp.asarray(np.arange(LANES)[:, None] == (np.arange(SSM_WIDTH)[None, :] // SSM_HEADDIM), BF16)
    rg_c_t = jnp.transpose(rg_c, (1, 0, 2))
    ssm_c_t = jnp.transpose(ssm_c, (1, 0, 2))
    in_specs = [
        pl.BlockSpec((SB, SSM_COLS), lambda i: (rb + i, 0)),
        pl.BlockSpec((SB, LANES), lambda i: (rb + i, HG0_OFF // LANES)),
        pl.BlockSpec((SB, HG_REST), lambda i: (rb + i, HG1_OFF // HG_REST)),
        pl.BlockSpec((SB, RG_WIDTH), lambda i: (rb + i, RG_OFF // RG_WIDTH)),
        pl.BlockSpec((SB, RG_WIDTH), lambda i: (rb + i, RG_OFF // RG_WIDTH + 1)),
        pl.BlockSpec((SB, HG_HEADS, HG_DK, HG_DV), lambda i: (i, 0, 0, 0)),
        pl.BlockSpec((SB, RG_WIDTH), lambda i: (i, 0)),
        pl.BlockSpec((CONV_W - 1, SB, RG_WIDTH), lambda i: (0, i, 0)),
        pl.BlockSpec((SB, SSM_HEADS, SSM_HEADDIM, SSM_DSTATE), lambda i: (i, 0, 0, 0)),
        pl.BlockSpec((CONV_W - 1, SB, SSM_CONV_DIM), lambda i: (0, i, 0)),
        pl.BlockSpec((1, HG_WIDTH), const2),
        pl.BlockSpec((1, HG_WIDTH), const2),
        pl.BlockSpec((CONV_W, RG_WIDTH), const2),
        pl.BlockSpec((1, RG_WIDTH), const2),
        pl.BlockSpec((RG_HEADS, RG_HEAD_DIM, RG_HEAD_DIM), const3),
        pl.BlockSpec((1, RG_WIDTH), const2),
        pl.BlockSpec((RG_HEADS, RG_HEAD_DIM, RG_HEAD_DIM), const3),
        pl.BlockSpec((1, RG_WIDTH), const2),
        pl.BlockSpec((1, RG_WIDTH), const2),
        pl.BlockSpec((CONV_W, SSM_CONV_DIM), const2),
        pl.BlockSpec((1, SSM_CONV_DIM), const2),
        pl.BlockSpec((1, LANES), const2),
        pl.BlockSpec((1, SSM_WIDTH), const2),
        pl.BlockSpec((1, SSM_WIDTH), const2),
        pl.BlockSpec((1, SSM_WIDTH), const2),
        pl.BlockSpec((LANES, SSM_WIDTH), const2),
        any_spec, any_spec, any_spec,
    ]
    out_specs = [
        pl.BlockSpec((SB, HG_WIDTH), lambda i: (rb + i, 0)),
        pl.BlockSpec((SB, RG_WIDTH), lambda i: (rb + i, 0)),
        pl.BlockSpec((SB, SSM_WIDTH), lambda i: (rb + i, 0)),
        pl.BlockSpec((SB, HG_HEADS, HG_DK, HG_DV), lambda i: (i, 0, 0, 0)),
        pl.BlockSpec((SB, RG_WIDTH), lambda i: (i, 0)),
        pl.BlockSpec((CONV_W - 1, SB, RG_WIDTH), lambda i: (0, i, 0)),
        pl.BlockSpec((SB, SSM_HEADS, SSM_HEADDIM, SSM_DSTATE), lambda i: (i, 0, 0, 0)),
        pl.BlockSpec((CONV_W - 1, SB, SSM_CONV_DIM), lambda i: (0, i, 0)),
    ]
    out_shape = [
        jax.ShapeDtypeStruct((m_all, HG_WIDTH), BF16),
        jax.ShapeDtypeStruct((m_all, RG_WIDTH), BF16),
        jax.ShapeDtypeStruct((m_all, SSM_WIDTH), BF16),
        jax.ShapeDtypeStruct(hg_s.shape, F32),
        jax.ShapeDtypeStruct(rg_h.shape, F32),
        jax.ShapeDtypeStruct(rg_c_t.shape, F32),
        jax.ShapeDtypeStruct(ssm_s.shape, F32),
        jax.ShapeDtypeStruct(ssm_c_t.shape, F32),
    ]
    n_in = len(in_specs)
    outs = pl.pallas_call(
        _sample_kernel,
        grid=(bs // SB,),
        in_specs=in_specs,
        out_specs=out_specs,
        out_shape=out_shape,
        input_output_aliases={n_in - 3: 0, n_in - 2: 1, n_in - 1: 2},
        scratch_shapes=[
            pltpu.VMEM((SB, HG_WIDTH), F32), pltpu.VMEM((SB, HG_WIDTH), F32), pltpu.VMEM((SB, HG_WIDTH), F32),
            pltpu.VMEM((SB, HG_WIDTH), F32),
            pltpu.VMEM((SB, SSM_WIDTH), F32), pltpu.VMEM((SB, SSM_WIDTH), F32),
            pltpu.VMEM((SB, SSM_GROUPS * SSM_DSTATE), F32), pltpu.VMEM((SB, SSM_GROUPS * SSM_DSTATE), F32),
            pltpu.VMEM((SB, SSM_WIDTH), F32),
        ],
        compiler_params=pltpu.CompilerParams(
            dimension_semantics=("parallel",), vmem_limit_bytes=VMEM_LIMIT),
        name="sample_mix",
    )(proj, proj, proj, proj, proj, hg_s, rg_h, rg_c_t, ssm_s, ssm_c_t,
      lb.reshape(1, HG_WIDTH), row(p['hg_norm_w'], HG_WIDTH),
      p['rg_conv_w'], row(p['rg_conv_b'], RG_WIDTH), p['rg_wa'].astype(BF16), row(p['rg_ba'], RG_WIDTH),
      p['rg_wx'].astype(BF16), row(p['rg_bx'], RG_WIDTH), row(p['rg_lambda'], RG_WIDTH),
      p['ssm_conv_w'], row(p['ssm_conv_b'], SSM_CONV_DIM), dtb, aexp, dexp, row(p['ssm_norm_w'], SSM_WIDTH),
      expand, *o_bufs)
    o_hg, o_rg, o_ssm, hg_new, rg_h_new, rg_c_new, ssm_new, ssm_c_new = outs
    return (o_hg, o_rg, o_ssm), (hg_new, rg_h_new, jnp.transpose(rg_c_new, (1, 0, 2)), ssm_new,
                                 jnp.transpose(ssm_c_new, (1, 0, 2)))


def kernel(x_prompt, x_sample, state_hgrn, state_rglru, state_rglru_conv, state_ssm, state_ssm_conv, norm_g, ffn1_w_gate, ffn1_w_up, ffn1_w_down, ffn2_w_gate, ffn2_w_up, ffn2_w_down, w_in, w_out, hg_lb_logits, hg_norm_w, rg_conv_w, rg_conv_b, rg_wa, rg_ba, rg_wx, rg_bx, rg_lambda, ssm_conv_w, ssm_conv_b, ssm_dt_bias, ssm_a_log, ssm_d, ssm_norm_w):
    bp, lp, _ = x_prompt.shape
    bs, ls, _ = x_sample.shape
    mp = bp * lp
    ms = bs * ls
    assert (mp + ms) % ROW_TILE == 0 and D_FF % FF_TILE == 0 and D_IN_PAD % IN_TILE == 0

    lw = {
        'hg_norm_w': hg_norm_w, 'rg_conv_w': rg_conv_w, 'rg_conv_b': rg_conv_b, 'rg_wa': rg_wa,
        'rg_ba': rg_ba, 'rg_wx': rg_wx, 'rg_bx': rg_bx, 'rg_lambda': rg_lambda,
        'ssm_conv_w': ssm_conv_w, 'ssm_conv_b': ssm_conv_b, 'ssm_dt_bias': ssm_dt_bias,
        'ssm_a_log': ssm_a_log, 'ssm_d': ssm_d, 'ssm_norm_w': ssm_norm_w,
    }
    lb_cum = jnp.cumsum(jax.nn.softmax(hg_lb_logits.astype(F32), axis=0), axis=0)
    lower_bounds = lb_cum - lb_cum[:1]

    w1g, w1u, w1d = (w.astype(BF16) for w in (ffn1_w_gate, ffn1_w_up, ffn1_w_down))
    w2g, w2u, w2d = (w.astype(BF16) for w in (ffn2_w_gate, ffn2_w_up, ffn2_w_down))
    w_in_b = _permute_w_in(w_in)
    w_out_b = w_out.astype(BF16)

    sample_init = (state_hgrn, state_rglru, state_rglru_conv, state_ssm, state_ssm_conv)

    x = jnp.concatenate([x_prompt.reshape(mp, D_MODEL), x_sample.reshape(ms, D_MODEL)], axis=0)
    new_p = ([], [], [], [], [])
    new_s = ([], [], [], [], [])
    for l in range(DEPTH):
        g = norm_g[l].reshape(6, 1, D_MODEL)
        p = {name: arr[l] for name, arr in lw.items()}
        x = _ffn(x, g[0], g[1], w1g, w1u, w1d, l)
        proj = _inproj(x, g[2], w_in_b, l)
        o_hg, hg_new = _hgrn_prompt(proj, p, lower_bounds[l], bp, lp, mp + ms)
        o_rg, rg_h_new, rg_c_new = _rglru_prompt(proj, p, bp, lp, mp + ms)
        o_ssm, ssm_new, ssm_c_new = _ssd_prompt(proj, p, bp, lp, mp + ms)
        st_p = (hg_new, rg_h_new.reshape(bp, RG_WIDTH), rg_c_new, ssm_new, ssm_c_new)
        (o_hg, o_rg, o_ssm), st_s = _sample_mix(proj, tuple(s[l] for s in sample_init), p, lower_bounds[l],
                                                mp, (o_hg, o_rg, o_ssm))
        x = _outproj(x, o_hg, o_rg, o_ssm, g[3], w_out_b, l)
        x = _ffn(x, g[4], g[5], w2g, w2u, w2d, l)
        for acc, s in zip(new_p, st_p):
            acc.append(s)
        for acc, s in zip(new_s, st_s):
            acc.append(s)
    hg_p, rg_p, rgc_p, ssm_p, ssmc_p = (jnp.stack(a) for a in new_p)
    hg_s, rg_s, rgc_s, ssm_s, ssmc_s = (jnp.stack(a) for a in new_s)
    y_prompt = x[:mp].reshape(bp, lp, D_MODEL)
    y_sample = x[mp:].reshape(bs, ls, D_MODEL)
    return (y_prompt, y_sample, hg_p, hg_s, rg_p, rg_s, rgc_p, rgc_s, ssm_p, ssm_s, ssmc_p, ssmc_s)
```

```python
import functools
import math

import jax
import jax.numpy as jnp
import numpy as np
from jax import lax
from jax.experimental import pallas as pl
from jax.experimental.pallas import tpu as pltpu

F32 = jnp.float32
BF16 = jnp.bfloat16

D_MODEL = 2048
DEPTH = 2
EPS = 1e-6
CONV_W = 4
HG_HEADS = 4
HG_DK = 128
HG_DV = 128
HG_WIDTH = HG_HEADS * HG_DV
HG_CHUNK = 32
RG_HEADS = 6
RG_HEAD_DIM = 128
RG_WIDTH = RG_HEADS * RG_HEAD_DIM
RG_C = 8.0
SSM_HEADS = 12
SSM_HEADDIM = 64
SSM_WIDTH = SSM_HEADS * SSM_HEADDIM
SSM_GROUPS = 2
HEADS_PER_GROUP = SSM_HEADS // SSM_GROUPS
SSM_DSTATE = 128
SSM_CHUNK = 64
SSM_CONV_DIM = SSM_WIDTH + 2 * SSM_GROUPS * SSM_DSTATE
D_MIX = HG_WIDTH + RG_WIDTH + SSM_WIDTH
IN_SIZES = (HG_HEADS * HG_DK, HG_HEADS * HG_DK, HG_WIDTH, HG_WIDTH, RG_WIDTH, RG_WIDTH,
            SSM_WIDTH, SSM_CONV_DIM, SSM_HEADS)
D_IN_PROJ = sum(IN_SIZES)
D_FF = 5632

LANES = 128
D_IN_PAD = -(-D_IN_PROJ // LANES) * LANES
ROW_TILE = 640
FF_TILE = 512
IN_TILE = 1152
VMEM_LIMIT = 56 * 1024 * 1024


def _rms(x, g):
    return x * lax.rsqrt(jnp.mean(x * x, axis=-1, keepdims=True) + EPS) * g


def _ffn_kernel(x_ref, gin_ref, gout_ref, wg_ref, wu_ref, wd_ref, o_ref, xn_ref):
    j = pl.program_id(1)

    @pl.when(j == 0)
    def _():
        xn_ref[...] = _rms(x_ref[...], gin_ref[...]).astype(BF16)
        o_ref[...] = jnp.zeros_like(o_ref)

    xn = xn_ref[...]
    g = jnp.dot(xn, wg_ref[...], preferred_element_type=F32)
    u = jnp.dot(xn, wu_ref[...], preferred_element_type=F32)
    h = (g * jax.nn.sigmoid(g) * u).astype(BF16)
    o_ref[...] += jnp.dot(h, wd_ref[...], preferred_element_type=F32)

    @pl.when(j == pl.num_programs(1) - 1)
    def _():
        o_ref[...] = x_ref[...] + 0.5 * _rms(o_ref[...], gout_ref[...])


def _ffn(x, g_in, g_out, wg, wu, wd, layer):
    m = x.shape[0]
    return pl.pallas_call(
        _ffn_kernel,
        grid=(m // ROW_TILE, D_FF // FF_TILE),
        in_specs=[
            pl.BlockSpec((ROW_TILE, D_MODEL), lambda i, j: (i, 0)),
            pl.BlockSpec((1, D_MODEL), lambda i, j: (0, 0)),
            pl.BlockSpec((1, D_MODEL), lambda i, j: (0, 0)),
            pl.BlockSpec((None, D_MODEL, FF_TILE), lambda i, j: (layer, 0, j)),
            pl.BlockSpec((None, D_MODEL, FF_TILE), lambda i, j: (layer, 0, j)),
            pl.BlockSpec((None, FF_TILE, D_MODEL), lambda i, j: (layer, j, 0)),
        ],
        out_specs=pl.BlockSpec((ROW_TILE, D_MODEL), lambda i, j: (i, 0)),
        out_shape=jax.ShapeDtypeStruct((m, D_MODEL), F32),
        scratch_shapes=[pltpu.VMEM((ROW_TILE, D_MODEL), BF16)],
        compiler_params=pltpu.CompilerParams(
            dimension_semantics=("parallel", "arbitrary"), vmem_limit_bytes=VMEM_LIMIT),
        name="ffn",
    )(x, g_in, g_out, wg, wu, wd)


def _inproj_kernel(x_ref, g_ref, w_ref, o_ref, xn_ref):
    @pl.when(pl.program_id(1) == 0)
    def _():
        xn_ref[...] = _rms(x_ref[...], g_ref[...]).astype(BF16)

    o_ref[...] = jnp.dot(xn_ref[...], w_ref[...], preferred_element_type=F32)


def _inproj(x, g, w_in, layer):
    m = x.shape[0]
    return pl.pallas_call(
        _inproj_kernel,
        grid=(m // ROW_TILE, D_IN_PAD // IN_TILE),
        in_specs=[
            pl.BlockSpec((ROW_TILE, D_MODEL), lambda i, j: (i, 0)),
            pl.BlockSpec((1, D_MODEL), lambda i, j: (0, 0)),
            pl.BlockSpec((None, D_MODEL, IN_TILE), lambda i, j: (layer, 0, j)),
        ],
        out_specs=pl.BlockSpec((ROW_TILE, IN_TILE), lambda i, j: (i, j)),
        out_shape=jax.ShapeDtypeStruct((m, D_IN_PAD), F32),
        scratch_shapes=[pltpu.VMEM((ROW_TILE, D_MODEL), BF16)],
        compiler_params=pltpu.CompilerParams(
            dimension_semantics=("parallel", "arbitrary"), vmem_limit_bytes=VMEM_LIMIT),
        name="inproj",
    )(x, g, w_in)


OUT_TILE = 512


def _outproj_kernel(x_ref, php_ref, prg_ref, pss_ref, shg_ref, srg_ref, sss_ref, g_ref, w_ref, y_ref):
    i = pl.program_id(0)
    last = pl.num_programs(0) - 1
    r0, r1 = HG_WIDTH, HG_WIDTH + RG_WIDTH

    def mixed(ohg, org, oss):
        m = jnp.dot(ohg, w_ref[0:r0, :], preferred_element_type=F32)
        m += jnp.dot(org, w_ref[r0:r1, :], preferred_element_type=F32)
        m += jnp.dot(oss, w_ref[r1:, :], preferred_element_type=F32)
        return _rms(m, g_ref[...])

    @pl.when(i < last)
    def _():
        y_ref[...] = x_ref[...] + mixed(php_ref[...], prg_ref[...], pss_ref[...])

    @pl.when(i == last)
    def _():
        ns = shg_ref.shape[0]
        y_ref[0:ns, :] = x_ref[0:ns, :] + mixed(shg_ref[...], srg_ref[...], sss_ref[...])


def _outproj(x, o_prompt, o_sample, g, w_out, layer):
    mp, ms = o_prompt[0].shape[0], o_sample[0].shape[0]
    assert mp % OUT_TILE == 0 and ms <= OUT_TILE and x.shape[0] == mp + ms
    n_p = mp // OUT_TILE
    widths = (HG_WIDTH, RG_WIDTH, SSM_WIDTH)
    return pl.pallas_call(
        _outproj_kernel,
        grid=(n_p + 1,),
        in_specs=[pl.BlockSpec((OUT_TILE, D_MODEL), lambda i: (i, 0))]
        + [pl.BlockSpec((OUT_TILE, w), lambda i: (jnp.minimum(i, n_p - 1), 0)) for w in widths]
        + [pl.BlockSpec((ms, w), lambda i: (0, 0)) for w in widths]
        + [pl.BlockSpec((1, D_MODEL), lambda i: (0, 0)),
           pl.BlockSpec((None, D_MIX, D_MODEL), lambda i: (layer, 0, 0))],
        out_specs=pl.BlockSpec((OUT_TILE, D_MODEL), lambda i: (i, 0)),
        out_shape=jax.ShapeDtypeStruct((mp + ms, D_MODEL), F32),
        compiler_params=pltpu.CompilerParams(
            dimension_semantics=("parallel",), vmem_limit_bytes=VMEM_LIMIT),
        name="outproj",
    )(x, *o_prompt, *o_sample, g, w_out)


SSM_OFF = 0
SSM_COLS = SSM_WIDTH + SSM_CONV_DIM + LANES
HG0_OFF = SSM_OFF + SSM_COLS
RG_OFF = HG0_OFF + LANES
HG1_OFF = RG_OFF + 2 * RG_WIDTH
assert RG_OFF % RG_WIDTH == 0 and HG1_OFF % LANES == 0
assert HG1_OFF + 4 * HG_WIDTH - LANES == D_IN_PAD


def _permute_w_in(w):
    c = np.cumsum((0,) + IN_SIZES)
    w = w.astype(BF16)
    hg, rg, ssm = w[..., c[0]:c[4]], w[..., c[4]:c[6]], w[..., c[6]:c[9]]
    pad = jnp.zeros(w.shape[:-1] + (LANES - SSM_HEADS,), BF16)
    return jnp.concatenate([ssm, pad, hg[..., :LANES], rg, hg[..., LANES:]], axis=-1)

SUBLANES = 8
SSD_T = 256
NEG_BIG = -1e30


def _split3(x):
    hi = x.astype(BF16)
    r = x - hi.astype(F32)
    mid = r.astype(BF16)
    lo = (r - mid.astype(F32)).astype(BF16)
    return hi, mid, lo


def _cumsum_rows(tri, x):
    w = x.shape[1]
    parts = jnp.concatenate(_split3(x), axis=1)
    r = jnp.dot(tri, parts, preferred_element_type=F32)
    return r[:, :w] + r[:, w:2 * w] + r[:, 2 * w:]


def _silu(x):
    return x * jax.nn.sigmoid(x)


def _neg_expm1(x):
    series = -x * (1.0 + x * (1 / 2 + x * (1 / 6 + x * (1 / 24 + x * (1 / 120 + x * (1 / 720 + x / 5040))))))
    return jnp.where(x > -0.25, series, 1.0 - jnp.exp(x))


def _softplus(x):
    return jnp.maximum(x, 0.0) + jnp.log(1.0 + jnp.exp(-jnp.abs(x)))


def _conv4(xp_ref, tail_ref, x, w_ref, b_ref, first):
    t = x.shape[0]

    @pl.when(first)
    def _():
        tail_ref[...] = jnp.zeros_like(tail_ref)

    xp_ref[0:SUBLANES, :] = tail_ref[...]
    xp_ref[SUBLANES:, :] = x
    tail_ref[...] = x[t - SUBLANES:, :]
    y = b_ref[...] + w_ref[CONV_W - 1:CONV_W, :] * x
    for k in range(CONV_W - 1):
        off = SUBLANES - (CONV_W - 1) + k
        y = y + w_ref[k:k + 1, :] * xp_ref[off:off + t, :]
    return y


def _ssd_prompt_kernel(p_ref, tri_ref, cw_ref, cb_ref, dtb_ref, alog_ref, dexp_ref, nw_ref,
                       o_ref, st_ref, cst_ref, xp_ref, tail_ref, s_ref, y_ref):
    n = pl.program_id(1)
    t = SSD_T
    z = p_ref[:, 0:SSM_WIDTH]
    xbc_raw = p_ref[:, SSM_WIDTH:SSM_WIDTH + SSM_CONV_DIM]
    dt_raw = p_ref[:, SSM_WIDTH + SSM_CONV_DIM:]

    @pl.when(n == 0)
    def _():
        s_ref[...] = jnp.zeros_like(s_ref)

    xbc = _silu(_conv4(xp_ref, tail_ref, xbc_raw, cw_ref, cb_ref, n == 0))
    cst_ref[...] = p_ref[t - (CONV_W - 1):t, SSM_WIDTH:SSM_WIDTH + SSM_CONV_DIM]
    xs = xbc[:, :SSM_WIDTH]
    gs = SSM_GROUPS * SSM_DSTATE
    bm = xbc[:, SSM_WIDTH:SSM_WIDTH + gs].astype(BF16)
    cm = xbc[:, SSM_WIDTH + gs:].astype(BF16)

    dt = _softplus(dt_raw + dtb_ref[...])
    a = dt * -jnp.exp(alog_ref[...])
    cs = _cumsum_rows(tri_ref[...], a)
    cs_t = cs.T
    cs_last = cs[t - 1:t, :]
    e_cs = jnp.exp(cs)
    e_rem = jnp.exp(cs_last - cs)
    e_last = jnp.exp(cs_last)
    row = lax.broadcasted_iota(jnp.int32, (t, t), 0)
    col = lax.broadcasted_iota(jnp.int32, (t, t), 1)
    causal = row >= col

    for g in range(SSM_GROUPS):
        bg = bm[:, g * SSM_DSTATE:(g + 1) * SSM_DSTATE]
        cg = cm[:, g * SSM_DSTATE:(g + 1) * SSM_DSTATE]
        cb = lax.dot_general(cg, bg, (((1,), (1,)), ((), ())), preferred_element_type=F32)
        for hh in range(HEADS_PER_GROUP):
            h = g * HEADS_PER_GROUP + hh
            sl = slice(h * SSM_HEADDIM, (h + 1) * SSM_HEADDIM)
            seg = jnp.where(causal, cs[:, h:h + 1] - cs_t[h:h + 1, :], NEG_BIG)
            scores = (cb * jnp.exp(seg)).astype(BF16)
            xdt = xs[:, sl] * dt[:, h:h + 1]
            s_prev = s_ref[h]
            y = jnp.dot(scores, xdt.astype(BF16), preferred_element_type=F32)
            y_in = lax.dot_general(cg, s_prev.astype(BF16), (((1,), (1,)), ((), ())),
                                   preferred_element_type=F32)
            y_ref[:, sl] = y + e_cs[:, h:h + 1] * y_in
            w = (xdt * e_rem[:, h:h + 1]).astype(BF16)
            st = lax.dot_general(w, bg, (((0,), (0,)), ((), ())), preferred_element_type=F32)
            s_ref[h] = e_last[:, h:h + 1] * s_prev + st

    y = (y_ref[...] + dexp_ref[...] * xs) * _silu(z)
    gw = SSM_WIDTH // SSM_GROUPS
    for g in range(SSM_GROUPS):
        yg = y[:, g * gw:(g + 1) * gw]
        yg = yg * lax.rsqrt(jnp.mean(yg * yg, axis=-1, keepdims=True) + EPS)
        o_ref[:, g * gw:(g + 1) * gw] = (yg * nw_ref[:, g * gw:(g + 1) * gw]).astype(BF16)

    @pl.when(n == pl.num_programs(1) - 1)
    def _():
        st_ref[...] = s_ref[...]


def _tri(t):
    return jnp.tril(jnp.ones((t, t), F32)).astype(BF16)


def _ssd_prompt(proj, p, bp, lp):
    nc = lp // SSD_T
    full = lambda b, n: (0, 0)
    dtb = jnp.pad(p['ssm_dt_bias'].reshape(1, SSM_HEADS), ((0, 0), (0, LANES - SSM_HEADS)))
    alog = jnp.pad(p['ssm_a_log'].reshape(1, SSM_HEADS), ((0, 0), (0, LANES - SSM_HEADS)))
    dexp = jnp.repeat(p['ssm_d'], SSM_HEADDIM).reshape(1, SSM_WIDTH)
    return pl.pallas_call(
        _ssd_prompt_kernel,
        grid=(bp, nc),
        in_specs=[
            pl.BlockSpec((SSD_T, SSM_COLS), lambda b, n: (b * nc + n, SSM_OFF // SSM_COLS)),
            pl.BlockSpec((SSD_T, SSD_T), full),
            pl.BlockSpec((CONV_W, SSM_CONV_DIM), full),
            pl.BlockSpec((1, SSM_CONV_DIM), full),
            pl.BlockSpec((1, LANES), full),
            pl.BlockSpec((1, LANES), full),
            pl.BlockSpec((1, SSM_WIDTH), full),
            pl.BlockSpec((1, SSM_WIDTH), full),
        ],
        out_specs=[
            pl.BlockSpec((SSD_T, SSM_WIDTH), lambda b, n: (b * nc + n, 0)),
            pl.BlockSpec((None, SSM_HEADS, SSM_HEADDIM, SSM_DSTATE), lambda b, n: (b, 0, 0, 0)),
            pl.BlockSpec((None, CONV_W - 1, SSM_CONV_DIM), lambda b, n: (b, 0, 0)),
        ],
        out_shape=[
            jax.ShapeDtypeStruct((bp * lp, SSM_WIDTH), BF16),
            jax.ShapeDtypeStruct((bp, SSM_HEADS, SSM_HEADDIM, SSM_DSTATE), F32),
            jax.ShapeDtypeStruct((bp, CONV_W - 1, SSM_CONV_DIM), F32),
        ],
        scratch_shapes=[
            pltpu.VMEM((SSD_T + SUBLANES, SSM_CONV_DIM), F32),
            pltpu.VMEM((SUBLANES, SSM_CONV_DIM), F32),
            pltpu.VMEM((SSM_HEADS, SSM_HEADDIM, SSM_DSTATE), F32),
            pltpu.VMEM((SSD_T, SSM_WIDTH), F32),
        ],
        compiler_params=pltpu.CompilerParams(
            dimension_semantics=("parallel", "arbitrary"), vmem_limit_bytes=VMEM_LIMIT),
        name="ssd_prompt",
    )(proj, _tri(SSD_T), p['ssm_conv_w'], p['ssm_conv_b'].reshape(1, SSM_CONV_DIM), dtb, alog, dexp,
      p['ssm_norm_w'].reshape(1, SSM_WIDTH))


HG_T = 256

_NT = (((1,), (1,)), ((), ()))
_TN = (((0,), (0,)), ((), ()))


HG_REST = 4 * HG_WIDTH - LANES
HG_Q, HG_F, HG_V, HG_G = range(4)


def _hg_tile(hg0_ref, hg1_ref, kind, head, rows=slice(None)):
    idx = kind * HG_HEADS + head
    if idx == 0:
        return hg0_ref[rows, :]
    return hg1_ref[rows, (idx - 1) * LANES:idx * LANES]


def _hgrn_prompt_kernel(hg0_ref, hg1_ref, lb_ref, nw_ref, tri_ref, o_ref, st_ref, s_ref):
    n = pl.program_id(1)
    t, c = HG_T, HG_CHUNK
    nch = t // c

    @pl.when(n == 0)
    def _():
        s_ref[...] = jnp.zeros_like(s_ref)

    row = lax.broadcasted_iota(jnp.int32, (t, t), 0)
    col = lax.broadcasted_iota(jnp.int32, (t, t), 1)
    keep = (row >= col) & (row // c == col // c)
    tri = tri_ref[...]

    v, o_intra, kd, qe, decay = [], [], [], [], []
    for h in range(HG_HEADS):
        sl = slice(h * LANES, (h + 1) * LANES)
        lb = lb_ref[:, sl]
        fz = _hg_tile(hg0_ref, hg1_ref, HG_F, h)
        logf = jnp.log(lb + (1.0 - lb) * jax.nn.sigmoid(fz))
        kk3 = ((1.0 - lb) * jax.nn.sigmoid(-fz)).reshape(nch, c, HG_DK)
        qh3 = _silu(_hg_tile(hg0_ref, hg1_ref, HG_Q, h)).reshape(nch, c, HG_DK)
        vh = _hg_tile(hg0_ref, hg1_ref, HG_V, h).astype(BF16)
        b3 = _cumsum_rows(tri, logf).reshape(nch, c, HG_DK)
        b_mid = b3[:, c // 2:c // 2 + 1, :]
        b_last = b3[:, c - 1:c, :]
        q_in = (qh3 * jnp.exp(b3 - b_mid)).reshape(t, HG_DK).astype(BF16)
        k_in = (kk3 * jnp.exp(b_mid - b3)).reshape(t, HG_DK).astype(BF16)
        a = jnp.where(keep, lax.dot_general(q_in, k_in, _NT, preferred_element_type=F32), 0.0)
        v.append(vh)
        o_intra.append(jnp.dot(a.astype(BF16), vh, preferred_element_type=F32))
        kd.append((kk3 * jnp.exp(b_last - b3)).astype(BF16))
        qe.append((qh3 * jnp.exp(b3)).astype(BF16))
        decay.append(jnp.exp(b_last))

    outs = [[] for _ in range(HG_HEADS)]
    for ci in range(nch):
        rows = slice(ci * c, (ci + 1) * c)
        for h in range(HG_HEADS):
            s_t = s_ref[h]
            o_inter = lax.dot_general(qe[h][ci], s_t.astype(BF16), _NT, preferred_element_type=F32)
            outs[h].append(o_intra[h][rows, :] + o_inter)
            ds_t = lax.dot_general(v[h][rows, :], kd[h][ci], _TN, preferred_element_type=F32)
            s_ref[h] = s_t * decay[h][ci] + ds_t

    for h in range(HG_HEADS):
        sl = slice(h * LANES, (h + 1) * LANES)
        o = jnp.concatenate(outs[h], axis=0)
        o = o * lax.rsqrt(jnp.mean(o * o, axis=-1, keepdims=True) + EPS)
        g = _hg_tile(hg0_ref, hg1_ref, HG_G, h)
        o_ref[:, sl] = (o * nw_ref[:, sl] * _silu(g)).astype(BF16)

    @pl.when(n == pl.num_programs(1) - 1)
    def _():
        for h in range(HG_HEADS):
            st_ref[h] = s_ref[h].T


def _tri_chunks(t, c):
    r = np.arange(t)
    return jnp.asarray((r[:, None] >= r[None, :]) & (r[:, None] // c == r[None, :] // c), BF16)


def _hgrn_prompt(proj, p, lb, bp, lp):
    nc = lp // HG_T
    const = lambda b, n: (0, 0)
    return pl.pallas_call(
        _hgrn_prompt_kernel,
        grid=(bp, nc),
        in_specs=[
            pl.BlockSpec((HG_T, LANES), lambda b, n: (b * nc + n, HG0_OFF // LANES)),
            pl.BlockSpec((HG_T, HG_REST), lambda b, n: (b * nc + n, HG1_OFF // HG_REST)),
            pl.BlockSpec((1, HG_WIDTH), const),
            pl.BlockSpec((1, HG_WIDTH), const),
            pl.BlockSpec((HG_T, HG_T), const),
        ],
        out_specs=[
            pl.BlockSpec((HG_T, HG_WIDTH), lambda b, n: (b * nc + n, 0)),
            pl.BlockSpec((None, HG_HEADS, HG_DK, HG_DV), lambda b, n: (b, 0, 0, 0)),
        ],
        out_shape=[
            jax.ShapeDtypeStruct((bp * lp, HG_WIDTH), BF16),
            jax.ShapeDtypeStruct((bp, HG_HEADS, HG_DK, HG_DV), F32),
        ],
        scratch_shapes=[pltpu.VMEM((HG_HEADS, HG_DV, HG_DK), F32)],
        compiler_params=pltpu.CompilerParams(
            dimension_semantics=("parallel", "arbitrary"), vmem_limit_bytes=VMEM_LIMIT),
        name="hgrn_prompt",
    )(proj, proj, lb.reshape(1, HG_WIDTH), p['hg_norm_w'].reshape(1, HG_WIDTH),
      _tri_chunks(HG_T, HG_CHUNK))


RG_T = 256


def _scan_rows(a, u):
    t = a.shape[0]
    rows = lax.broadcasted_iota(jnp.int32, a.shape, 0)
    s = 1
    while s < t:
        keep = rows >= s
        a_sh = jnp.where(keep, pltpu.roll(a, s, axis=0), 1.0)
        u_sh = jnp.where(keep, pltpu.roll(u, s, axis=0), 0.0)
        u = u + a * u_sh
        a = a * a_sh
        s *= 2
    return a, u


def _rg_gates(xc, wa_ref, ba_ref, wx_ref, bx_ref, lam_ref):
    xb = xc.astype(BF16)
    ra, ri = [], []
    for h in range(RG_HEADS):
        sl = slice(h * RG_HEAD_DIM, (h + 1) * RG_HEAD_DIM)
        ra.append(jnp.dot(xb[:, sl], wa_ref[h], preferred_element_type=F32))
        ri.append(jnp.dot(xb[:, sl], wx_ref[h], preferred_element_type=F32))
    r = jax.nn.sigmoid(jnp.concatenate(ra, axis=1) + ba_ref[...])
    ig = jax.nn.sigmoid(jnp.concatenate(ri, axis=1) + bx_ref[...])
    log_a = -RG_C * r * _softplus(-lam_ref[...])
    a = jnp.exp(log_a)
    mult = jnp.sqrt(_neg_expm1(2.0 * log_a))
    return a, mult, ig


def _gelu_tanh(x):
    return 0.5 * x * (1.0 + jnp.tanh(math.sqrt(2.0 / math.pi) * (x + 0.044715 * (x * x * x))))


def _rglru_prompt_kernel(x_ref, gate_ref, cw_ref, cb_ref, wa_ref, ba_ref, wx_ref, bx_ref, lam_ref,
                         o_ref, h_ref, cst_ref, xp_ref, tail_ref, hprev_ref):
    n = pl.program_id(1)
    t = RG_T

    @pl.when(n == 0)
    def _():
        hprev_ref[...] = jnp.zeros_like(hprev_ref)

    x = x_ref[...]
    xc = _conv4(xp_ref, tail_ref, x, cw_ref, cb_ref, n == 0)
    cst_ref[...] = x_ref[t - (CONV_W - 1):t, :]
    a, mult, ig = _rg_gates(xc, wa_ref, ba_ref, wx_ref, bx_ref, lam_ref)
    rows = lax.broadcasted_iota(jnp.int32, a.shape, 0)
    mult = jnp.where((rows == 0) & (n == 0), 1.0, mult)
    a_cum, h = _scan_rows(a, mult * ig * xc)
    h = h + a_cum * hprev_ref[...]
    hprev_ref[...] = h[t - 1:t, :]
    h_ref[...] = h[t - 1:t, :]
    o_ref[...] = (h * _gelu_tanh(gate_ref[...])).astype(BF16)


def _rglru_prompt(proj, p, bp, lp):
    nc = lp // RG_T
    full2 = lambda b, n: (0, 0)
    full3 = lambda b, n: (0, 0, 0)
    row = lambda a: a.reshape(1, RG_WIDTH)
    return pl.pallas_call(
        _rglru_prompt_kernel,
        grid=(bp, nc),
        in_specs=[
            pl.BlockSpec((RG_T, RG_WIDTH), lambda b, n: (b * nc + n, RG_OFF // RG_WIDTH)),
            pl.BlockSpec((RG_T, RG_WIDTH), lambda b, n: (b * nc + n, RG_OFF // RG_WIDTH + 1)),
            pl.BlockSpec((CONV_W, RG_WIDTH), full2),
            pl.BlockSpec((1, RG_WIDTH), full2),
            pl.BlockSpec((RG_HEADS, RG_HEAD_DIM, RG_HEAD_DIM), full3),
            pl.BlockSpec((1, RG_WIDTH), full2),
            pl.BlockSpec((RG_HEADS, RG_HEAD_DIM, RG_HEAD_DIM), full3),
            pl.BlockSpec((1, RG_WIDTH), full2),
            pl.BlockSpec((1, RG_WIDTH), full2),
        ],
        out_specs=[
            pl.BlockSpec((RG_T, RG_WIDTH), lambda b, n: (b * nc + n, 0)),
            pl.BlockSpec((None, 1, RG_WIDTH), lambda b, n: (b, 0, 0)),
            pl.BlockSpec((None, CONV_W - 1, RG_WIDTH), lambda b, n: (b, 0, 0)),
        ],
        out_shape=[
            jax.ShapeDtypeStruct((bp * lp, RG_WIDTH), BF16),
            jax.ShapeDtypeStruct((bp, 1, RG_WIDTH), F32),
            jax.ShapeDtypeStruct((bp, CONV_W - 1, RG_WIDTH), F32),
        ],
        scratch_shapes=[
            pltpu.VMEM((RG_T + SUBLANES, RG_WIDTH), F32),
            pltpu.VMEM((SUBLANES, RG_WIDTH), F32),
            pltpu.VMEM((1, RG_WIDTH), F32),
        ],
        compiler_params=pltpu.CompilerParams(
            dimension_semantics=("parallel", "arbitrary"), vmem_limit_bytes=VMEM_LIMIT),
        name="rglru_prompt",
    )(proj, proj, p['rg_conv_w'], row(p['rg_conv_b']), p['rg_wa'].astype(BF16), row(p['rg_ba']),
      p['rg_wx'].astype(BF16), row(p['rg_bx']), row(p['rg_lambda']))


SB = 8


def _col_bcast(row):
    return jnp.broadcast_to(row, (LANES, LANES)).T


def _sample_kernel(ssm_ref, hg0_ref, hg1_ref, rx_ref, rgate_ref,
                   hs_ref, rh_ref, rc_ref, ss_ref, sc_ref,
                   lb_ref, hnw_ref,
                   rcw_ref, rcb_ref, wa_ref, ba_ref, wx_ref, bx_ref, lam_ref,
                   scw_ref, scb_ref, dtb_ref, aexp_ref, dexp_ref, snw_ref, expand_ref,
                   ohg_ref, org_ref, ossm_ref, hs_out, rh_out, rc_out, ss_out, sc_out,
                   f_sc, kk_sc, q_sc, o_sc, adec_sc, xdt_sc, b_sc, c_sc, y_sc):
    x = rx_ref[...]
    xc = rcb_ref[...] + rcw_ref[CONV_W - 1:CONV_W, :] * x
    for k in range(CONV_W - 1):
        xc = xc + rcw_ref[k:k + 1, :] * rc_ref[k]
        rc_out[k] = x if k == CONV_W - 2 else rc_ref[k + 1]
    a, mult, ig = _rg_gates(xc, wa_ref, ba_ref, wx_ref, bx_ref, lam_ref)
    h = a * rh_ref[...] + mult * ig * xc
    rh_out[...] = h
    org_ref[...] = (h * _gelu_tanh(rgate_ref[...])).astype(BF16)

    z = ssm_ref[:, 0:SSM_WIDTH]
    xbc_raw = ssm_ref[:, SSM_WIDTH:SSM_WIDTH + SSM_CONV_DIM]
    dt_raw = ssm_ref[:, SSM_WIDTH + SSM_CONV_DIM:]
    xbc = scb_ref[...] + scw_ref[CONV_W - 1:CONV_W, :] * xbc_raw
    for k in range(CONV_W - 1):
        xbc = xbc + scw_ref[k:k + 1, :] * sc_ref[k]
        sc_out[k] = xbc_raw if k == CONV_W - 2 else sc_ref[k + 1]
    xbc = _silu(xbc)
    xs = xbc[:, :SSM_WIDTH]
    gs = SSM_GROUPS * SSM_DSTATE
    b_sc[...] = xbc[:, SSM_WIDTH:SSM_WIDTH + gs]
    c_sc[...] = xbc[:, SSM_WIDTH + gs:]
    dt = _softplus(dt_raw + dtb_ref[...])
    parts = jnp.concatenate(_split3(dt), axis=0)
    r = jnp.dot(parts, expand_ref[...], preferred_element_type=F32)
    dt_exp = r[0:SB] + r[SB:2 * SB] + r[2 * SB:]
    xdt_sc[...] = xs * dt_exp
    adec_sc[...] = jnp.exp(dt_exp * aexp_ref[...])

    hg_tile = functools.partial(_hg_tile, hg0_ref, hg1_ref)
    for hd in range(HG_HEADS):
        sl = slice(hd * LANES, (hd + 1) * LANES)
        lb = lb_ref[:, sl]
        fz = hg_tile(HG_F, hd)
        f_sc[:, sl] = lb + (1.0 - lb) * jax.nn.sigmoid(fz)
        kk_sc[:, sl] = (1.0 - lb) * jax.nn.sigmoid(-fz)
        q_sc[:, sl] = _silu(hg_tile(HG_Q, hd))

    for j in range(SB):
        row = slice(j, j + 1)
        for hd in range(HG_HEADS):
            sl = slice(hd * LANES, (hd + 1) * LANES)
            v_row = hg_tile(HG_V, hd, row)
            s_new = _col_bcast(f_sc[row, sl]) * hs_ref[j, hd] + _col_bcast(kk_sc[row, sl]) * v_row
            hs_out[j, hd] = s_new
            q8 = jnp.broadcast_to(q_sc[row, sl], (SB, LANES)).astype(BF16)
            o_sc[row, sl] = jnp.dot(q8, s_new.astype(BF16), preferred_element_type=F32)[0:1]
        for hp in range(SSM_HEADS // 2):
            sl = slice(hp * LANES, (hp + 1) * LANES)
            g = (2 * hp) // HEADS_PER_GROUP
            gsl = slice(g * SSM_DSTATE, (g + 1) * SSM_DSTATE)
            s_old = ss_ref[j, 2 * hp:2 * hp + 2].reshape(LANES, SSM_DSTATE)
            s_new = _col_bcast(adec_sc[row, sl]) * s_old + _col_bcast(xdt_sc[row, sl]) * b_sc[row, gsl]
            ss_out[j, 2 * hp:2 * hp + 2] = s_new.reshape(2, SSM_HEADDIM, SSM_DSTATE)
            c8 = jnp.broadcast_to(c_sc[row, gsl], (SB, SSM_DSTATE)).astype(BF16)
            y_sc[row, sl] = lax.dot_general(c8, s_new.astype(BF16), _NT, preferred_element_type=F32)[0:1]

    for hd in range(HG_HEADS):
        sl = slice(hd * LANES, (hd + 1) * LANES)
        o = o_sc[:, sl]
        o = o * lax.rsqrt(jnp.mean(o * o, axis=-1, keepdims=True) + EPS)
        ohg_ref[:, sl] = (o * hnw_ref[:, sl] * _silu(hg_tile(HG_G, hd))).astype(BF16)
    y = (y_sc[...] + dexp_ref[...] * xs) * _silu(z)
    gw = SSM_WIDTH // SSM_GROUPS
    for g in range(SSM_GROUPS):
        yg = y[:, g * gw:(g + 1) * gw]
        yg = yg * lax.rsqrt(jnp.mean(yg * yg, axis=-1, keepdims=True) + EPS)
        ossm_ref[:, g * gw:(g + 1) * gw] = (yg * snw_ref[:, g * gw:(g + 1) * gw]).astype(BF16)


def _sample_mix(proj, states, layer, p, lb, mp):
    hg_s, rg_h, rg_c_t, ssm_s, ssm_c_t = states
    bs = hg_s.shape[1]
    rb = mp // SB
    const2 = lambda i: (0, 0)
    const3 = lambda i: (0, 0, 0)
    row = lambda a, w: a.reshape(1, w)
    dtb = jnp.pad(p['ssm_dt_bias'].reshape(1, SSM_HEADS), ((0, 0), (0, LANES - SSM_HEADS)))
    aexp = jnp.repeat(-jnp.exp(p['ssm_a_log']), SSM_HEADDIM).reshape(1, SSM_WIDTH)
    dexp = jnp.repeat(p['ssm_d'], SSM_HEADDIM).reshape(1, SSM_WIDTH)
    expand = jnp.asarray(np.arange(LANES)[:, None] == (np.arange(SSM_WIDTH)[None, :] // SSM_HEADDIM), BF16)
    in_specs = [
        pl.BlockSpec((SB, SSM_COLS), lambda i: (rb + i, 0)),
        pl.BlockSpec((SB, LANES), lambda i: (rb + i, HG0_OFF // LANES)),
        pl.BlockSpec((SB, HG_REST), lambda i: (rb + i, HG1_OFF // HG_REST)),
        pl.BlockSpec((SB, RG_WIDTH), lambda i: (rb + i, RG_OFF // RG_WIDTH)),
        pl.BlockSpec((SB, RG_WIDTH), lambda i: (rb + i, RG_OFF // RG_WIDTH + 1)),
        pl.BlockSpec((None, SB, HG_HEADS, HG_DK, HG_DV), lambda i: (layer, i, 0, 0, 0)),
        pl.BlockSpec((None, SB, RG_WIDTH), lambda i: (layer, i, 0)),
        pl.BlockSpec((None, CONV_W - 1, SB, RG_WIDTH), lambda i: (layer, 0, i, 0)),
        pl.BlockSpec((None, SB, SSM_HEADS, SSM_HEADDIM, SSM_DSTATE), lambda i: (layer, i, 0, 0, 0)),
        pl.BlockSpec((None, CONV_W - 1, SB, SSM_CONV_DIM), lambda i: (layer, 0, i, 0)),
        pl.BlockSpec((1, HG_WIDTH), const2),
        pl.BlockSpec((1, HG_WIDTH), const2),
        pl.BlockSpec((CONV_W, RG_WIDTH), const2),
        pl.BlockSpec((1, RG_WIDTH), const2),
        pl.BlockSpec((RG_HEADS, RG_HEAD_DIM, RG_HEAD_DIM), const3),
        pl.BlockSpec((1, RG_WIDTH), const2),
        pl.BlockSpec((RG_HEADS, RG_HEAD_DIM, RG_HEAD_DIM), const3),
        pl.BlockSpec((1, RG_WIDTH), const2),
        pl.BlockSpec((1, RG_WIDTH), const2),
        pl.BlockSpec((CONV_W, SSM_CONV_DIM), const2),
        pl.BlockSpec((1, SSM_CONV_DIM), const2),
        pl.BlockSpec((1, LANES), const2),
        pl.BlockSpec((1, SSM_WIDTH), const2),
        pl.BlockSpec((1, SSM_WIDTH), const2),
        pl.BlockSpec((1, SSM_WIDTH), const2),
        pl.BlockSpec((LANES, SSM_WIDTH), const2),
    ]
    out_specs = [
        pl.BlockSpec((SB, HG_WIDTH), lambda i: (i, 0)),
        pl.BlockSpec((SB, RG_WIDTH), lambda i: (i, 0)),
        pl.BlockSpec((SB, SSM_WIDTH), lambda i: (i, 0)),
        pl.BlockSpec((SB, HG_HEADS, HG_DK, HG_DV), lambda i: (i, 0, 0, 0)),
        pl.BlockSpec((SB, RG_WIDTH), lambda i: (i, 0)),
        pl.BlockSpec((CONV_W - 1, SB, RG_WIDTH), lambda i: (0, i, 0)),
        pl.BlockSpec((SB, SSM_HEADS, SSM_HEADDIM, SSM_DSTATE), lambda i: (i, 0, 0, 0)),
        pl.BlockSpec((CONV_W - 1, SB, SSM_CONV_DIM), lambda i: (0, i, 0)),
    ]
    out_shape = [
        jax.ShapeDtypeStruct((bs, HG_WIDTH), BF16),
        jax.ShapeDtypeStruct((bs, RG_WIDTH), BF16),
        jax.ShapeDtypeStruct((bs, SSM_WIDTH), BF16),
        jax.ShapeDtypeStruct(hg_s.shape[1:], F32),
        jax.ShapeDtypeStruct(rg_h.shape[1:], F32),
        jax.ShapeDtypeStruct(rg_c_t.shape[1:], F32),
        jax.ShapeDtypeStruct(ssm_s.shape[1:], F32),
        jax.ShapeDtypeStruct(ssm_c_t.shape[1:], F32),
    ]
    outs = pl.pallas_call(
        _sample_kernel,
        grid=(bs // SB,),
        in_specs=in_specs,
        out_specs=out_specs,
        out_shape=out_shape,
        scratch_shapes=[
            pltpu.VMEM((SB, HG_WIDTH), F32), pltpu.VMEM((SB, HG_WIDTH), F32), pltpu.VMEM((SB, HG_WIDTH), F32),
            pltpu.VMEM((SB, HG_WIDTH), F32),
            pltpu.VMEM((SB, SSM_WIDTH), F32), pltpu.VMEM((SB, SSM_WIDTH), F32),
            pltpu.VMEM((SB, SSM_GROUPS * SSM_DSTATE), F32), pltpu.VMEM((SB, SSM_GROUPS * SSM_DSTATE), F32),
            pltpu.VMEM((SB, SSM_WIDTH), F32),
        ],
        compiler_params=pltpu.CompilerParams(
            dimension_semantics=("parallel",), vmem_limit_bytes=VMEM_LIMIT),
        name="sample_mix",
    )(proj, proj, proj, proj, proj, hg_s, rg_h, rg_c_t, ssm_s, ssm_c_t,
      lb.reshape(1, HG_WIDTH), row(p['hg_norm_w'], HG_WIDTH),
      p['rg_conv_w'], row(p['rg_conv_b'], RG_WIDTH), p['rg_wa'].astype(BF16), row(p['rg_ba'], RG_WIDTH),
      p['rg_wx'].astype(BF16), row(p['rg_bx'], RG_WIDTH), row(p['rg_lambda'], RG_WIDTH),
      p['ssm_conv_w'], row(p['ssm_conv_b'], SSM_CONV_DIM), dtb, aexp, dexp, row(p['ssm_norm_w'], SSM_WIDTH),
      expand)
    return tuple(outs[:3]), tuple(outs[3:])


def kernel(x_prompt, x_sample, state_hgrn, state_rglru, state_rglru_conv, state_ssm, state_ssm_conv, norm_g, ffn1_w_gate, ffn1_w_up, ffn1_w_down, ffn2_w_gate, ffn2_w_up, ffn2_w_down, w_in, w_out, hg_lb_logits, hg_norm_w, rg_conv_w, rg_conv_b, rg_wa, rg_ba, rg_wx, rg_bx, rg_lambda, ssm_conv_w, ssm_conv_b, ssm_dt_bias, ssm_a_log, ssm_d, ssm_norm_w):
    bp, lp, _ = x_prompt.shape
    bs, ls, _ = x_sample.shape
    mp = bp * lp
    ms = bs * ls
    assert (mp + ms) % ROW_TILE == 0 and D_FF % FF_TILE == 0 and D_IN_PAD % IN_TILE == 0

    lw = {
        'hg_norm_w': hg_norm_w, 'rg_conv_w': rg_conv_w, 'rg_conv_b': rg_conv_b, 'rg_wa': rg_wa,
        'rg_ba': rg_ba, 'rg_wx': rg_wx, 'rg_bx': rg_bx, 'rg_lambda': rg_lambda,
        'ssm_conv_w': ssm_conv_w, 'ssm_conv_b': ssm_conv_b, 'ssm_dt_bias': ssm_dt_bias,
        'ssm_a_log': ssm_a_log, 'ssm_d': ssm_d, 'ssm_norm_w': ssm_norm_w,
    }
    lb_cum = jnp.cumsum(jax.nn.softmax(hg_lb_logits.astype(F32), axis=0), axis=0)
    lower_bounds = lb_cum - lb_cum[:1]

    w1g, w1u, w1d = (w.astype(BF16) for w in (ffn1_w_gate, ffn1_w_up, ffn1_w_down))
    w2g, w2u, w2d = (w.astype(BF16) for w in (ffn2_w_gate, ffn2_w_up, ffn2_w_down))
    w_in_b = _permute_w_in(w_in)
    w_out_b = w_out.astype(BF16)

    taps_first = lambda a: jnp.transpose(a, (0, 2, 1, 3))
    sample_init = (state_hgrn, state_rglru, taps_first(state_rglru_conv), state_ssm, taps_first(state_ssm_conv))

    x = jnp.concatenate([x_prompt.reshape(mp, D_MODEL), x_sample.reshape(ms, D_MODEL)], axis=0)
    new_p = ([], [], [], [], [])
    new_s = ([], [], [], [], [])
    for l in range(DEPTH):
        g = norm_g[l].reshape(6, 1, D_MODEL)
        p = {name: arr[l] for name, arr in lw.items()}
        x = _ffn(x, g[0], g[1], w1g, w1u, w1d, l)
        proj = _inproj(x, g[2], w_in_b, l)
        o_hg, hg_new = _hgrn_prompt(proj, p, lower_bounds[l], bp, lp)
        o_rg, rg_h_new, rg_c_new = _rglru_prompt(proj, p, bp, lp)
        o_ssm, ssm_new, ssm_c_new = _ssd_prompt(proj, p, bp, lp)
        st_p = (hg_new, rg_h_new.reshape(bp, RG_WIDTH), rg_c_new, ssm_new, ssm_c_new)
        o_s, st_s = _sample_mix(proj, sample_init, l, p, lower_bounds[l], mp)
        x = _outproj(x, (o_hg, o_rg, o_ssm), o_s, g[3], w_out_b, l)
        x = _ffn(x, g[4], g[5], w2g, w2u, w2d, l)
        for acc, s in zip(new_p, st_p):
            acc.append(s)
        for acc, s in zip(new_s, st_s):
            acc.append(s)
    hg_p, rg_p, rgc_p, ssm_p, ssmc_p = (jnp.stack(a) for a in new_p)
    hg_s, rg_s, rgc_s, ssm_s, ssmc_s = (jnp.stack(a) for a in new_s)
    rgc_s, ssmc_s = taps_first(rgc_s), taps_first(ssmc_s)
    y_prompt = x[:mp].reshape(bp, lp, D_MODEL)
    y_sample = x[mp:].reshape(bs, ls, D_MODEL)
    return (y_prompt, y_sample, hg_p, hg_s, rg_p, rg_s, rgc_p, rgc_s, ssm_p, ssm_s, ssmc_p, ssmc_s)
```

```python
import functools
import math

import jax
import jax.numpy as jnp
import numpy as np
from jax import lax
from jax.experimental import pallas as pl
from jax.experimental.pallas import tpu as pltpu

F32 = jnp.float32
BF16 = jnp.bfloat16

D_MODEL = 2048
DEPTH = 2
EPS = 1e-6
CONV_W = 4
HG_HEADS = 4
HG_DK = 128
HG_DV = 128
HG_WIDTH = HG_HEADS * HG_DV
HG_CHUNK = 32
RG_HEADS = 6
RG_HEAD_DIM = 128
RG_WIDTH = RG_HEADS * RG_HEAD_DIM
RG_C = 8.0
SSM_HEADS = 12
SSM_HEADDIM = 64
SSM_WIDTH = SSM_HEADS * SSM_HEADDIM
SSM_GROUPS = 2
HEADS_PER_GROUP = SSM_HEADS // SSM_GROUPS
SSM_DSTATE = 128
SSM_CHUNK = 64
SSM_CONV_DIM = SSM_WIDTH + 2 * SSM_GROUPS * SSM_DSTATE
D_MIX = HG_WIDTH + RG_WIDTH + SSM_WIDTH
IN_SIZES = (HG_HEADS * HG_DK, HG_HEADS * HG_DK, HG_WIDTH, HG_WIDTH, RG_WIDTH, RG_WIDTH,
            SSM_WIDTH, SSM_CONV_DIM, SSM_HEADS)
D_IN_PROJ = sum(IN_SIZES)
D_FF = 5632

LANES = 128
D_IN_PAD = -(-D_IN_PROJ // LANES) * LANES
ROW_TILE = 640
FFN_ROWS = 1040
FF_TILE = 256
IN_TILE = 1152
VMEM_LIMIT = 56 * 1024 * 1024
FFN_VMEM_LIMIT = 62 * 1024 * 1024


def _rms(x, g):
    return x * lax.rsqrt(jnp.mean(x * x, axis=-1, keepdims=True) + EPS) * g


def _ffn_kernel(x_ref, gin_ref, gout_ref, wg_ref, wu_ref, wd_ref, o_ref, xn_ref):
    j = pl.program_id(1)

    @pl.when(j == 0)
    def _():
        xn_ref[...] = _rms(x_ref[...], gin_ref[...]).astype(BF16)
        o_ref[...] = jnp.zeros_like(o_ref)

    xn = xn_ref[...]
    g = jnp.dot(xn, wg_ref[...].astype(BF16), preferred_element_type=F32)
    u = jnp.dot(xn, wu_ref[...].astype(BF16), preferred_element_type=F32)
    h = (g * jax.nn.sigmoid(g) * u).astype(BF16)
    o_ref[...] += jnp.dot(h, wd_ref[...].astype(BF16), preferred_element_type=F32)

    @pl.when(j == pl.num_programs(1) - 1)
    def _():
        o_ref[...] = x_ref[...] + 0.5 * _rms(o_ref[...], gout_ref[...])


def _ffn(x, g_in, g_out, wg, wu, wd, layer):
    m = x.shape[0]
    return pl.pallas_call(
        _ffn_kernel,
        grid=(m // FFN_ROWS, D_FF // FF_TILE),
        in_specs=[
            pl.BlockSpec((FFN_ROWS, D_MODEL), lambda i, j: (i, 0)),
            pl.BlockSpec((1, D_MODEL), lambda i, j: (0, 0)),
            pl.BlockSpec((1, D_MODEL), lambda i, j: (0, 0)),
            pl.BlockSpec((None, D_MODEL, FF_TILE), lambda i, j: (layer, 0, j)),
            pl.BlockSpec((None, D_MODEL, FF_TILE), lambda i, j: (layer, 0, j)),
            pl.BlockSpec((None, FF_TILE, D_MODEL), lambda i, j: (layer, j, 0)),
        ],
        out_specs=pl.BlockSpec((FFN_ROWS, D_MODEL), lambda i, j: (i, 0)),
        out_shape=jax.ShapeDtypeStruct((m, D_MODEL), F32),
        scratch_shapes=[pltpu.VMEM((FFN_ROWS, D_MODEL), BF16)],
        compiler_params=pltpu.CompilerParams(
            dimension_semantics=("parallel", "arbitrary"), vmem_limit_bytes=FFN_VMEM_LIMIT),
        name="ffn",
    )(x, g_in, g_out, wg, wu, wd)


def _inproj_kernel(x_ref, g_ref, w_ref, o_ref, xn_ref):
    @pl.when(pl.program_id(1) == 0)
    def _():
        xn_ref[...] = _rms(x_ref[...], g_ref[...]).astype(BF16)

    o_ref[...] = jnp.dot(xn_ref[...], w_ref[...], preferred_element_type=F32)


def _inproj(x, g, w_in, layer):
    m = x.shape[0]
    return pl.pallas_call(
        _inproj_kernel,
        grid=(m // ROW_TILE, D_IN_PAD // IN_TILE),
        in_specs=[
            pl.BlockSpec((ROW_TILE, D_MODEL), lambda i, j: (i, 0)),
            pl.BlockSpec((1, D_MODEL), lambda i, j: (0, 0)),
            pl.BlockSpec((None, D_MODEL, IN_TILE), lambda i, j: (layer, 0, j)),
        ],
        out_specs=pl.BlockSpec((ROW_TILE, IN_TILE), lambda i, j: (i, j)),
        out_shape=jax.ShapeDtypeStruct((m, D_IN_PAD), F32),
        scratch_shapes=[pltpu.VMEM((ROW_TILE, D_MODEL), BF16)],
        compiler_params=pltpu.CompilerParams(
            dimension_semantics=("parallel", "arbitrary"), vmem_limit_bytes=VMEM_LIMIT),
        name="inproj",
    )(x, g, w_in)


OUT_TILE = 512


def _outproj_kernel(x_ref, php_ref, prg_ref, pss_ref, shg_ref, srg_ref, sss_ref, g_ref, w_ref, y_ref):
    i = pl.program_id(0)
    last = pl.num_programs(0) - 1
    r0, r1 = HG_WIDTH, HG_WIDTH + RG_WIDTH

    def mixed(ohg, org, oss):
        m = jnp.dot(ohg, w_ref[0:r0, :], preferred_element_type=F32)
        m += jnp.dot(org, w_ref[r0:r1, :], preferred_element_type=F32)
        m += jnp.dot(oss, w_ref[r1:, :], preferred_element_type=F32)
        return _rms(m, g_ref[...])

    @pl.when(i < last)
    def _():
        y_ref[...] = x_ref[...] + mixed(php_ref[...], prg_ref[...], pss_ref[...])

    @pl.when(i == last)
    def _():
        ns = shg_ref.shape[0]
        y_ref[0:ns, :] = x_ref[0:ns, :] + mixed(shg_ref[...], srg_ref[...], sss_ref[...])


def _outproj(x, o_prompt, o_sample, g, w_out, layer):
    mp, ms = o_prompt[0].shape[0], o_sample[0].shape[0]
    assert mp % OUT_TILE == 0 and ms <= OUT_TILE and x.shape[0] == mp + ms
    n_p = mp // OUT_TILE
    widths = (HG_WIDTH, RG_WIDTH, SSM_WIDTH)
    return pl.pallas_call(
        _outproj_kernel,
        grid=(n_p + 1,),
        in_specs=[pl.BlockSpec((OUT_TILE, D_MODEL), lambda i: (i, 0))]
        + [pl.BlockSpec((OUT_TILE, w), lambda i: (jnp.minimum(i, n_p - 1), 0)) for w in widths]
        + [pl.BlockSpec((ms, w), lambda i: (0, 0)) for w in widths]
        + [pl.BlockSpec((1, D_MODEL), lambda i: (0, 0)),
           pl.BlockSpec((None, D_MIX, D_MODEL), lambda i: (layer, 0, 0))],
        out_specs=pl.BlockSpec((OUT_TILE, D_MODEL), lambda i: (i, 0)),
        out_shape=jax.ShapeDtypeStruct((mp + ms, D_MODEL), F32),
        compiler_params=pltpu.CompilerParams(
            dimension_semantics=("parallel",), vmem_limit_bytes=VMEM_LIMIT),
        name="outproj",
    )(x, *o_prompt, *o_sample, g, w_out)


_COL = np.cumsum((0,) + IN_SIZES)
HGQ_OFF, HGF_OFF, HGV_OFF, HGG_OFF, RGX_OFF, RGG_OFF, SSZ_OFF, SSX_OFF, SSDT_OFF = (int(c) for c in _COL[:9])
PIECE = 256
assert all(off % PIECE == 0 for off in (RGX_OFF, RGG_OFF, SSZ_OFF, SSX_OFF)) and SSDT_OFF % LANES == 0
assert all(w % PIECE == 0 for w in (RG_WIDTH, SSM_WIDTH, SSM_CONV_DIM))


def _pieces(off, width, rows, row_map):
    return [pl.BlockSpec((rows, PIECE), functools.partial(lambda k, *g: (row_map(*g), k), off // PIECE + k))
            for k in range(width // PIECE)]


def _cat(refs, rows=slice(None)):
    return jnp.concatenate([r[rows, :] for r in refs], axis=1)


SUBLANES = 8
SSD_T = 256
NEG_BIG = -1e30


def _split3(x):
    hi = x.astype(BF16)
    r = x - hi.astype(F32)
    mid = r.astype(BF16)
    lo = (r - mid.astype(F32)).astype(BF16)
    return hi, mid, lo


def _cumsum_rows(tri, x):
    w = x.shape[1]
    parts = jnp.concatenate(_split3(x), axis=1)
    r = jnp.dot(tri, parts, preferred_element_type=F32)
    return r[:, :w] + r[:, w:2 * w] + r[:, 2 * w:]


def _silu(x):
    return x * jax.nn.sigmoid(x)


def _neg_expm1(x):
    series = -x * (1.0 + x * (1 / 2 + x * (1 / 6 + x * (1 / 24 + x * (1 / 120 + x * (1 / 720 + x / 5040))))))
    return jnp.where(x > -0.25, series, 1.0 - jnp.exp(x))


def _softplus(x):
    return jnp.maximum(x, 0.0) + jnp.log(1.0 + jnp.exp(-jnp.abs(x)))


def _conv4(xp_ref, tail_ref, x, w_ref, b_ref, first):
    t = x.shape[0]

    @pl.when(first)
    def _():
        tail_ref[...] = jnp.zeros_like(tail_ref)

    xp_ref[0:SUBLANES, :] = tail_ref[...]
    xp_ref[SUBLANES:, :] = x
    tail_ref[...] = x[t - SUBLANES:, :]
    y = b_ref[...] + w_ref[CONV_W - 1:CONV_W, :] * x
    for k in range(CONV_W - 1):
        off = SUBLANES - (CONV_W - 1) + k
        y = y + w_ref[k:k + 1, :] * xp_ref[off:off + t, :]
    return y


N_Z, N_XBC = SSM_WIDTH // PIECE, SSM_CONV_DIM // PIECE


def _ssd_prompt_kernel(*refs):
    z_refs, xbc_refs = refs[:N_Z], refs[N_Z:N_Z + N_XBC]
    (dt_ref, tri_ref, cw_ref, cb_ref, dtb_ref, alog_ref, dexp_ref, nw_ref,
     o_ref, st_ref, cst_ref, xp_ref, tail_ref, s_ref, y_ref) = refs[N_Z + N_XBC:]
    n = pl.program_id(1)
    t = SSD_T
    z = _cat(z_refs)
    xbc_raw = _cat(xbc_refs)
    dt_raw = dt_ref[...]

    @pl.when(n == 0)
    def _():
        s_ref[...] = jnp.zeros_like(s_ref)

    xbc = _silu(_conv4(xp_ref, tail_ref, xbc_raw, cw_ref, cb_ref, n == 0))
    cst_ref[...] = _cat(xbc_refs, slice(t - (CONV_W - 1), t))
    xs = xbc[:, :SSM_WIDTH]
    gs = SSM_GROUPS * SSM_DSTATE
    bm = xbc[:, SSM_WIDTH:SSM_WIDTH + gs].astype(BF16)
    cm = xbc[:, SSM_WIDTH + gs:].astype(BF16)

    dt = _softplus(dt_raw + dtb_ref[...])
    a = dt * -jnp.exp(alog_ref[...])
    cs = _cumsum_rows(tri_ref[...], a)
    cs_t = cs.T
    cs_last = cs[t - 1:t, :]
    e_cs = jnp.exp(cs)
    e_rem = jnp.exp(cs_last - cs)
    e_last = jnp.exp(cs_last)
    row = lax.broadcasted_iota(jnp.int32, (t, t), 0)
    col = lax.broadcasted_iota(jnp.int32, (t, t), 1)
    causal = row >= col

    for g in range(SSM_GROUPS):
        bg = bm[:, g * SSM_DSTATE:(g + 1) * SSM_DSTATE]
        cg = cm[:, g * SSM_DSTATE:(g + 1) * SSM_DSTATE]
        cb = lax.dot_general(cg, bg, (((1,), (1,)), ((), ())), preferred_element_type=F32)
        for hh in range(HEADS_PER_GROUP):
            h = g * HEADS_PER_GROUP + hh
            sl = slice(h * SSM_HEADDIM, (h + 1) * SSM_HEADDIM)
            seg = jnp.where(causal, cs[:, h:h + 1] - cs_t[h:h + 1, :], NEG_BIG)
            scores = (cb * jnp.exp(seg)).astype(BF16)
            xdt = xs[:, sl] * dt[:, h:h + 1]
            s_prev = s_ref[h]
            y = jnp.dot(scores, xdt.astype(BF16), preferred_element_type=F32)
            y_in = lax.dot_general(cg, s_prev.astype(BF16), (((1,), (1,)), ((), ())),
                                   preferred_element_type=F32)
            y_ref[:, sl] = y + e_cs[:, h:h + 1] * y_in
            w = (xdt * e_rem[:, h:h + 1]).astype(BF16)
            st = lax.dot_general(w, bg, (((0,), (0,)), ((), ())), preferred_element_type=F32)
            s_ref[h] = e_last[:, h:h + 1] * s_prev + st

    y = (y_ref[...] + dexp_ref[...] * xs) * _silu(z)
    gw = SSM_WIDTH // SSM_GROUPS
    for g in range(SSM_GROUPS):
        yg = y[:, g * gw:(g + 1) * gw]
        yg = yg * lax.rsqrt(jnp.mean(yg * yg, axis=-1, keepdims=True) + EPS)
        o_ref[:, g * gw:(g + 1) * gw] = (yg * nw_ref[:, g * gw:(g + 1) * gw]).astype(BF16)

    @pl.when(n == pl.num_programs(1) - 1)
    def _():
        st_ref[...] = s_ref[...]


def _tri(t):
    return jnp.tril(jnp.ones((t, t), F32)).astype(BF16)


def _ssd_prompt(proj, p, bp, lp):
    nc = lp // SSD_T
    full = lambda b, n: (0, 0)
    rows = lambda b, n: b * nc + n
    dtb = jnp.pad(p['ssm_dt_bias'].reshape(1, SSM_HEADS), ((0, 0), (0, LANES - SSM_HEADS)))
    alog = jnp.pad(p['ssm_a_log'].reshape(1, SSM_HEADS), ((0, 0), (0, LANES - SSM_HEADS)))
    dexp = jnp.repeat(p['ssm_d'], SSM_HEADDIM).reshape(1, SSM_WIDTH)
    return pl.pallas_call(
        _ssd_prompt_kernel,
        grid=(bp, nc),
        in_specs=_pieces(SSZ_OFF, SSM_WIDTH, SSD_T, rows) + _pieces(SSX_OFF, SSM_CONV_DIM, SSD_T, rows) + [
            pl.BlockSpec((SSD_T, LANES), lambda b, n: (rows(b, n), SSDT_OFF // LANES)),
            pl.BlockSpec((SSD_T, SSD_T), full),
            pl.BlockSpec((CONV_W, SSM_CONV_DIM), full),
            pl.BlockSpec((1, SSM_CONV_DIM), full),
            pl.BlockSpec((1, LANES), full),
            pl.BlockSpec((1, LANES), full),
            pl.BlockSpec((1, SSM_WIDTH), full),
            pl.BlockSpec((1, SSM_WIDTH), full),
        ],
        out_specs=[
            pl.BlockSpec((SSD_T, SSM_WIDTH), lambda b, n: (b * nc + n, 0)),
            pl.BlockSpec((None, SSM_HEADS, SSM_HEADDIM, SSM_DSTATE), lambda b, n: (b, 0, 0, 0)),
            pl.BlockSpec((None, CONV_W - 1, SSM_CONV_DIM), lambda b, n: (b, 0, 0)),
        ],
        out_shape=[
            jax.ShapeDtypeStruct((bp * lp, SSM_WIDTH), BF16),
            jax.ShapeDtypeStruct((bp, SSM_HEADS, SSM_HEADDIM, SSM_DSTATE), F32),
            jax.ShapeDtypeStruct((bp, CONV_W - 1, SSM_CONV_DIM), F32),
        ],
        scratch_shapes=[
            pltpu.VMEM((SSD_T + SUBLANES, SSM_CONV_DIM), F32),
            pltpu.VMEM((SUBLANES, SSM_CONV_DIM), F32),
            pltpu.VMEM((SSM_HEADS, SSM_HEADDIM, SSM_DSTATE), F32),
            pltpu.VMEM((SSD_T, SSM_WIDTH), F32),
        ],
        compiler_params=pltpu.CompilerParams(
            dimension_semantics=("parallel", "arbitrary"), vmem_limit_bytes=VMEM_LIMIT),
        name="ssd_prompt",
    )(*([proj] * (N_Z + N_XBC + 1)), _tri(SSD_T), p['ssm_conv_w'], p['ssm_conv_b'].reshape(1, SSM_CONV_DIM), dtb, alog, dexp,
      p['ssm_norm_w'].reshape(1, SSM_WIDTH))


HG_T = 256

_NT = (((1,), (1,)), ((), ()))
_TN = (((0,), (0,)), ((), ()))


HG_COLS = 4 * HG_WIDTH
assert (HGQ_OFF, HGF_OFF, HGV_OFF, HGG_OFF) == (0, HG_WIDTH, 2 * HG_WIDTH, 3 * HG_WIDTH)


def _hg_tile(p_ref, off, head, rows=slice(None)):
    return p_ref[rows, off + head * LANES:off + (head + 1) * LANES]


def _hgrn_prompt_kernel(hg_ref, lb_ref, nw_ref, tri_ref, o_ref, st_ref, s_ref):
    n = pl.program_id(1)
    t, c = HG_T, HG_CHUNK
    nch = t // c

    @pl.when(n == 0)
    def _():
        s_ref[...] = jnp.zeros_like(s_ref)

    row = lax.broadcasted_iota(jnp.int32, (t, t), 0)
    col = lax.broadcasted_iota(jnp.int32, (t, t), 1)
    keep = (row >= col) & (row // c == col // c)
    tri = tri_ref[...]

    v, o_intra, kd, qe, decay = [], [], [], [], []
    for h in range(HG_HEADS):
        sl = slice(h * LANES, (h + 1) * LANES)
        lb = lb_ref[:, sl]
        fz = _hg_tile(hg_ref, HGF_OFF, h)
        logf = jnp.log(lb + (1.0 - lb) * jax.nn.sigmoid(fz))
        kk3 = ((1.0 - lb) * jax.nn.sigmoid(-fz)).reshape(nch, c, HG_DK)
        qh3 = _silu(_hg_tile(hg_ref, HGQ_OFF, h)).reshape(nch, c, HG_DK)
        vh = _hg_tile(hg_ref, HGV_OFF, h).astype(BF16)
        b3 = _cumsum_rows(tri, logf).reshape(nch, c, HG_DK)
        b_mid = b3[:, c // 2:c // 2 + 1, :]
        b_last = b3[:, c - 1:c, :]
        q_in = (qh3 * jnp.exp(b3 - b_mid)).reshape(t, HG_DK).astype(BF16)
        k_in = (kk3 * jnp.exp(b_mid - b3)).reshape(t, HG_DK).astype(BF16)
        a = jnp.where(keep, lax.dot_general(q_in, k_in, _NT, preferred_element_type=F32), 0.0)
        v.append(vh)
        o_intra.append(jnp.dot(a.astype(BF16), vh, preferred_element_type=F32))
        kd.append((kk3 * jnp.exp(b_last - b3)).astype(BF16))
        qe.append((qh3 * jnp.exp(b3)).astype(BF16))
        decay.append(jnp.exp(b_last))

    outs = [[] for _ in range(HG_HEADS)]
    for ci in range(nch):
        rows = slice(ci * c, (ci + 1) * c)
        for h in range(HG_HEADS):
            s_t = s_ref[h]
            o_inter = lax.dot_general(qe[h][ci], s_t.astype(BF16), _NT, preferred_element_type=F32)
            outs[h].append(o_intra[h][rows, :] + o_inter)
            ds_t = lax.dot_general(v[h][rows, :], kd[h][ci], _TN, preferred_element_type=F32)
            s_ref[h] = s_t * decay[h][ci] + ds_t

    for h in range(HG_HEADS):
        sl = slice(h * LANES, (h + 1) * LANES)
        o = jnp.concatenate(outs[h], axis=0)
        o = o * lax.rsqrt(jnp.mean(o * o, axis=-1, keepdims=True) + EPS)
        g = _hg_tile(hg_ref, HGG_OFF, h)
        o_ref[:, sl] = (o * nw_ref[:, sl] * _silu(g)).astype(BF16)

    @pl.when(n == pl.num_programs(1) - 1)
    def _():
        for h in range(HG_HEADS):
            st_ref[h] = s_ref[h].T


def _tri_chunks(t, c):
    r = np.arange(t)
    return jnp.asarray((r[:, None] >= r[None, :]) & (r[:, None] // c == r[None, :] // c), BF16)


def _hgrn_prompt(proj, p, lb, bp, lp):
    nc = lp // HG_T
    const = lambda b, n: (0, 0)
    return pl.pallas_call(
        _hgrn_prompt_kernel,
        grid=(bp, nc),
        in_specs=[
            pl.BlockSpec((HG_T, HG_COLS), lambda b, n: (b * nc + n, 0)),
            pl.BlockSpec((1, HG_WIDTH), const),
            pl.BlockSpec((1, HG_WIDTH), const),
            pl.BlockSpec((HG_T, HG_T), const),
        ],
        out_specs=[
            pl.BlockSpec((HG_T, HG_WIDTH), lambda b, n: (b * nc + n, 0)),
            pl.BlockSpec((None, HG_HEADS, HG_DK, HG_DV), lambda b, n: (b, 0, 0, 0)),
        ],
        out_shape=[
            jax.ShapeDtypeStruct((bp * lp, HG_WIDTH), BF16),
            jax.ShapeDtypeStruct((bp, HG_HEADS, HG_DK, HG_DV), F32),
        ],
        scratch_shapes=[pltpu.VMEM((HG_HEADS, HG_DV, HG_DK), F32)],
        compiler_params=pltpu.CompilerParams(
            dimension_semantics=("parallel", "arbitrary"), vmem_limit_bytes=VMEM_LIMIT),
        name="hgrn_prompt",
    )(proj, lb.reshape(1, HG_WIDTH), p['hg_norm_w'].reshape(1, HG_WIDTH),
      _tri_chunks(HG_T, HG_CHUNK))


RG_T = 256


def _scan_rows(a, u):
    t = a.shape[0]
    rows = lax.broadcasted_iota(jnp.int32, a.shape, 0)
    s = 1
    while s < t:
        keep = rows >= s
        a_sh = jnp.where(keep, pltpu.roll(a, s, axis=0), 1.0)
        u_sh = jnp.where(keep, pltpu.roll(u, s, axis=0), 0.0)
        u = u + a * u_sh
        a = a * a_sh
        s *= 2
    return a, u


def _rg_gates(xc, wa_ref, ba_ref, wx_ref, bx_ref, lam_ref):
    xb = xc.astype(BF16)
    ra, ri = [], []
    for h in range(RG_HEADS):
        sl = slice(h * RG_HEAD_DIM, (h + 1) * RG_HEAD_DIM)
        ra.append(jnp.dot(xb[:, sl], wa_ref[h], preferred_element_type=F32))
        ri.append(jnp.dot(xb[:, sl], wx_ref[h], preferred_element_type=F32))
    r = jax.nn.sigmoid(jnp.concatenate(ra, axis=1) + ba_ref[...])
    ig = jax.nn.sigmoid(jnp.concatenate(ri, axis=1) + bx_ref[...])
    log_a = -RG_C * r * _softplus(-lam_ref[...])
    a = jnp.exp(log_a)
    mult = jnp.sqrt(_neg_expm1(2.0 * log_a))
    return a, mult, ig


def _gelu_tanh(x):
    return 0.5 * x * (1.0 + jnp.tanh(math.sqrt(2.0 / math.pi) * (x + 0.044715 * (x * x * x))))


N_RG = RG_WIDTH // PIECE


def _rglru_prompt_kernel(*refs):
    x_refs, gate_refs = refs[:N_RG], refs[N_RG:2 * N_RG]
    (cw_ref, cb_ref, wa_ref, ba_ref, wx_ref, bx_ref, lam_ref,
     o_ref, h_ref, cst_ref, xp_ref, tail_ref, hprev_ref) = refs[2 * N_RG:]
    n = pl.program_id(1)
    t = RG_T

    @pl.when(n == 0)
    def _():
        hprev_ref[...] = jnp.zeros_like(hprev_ref)

    x = _cat(x_refs)
    xc = _conv4(xp_ref, tail_ref, x, cw_ref, cb_ref, n == 0)
    cst_ref[...] = _cat(x_refs, slice(t - (CONV_W - 1), t))
    a, mult, ig = _rg_gates(xc, wa_ref, ba_ref, wx_ref, bx_ref, lam_ref)
    rows = lax.broadcasted_iota(jnp.int32, a.shape, 0)
    mult = jnp.where((rows == 0) & (n == 0), 1.0, mult)
    a_cum, h = _scan_rows(a, mult * ig * xc)
    h = h + a_cum * hprev_ref[...]
    hprev_ref[...] = h[t - 1:t, :]
    h_ref[...] = h[t - 1:t, :]
    o_ref[...] = (h * _gelu_tanh(_cat(gate_refs))).astype(BF16)


def _rglru_prompt(proj, p, bp, lp):
    nc = lp // RG_T
    full2 = lambda b, n: (0, 0)
    full3 = lambda b, n: (0, 0, 0)
    rows = lambda b, n: b * nc + n
    row = lambda a: a.reshape(1, RG_WIDTH)
    return pl.pallas_call(
        _rglru_prompt_kernel,
        grid=(bp, nc),
        in_specs=_pieces(RGX_OFF, RG_WIDTH, RG_T, rows) + _pieces(RGG_OFF, RG_WIDTH, RG_T, rows) + [
            pl.BlockSpec((CONV_W, RG_WIDTH), full2),
            pl.BlockSpec((1, RG_WIDTH), full2),
            pl.BlockSpec((RG_HEADS, RG_HEAD_DIM, RG_HEAD_DIM), full3),
            pl.BlockSpec((1, RG_WIDTH), full2),
            pl.BlockSpec((RG_HEADS, RG_HEAD_DIM, RG_HEAD_DIM), full3),
            pl.BlockSpec((1, RG_WIDTH), full2),
            pl.BlockSpec((1, RG_WIDTH), full2),
        ],
        out_specs=[
            pl.BlockSpec((RG_T, RG_WIDTH), lambda b, n: (b * nc + n, 0)),
            pl.BlockSpec((None, 1, RG_WIDTH), lambda b, n: (b, 0, 0)),
            pl.BlockSpec((None, CONV_W - 1, RG_WIDTH), lambda b, n: (b, 0, 0)),
        ],
        out_shape=[
            jax.ShapeDtypeStruct((bp * lp, RG_WIDTH), BF16),
            jax.ShapeDtypeStruct((bp, 1, RG_WIDTH), F32),
            jax.ShapeDtypeStruct((bp, CONV_W - 1, RG_WIDTH), F32),
        ],
        scratch_shapes=[
            pltpu.VMEM((RG_T + SUBLANES, RG_WIDTH), F32),
            pltpu.VMEM((SUBLANES, RG_WIDTH), F32),
            pltpu.VMEM((1, RG_WIDTH), F32),
        ],
        compiler_params=pltpu.CompilerParams(
            dimension_semantics=("parallel", "arbitrary"), vmem_limit_bytes=VMEM_LIMIT),
        name="rglru_prompt",
    )(*([proj] * (2 * N_RG)), p['rg_conv_w'], row(p['rg_conv_b']), p['rg_wa'].astype(BF16), row(p['rg_ba']),
      p['rg_wx'].astype(BF16), row(p['rg_bx']), row(p['rg_lambda']))


SB = 8


def _col_bcast(row):
    return jnp.broadcast_to(row, (LANES, LANES)).T


def _sample_kernel(p_ref,
                   hs_ref, rh_ref, rc_ref, ss_ref, sc_ref,
                   lb_ref, hnw_ref,
                   rcw_ref, rcb_ref, wa_ref, ba_ref, wx_ref, bx_ref, lam_ref,
                   scw_ref, scb_ref, dtb_ref, aexp_ref, dexp_ref, snw_ref, expand_ref,
                   ohg_ref, org_ref, ossm_ref, hs_out, rh_out, rc_out, ss_out, sc_out,
                   f_sc, kk_sc, q_sc, o_sc, adec_sc, xdt_sc, b_sc, c_sc, y_sc):
    x = p_ref[:, RGX_OFF:RGX_OFF + RG_WIDTH]
    xc = rcb_ref[...] + rcw_ref[CONV_W - 1:CONV_W, :] * x
    for k in range(CONV_W - 1):
        xc = xc + rcw_ref[k:k + 1, :] * rc_ref[k]
        rc_out[k] = x if k == CONV_W - 2 else rc_ref[k + 1]
    a, mult, ig = _rg_gates(xc, wa_ref, ba_ref, wx_ref, bx_ref, lam_ref)
    h = a * rh_ref[...] + mult * ig * xc
    rh_out[...] = h
    org_ref[...] = (h * _gelu_tanh(p_ref[:, RGG_OFF:RGG_OFF + RG_WIDTH])).astype(BF16)

    z = p_ref[:, SSZ_OFF:SSZ_OFF + SSM_WIDTH]
    xbc_raw = p_ref[:, SSX_OFF:SSX_OFF + SSM_CONV_DIM]
    dt_raw = p_ref[:, SSDT_OFF:SSDT_OFF + LANES]
    xbc = scb_ref[...] + scw_ref[CONV_W - 1:CONV_W, :] * xbc_raw
    for k in range(CONV_W - 1):
        xbc = xbc + scw_ref[k:k + 1, :] * sc_ref[k]
        sc_out[k] = xbc_raw if k == CONV_W - 2 else sc_ref[k + 1]
    xbc = _silu(xbc)
    xs = xbc[:, :SSM_WIDTH]
    gs = SSM_GROUPS * SSM_DSTATE
    b_sc[...] = xbc[:, SSM_WIDTH:SSM_WIDTH + gs]
    c_sc[...] = xbc[:, SSM_WIDTH + gs:]
    dt = _softplus(dt_raw + dtb_ref[...])
    parts = jnp.concatenate(_split3(dt), axis=0)
    r = jnp.dot(parts, expand_ref[...], preferred_element_type=F32)
    dt_exp = r[0:SB] + r[SB:2 * SB] + r[2 * SB:]
    xdt_sc[...] = xs * dt_exp
    adec_sc[...] = jnp.exp(dt_exp * aexp_ref[...])

    hg_tile = functools.partial(_hg_tile, p_ref)
    for hd in range(HG_HEADS):
        sl = slice(hd * LANES, (hd + 1) * LANES)
        lb = lb_ref[:, sl]
        fz = hg_tile(HGF_OFF, hd)
        f_sc[:, sl] = lb + (1.0 - lb) * jax.nn.sigmoid(fz)
        kk_sc[:, sl] = (1.0 - lb) * jax.nn.sigmoid(-fz)
        q_sc[:, sl] = _silu(hg_tile(HGQ_OFF, hd))

    for j in range(SB):
        row = slice(j, j + 1)
        for hd in range(HG_HEADS):
            sl = slice(hd * LANES, (hd + 1) * LANES)
            v_row = hg_tile(HGV_OFF, hd, row)
            s_new = _col_bcast(f_sc[row, sl]) * hs_ref[j, hd] + _col_bcast(kk_sc[row, sl]) * v_row
            hs_out[j, hd] = s_new
            q8 = jnp.broadcast_to(q_sc[row, sl], (SB, LANES)).astype(BF16)
            o_sc[row, sl] = jnp.dot(q8, s_new.astype(BF16), preferred_element_type=F32)[0:1]
        for hp in range(SSM_HEADS // 2):
            sl = slice(hp * LANES, (hp + 1) * LANES)
            g = (2 * hp) // HEADS_PER_GROUP
            gsl = slice(g * SSM_DSTATE, (g + 1) * SSM_DSTATE)
            s_old = ss_ref[j, 2 * hp:2 * hp + 2].reshape(LANES, SSM_DSTATE)
            s_new = _col_bcast(adec_sc[row, sl]) * s_old + _col_bcast(xdt_sc[row, sl]) * b_sc[row, gsl]
            ss_out[j, 2 * hp:2 * hp + 2] = s_new.reshape(2, SSM_HEADDIM, SSM_DSTATE)
            c8 = jnp.broadcast_to(c_sc[row, gsl], (SB, SSM_DSTATE)).astype(BF16)
            y_sc[row, sl] = lax.dot_general(c8, s_new.astype(BF16), _NT, preferred_element_type=F32)[0:1]

    for hd in range(HG_HEADS):
        sl = slice(hd * LANES, (hd + 1) * LANES)
        o = o_sc[:, sl]
        o = o * lax.rsqrt(jnp.mean(o * o, axis=-1, keepdims=True) + EPS)
        ohg_ref[:, sl] = (o * hnw_ref[:, sl] * _silu(hg_tile(HGG_OFF, hd))).astype(BF16)
    y = (y_sc[...] + dexp_ref[...] * xs) * _silu(z)
    gw = SSM_WIDTH // SSM_GROUPS
    for g in range(SSM_GROUPS):
        yg = y[:, g * gw:(g + 1) * gw]
        yg = yg * lax.rsqrt(jnp.mean(yg * yg, axis=-1, keepdims=True) + EPS)
        ossm_ref[:, g * gw:(g + 1) * gw] = (yg * snw_ref[:, g * gw:(g + 1) * gw]).astype(BF16)


def _sample_mix(proj, states, layer, p, lb, mp):
    hg_s, rg_h, rg_c_t, ssm_s, ssm_c_t = states
    bs = hg_s.shape[1]
    rb = mp // SB
    const2 = lambda i: (0, 0)
    const3 = lambda i: (0, 0, 0)
    row = lambda a, w: a.reshape(1, w)
    dtb = jnp.pad(p['ssm_dt_bias'].reshape(1, SSM_HEADS), ((0, 0), (0, LANES - SSM_HEADS)))
    aexp = jnp.repeat(-jnp.exp(p['ssm_a_log']), SSM_HEADDIM).reshape(1, SSM_WIDTH)
    dexp = jnp.repeat(p['ssm_d'], SSM_HEADDIM).reshape(1, SSM_WIDTH)
    expand = jnp.asarray(np.arange(LANES)[:, None] == (np.arange(SSM_WIDTH)[None, :] // SSM_HEADDIM), BF16)
    in_specs = [
        pl.BlockSpec((SB, D_IN_PAD), lambda i: (rb + i, 0)),
        pl.BlockSpec((None, SB, HG_HEADS, HG_DK, HG_DV), lambda i: (layer, i, 0, 0, 0)),
        pl.BlockSpec((None, SB, RG_WIDTH), lambda i: (layer, i, 0)),
        pl.BlockSpec((None, CONV_W - 1, SB, RG_WIDTH), lambda i: (layer, 0, i, 0)),
        pl.BlockSpec((None, SB, SSM_HEADS, SSM_HEADDIM, SSM_DSTATE), lambda i: (layer, i, 0, 0, 0)),
        pl.BlockSpec((None, CONV_W - 1, SB, SSM_CONV_DIM), lambda i: (layer, 0, i, 0)),
        pl.BlockSpec((1, HG_WIDTH), const2),
        pl.BlockSpec((1, HG_WIDTH), const2),
        pl.BlockSpec((CONV_W, RG_WIDTH), const2),
        pl.BlockSpec((1, RG_WIDTH), const2),
        pl.BlockSpec((RG_HEADS, RG_HEAD_DIM, RG_HEAD_DIM), const3),
        pl.BlockSpec((1, RG_WIDTH), const2),
        pl.BlockSpec((RG_HEADS, RG_HEAD_DIM, RG_HEAD_DIM), const3),
        pl.BlockSpec((1, RG_WIDTH), const2),
        pl.BlockSpec((1, RG_WIDTH), const2),
        pl.BlockSpec((CONV_W, SSM_CONV_DIM), const2),
        pl.BlockSpec((1, SSM_CONV_DIM), const2),
        pl.BlockSpec((1, LANES), const2),
        pl.BlockSpec((1, SSM_WIDTH), const2),
        pl.BlockSpec((1, SSM_WIDTH), const2),
        pl.BlockSpec((1, SSM_WIDTH), const2),
        pl.BlockSpec((LANES, SSM_WIDTH), const2),
    ]
    out_specs = [
        pl.BlockSpec((SB, HG_WIDTH), lambda i: (i, 0)),
        pl.BlockSpec((SB, RG_WIDTH), lambda i: (i, 0)),
        pl.BlockSpec((SB, SSM_WIDTH), lambda i: (i, 0)),
        pl.BlockSpec((SB, HG_HEADS, HG_DK, HG_DV), lambda i: (i, 0, 0, 0)),
        pl.BlockSpec((SB, RG_WIDTH), lambda i: (i, 0)),
        pl.BlockSpec((CONV_W - 1, SB, RG_WIDTH), lambda i: (0, i, 0)),
        pl.BlockSpec((SB, SSM_HEADS, SSM_HEADDIM, SSM_DSTATE), lambda i: (i, 0, 0, 0)),
        pl.BlockSpec((CONV_W - 1, SB, SSM_CONV_DIM), lambda i: (0, i, 0)),
    ]
    out_shape = [
        jax.ShapeDtypeStruct((bs, HG_WIDTH), BF16),
        jax.ShapeDtypeStruct((bs, RG_WIDTH), BF16),
        jax.ShapeDtypeStruct((bs, SSM_WIDTH), BF16),
        jax.ShapeDtypeStruct(hg_s.shape[1:], F32),
        jax.ShapeDtypeStruct(rg_h.shape[1:], F32),
        jax.ShapeDtypeStruct(rg_c_t.shape[1:], F32),
        jax.ShapeDtypeStruct(ssm_s.shape[1:], F32),
        jax.ShapeDtypeStruct(ssm_c_t.shape[1:], F32),
    ]
    outs = pl.pallas_call(
        _sample_kernel,
        grid=(bs // SB,),
        in_specs=in_specs,
        out_specs=out_specs,
        out_shape=out_shape,
        scratch_shapes=[
            pltpu.VMEM((SB, HG_WIDTH), F32), pltpu.VMEM((SB, HG_WIDTH), F32), pltpu.VMEM((SB, HG_WIDTH), F32),
            pltpu.VMEM((SB, HG_WIDTH), F32),
            pltpu.VMEM((SB, SSM_WIDTH), F32), pltpu.VMEM((SB, SSM_WIDTH), F32),
            pltpu.VMEM((SB, SSM_GROUPS * SSM_DSTATE), F32), pltpu.VMEM((SB, SSM_GROUPS * SSM_DSTATE), F32),
            pltpu.VMEM((SB, SSM_WIDTH), F32),
        ],
        compiler_params=pltpu.CompilerParams(
            dimension_semantics=("parallel",), vmem_limit_bytes=VMEM_LIMIT),
        name="sample_mix",
    )(proj, hg_s, rg_h, rg_c_t, ssm_s, ssm_c_t,
      lb.reshape(1, HG_WIDTH), row(p['hg_norm_w'], HG_WIDTH),
      p['rg_conv_w'], row(p['rg_conv_b'], RG_WIDTH), p['rg_wa'].astype(BF16), row(p['rg_ba'], RG_WIDTH),
      p['rg_wx'].astype(BF16), row(p['rg_bx'], RG_WIDTH), row(p['rg_lambda'], RG_WIDTH),
      p['ssm_conv_w'], row(p['ssm_conv_b'], SSM_CONV_DIM), dtb, aexp, dexp, row(p['ssm_norm_w'], SSM_WIDTH),
      expand)
    return tuple(outs[:3]), tuple(outs[3:])


def kernel(x_prompt, x_sample, state_hgrn, state_rglru, state_rglru_conv, state_ssm, state_ssm_conv, norm_g, ffn1_w_gate, ffn1_w_up, ffn1_w_down, ffn2_w_gate, ffn2_w_up, ffn2_w_down, w_in, w_out, hg_lb_logits, hg_norm_w, rg_conv_w, rg_conv_b, rg_wa, rg_ba, rg_wx, rg_bx, rg_lambda, ssm_conv_w, ssm_conv_b, ssm_dt_bias, ssm_a_log, ssm_d, ssm_norm_w):
    bp, lp, _ = x_prompt.shape
    bs, ls, _ = x_sample.shape
    mp = bp * lp
    ms = bs * ls
    assert (mp + ms) % ROW_TILE == 0 and (mp + ms) % FFN_ROWS == 0
    assert D_FF % FF_TILE == 0 and D_IN_PAD % IN_TILE == 0

    lw = {
        'hg_norm_w': hg_norm_w, 'rg_conv_w': rg_conv_w, 'rg_conv_b': rg_conv_b, 'rg_wa': rg_wa,
        'rg_ba': rg_ba, 'rg_wx': rg_wx, 'rg_bx': rg_bx, 'rg_lambda': rg_lambda,
        'ssm_conv_w': ssm_conv_w, 'ssm_conv_b': ssm_conv_b, 'ssm_dt_bias': ssm_dt_bias,
        'ssm_a_log': ssm_a_log, 'ssm_d': ssm_d, 'ssm_norm_w': ssm_norm_w,
    }
    lb_cum = jnp.cumsum(jax.nn.softmax(hg_lb_logits.astype(F32), axis=0), axis=0)
    lower_bounds = lb_cum - lb_cum[:1]

    w1g, w1u, w1d = ffn1_w_gate, ffn1_w_up, ffn1_w_down
    w2g, w2u, w2d = ffn2_w_gate, ffn2_w_up, ffn2_w_down
    w_in_b = jnp.pad(w_in.astype(BF16), ((0, 0), (0, 0), (0, D_IN_PAD - D_IN_PROJ)))
    w_out_b = w_out.astype(BF16)

    taps_first = lambda a: jnp.transpose(a, (0, 2, 1, 3))
    sample_init = (state_hgrn, state_rglru, taps_first(state_rglru_conv), state_ssm, taps_first(state_ssm_conv))

    x = jnp.concatenate([x_prompt.reshape(mp, D_MODEL), x_sample.reshape(ms, D_MODEL)], axis=0)
    new_p = ([], [], [], [], [])
    new_s = ([], [], [], [], [])
    for l in range(DEPTH):
        g = norm_g[l].reshape(6, 1, D_MODEL)
        p = {name: arr[l] for name, arr in lw.items()}
        x = _ffn(x, g[0], g[1], w1g, w1u, w1d, l)
        proj = _inproj(x, g[2], w_in_b, l)
        o_hg, hg_new = _hgrn_prompt(proj, p, lower_bounds[l], bp, lp)
        o_rg, rg_h_new, rg_c_new = _rglru_prompt(proj, p, bp, lp)
        o_ssm, ssm_new, ssm_c_new = _ssd_prompt(proj, p, bp, lp)
        st_p = (hg_new, rg_h_new.reshape(bp, RG_WIDTH), rg_c_new, ssm_new, ssm_c_new)
        o_s, st_s = _sample_mix(proj, sample_init, l, p, lower_bounds[l], mp)
        x = _outproj(x, (o_hg, o_rg, o_ssm), o_s, g[3], w_out_b, l)
        x = _ffn(x, g[4], g[5], w2g, w2u, w2d, l)
        for acc, s in zip(new_p, st_p):
            acc.append(s)
        for acc, s in zip(new_s, st_s):
            acc.append(s)
    hg_p, rg_p, rgc_p, ssm_p, ssmc_p = (jnp.stack(a) for a in new_p)
    hg_s, rg_s, rgc_s, ssm_s, ssmc_s = (jnp.stack(a) for a in new_s)
    rgc_s, ssmc_s = taps_first(rgc_s), taps_first(ssmc_s)
    y_prompt = x[:mp].reshape(bp, lp, D_MODEL)
    y_sample = x[mp:].reshape(bs, ls, D_MODEL)
    return (y_prompt, y_sample, hg_p, hg_s, rg_p, rg_s, rgc_p, rgc_s, ssm_p, ssm_s, ssmc_p, ssmc_s)
```

```python
import functools
import math

import jax
import jax.numpy as jnp
import numpy as np
from jax import lax
from jax.experimental import pallas as pl
from jax.experimental.pallas import tpu as pltpu

F32 = jnp.float32
BF16 = jnp.bfloat16

D_MODEL = 2048
DEPTH = 2
EPS = 1e-6
CONV_W = 4
HG_HEADS = 4
HG_DK = 128
HG_DV = 128
HG_WIDTH = HG_HEADS * HG_DV
HG_CHUNK = 32
RG_HEADS = 6
RG_HEAD_DIM = 128
RG_WIDTH = RG_HEADS * RG_HEAD_DIM
RG_C = 8.0
SSM_HEADS = 12
SSM_HEADDIM = 64
SSM_WIDTH = SSM_HEADS * SSM_HEADDIM
SSM_GROUPS = 2
HEADS_PER_GROUP = SSM_HEADS // SSM_GROUPS
SSM_DSTATE = 128
SSM_CHUNK = 64
SSM_CONV_DIM = SSM_WIDTH + 2 * SSM_GROUPS * SSM_DSTATE
D_MIX = HG_WIDTH + RG_WIDTH + SSM_WIDTH
IN_SIZES = (HG_HEADS * HG_DK, HG_HEADS * HG_DK, HG_WIDTH, HG_WIDTH, RG_WIDTH, RG_WIDTH,
            SSM_WIDTH, SSM_CONV_DIM, SSM_HEADS)
D_IN_PROJ = sum(IN_SIZES)
D_FF = 5632

LANES = 128
D_IN_PAD = -(-D_IN_PROJ // LANES) * LANES
FFN_ROWS = 1040
FF_TILE = 256
IN_ROWS = 1040
IN_TILE = 640
VMEM_LIMIT = 56 * 1024 * 1024
FFN_VMEM_LIMIT = 62 * 1024 * 1024


def _rms(x, g):
    return x * lax.rsqrt(jnp.mean(x * x, axis=-1, keepdims=True) + EPS) * g


def _ffn_kernel(*refs, split_in, split_out, n_prompt_last):
    refs = list(refs)
    x_ref = refs.pop(0)
    xs_ref = refs.pop(0) if split_in else None
    gin_ref, gout_ref, wg_ref, wu_ref, wd_ref, o_ref = refs[:6]
    ys_ref = refs[6] if split_out else None
    xn_ref = refs[-1]
    i, j = pl.program_id(0), pl.program_id(1)
    last_i = pl.num_programs(0) - 1
    npl = n_prompt_last

    @pl.when(j == 0)
    def _():
        o_ref[...] = jnp.zeros_like(o_ref)
        if split_in:
            @pl.when(i < last_i)
            def _():
                xn_ref[...] = _rms(x_ref[...], gin_ref[...]).astype(BF16)

            @pl.when(i == last_i)
            def _():
                xn_ref[0:npl, :] = _rms(x_ref[0:npl, :], gin_ref[...]).astype(BF16)
                xn_ref[npl:, :] = _rms(xs_ref[...], gin_ref[...]).astype(BF16)
        else:
            xn_ref[...] = _rms(x_ref[...], gin_ref[...]).astype(BF16)

    xn = xn_ref[...]
    g = jnp.dot(xn, wg_ref[...].astype(BF16), preferred_element_type=F32)
    u = jnp.dot(xn, wu_ref[...].astype(BF16), preferred_element_type=F32)
    h = (g * jax.nn.sigmoid(g) * u).astype(BF16)
    o_ref[...] += jnp.dot(h, wd_ref[...].astype(BF16), preferred_element_type=F32)

    @pl.when(j == pl.num_programs(1) - 1)
    def _():
        if split_in:
            @pl.when(i < last_i)
            def _():
                o_ref[...] = x_ref[...] + 0.5 * _rms(o_ref[...], gout_ref[...])

            @pl.when(i == last_i)
            def _():
                o_ref[0:npl, :] = x_ref[0:npl, :] + 0.5 * _rms(o_ref[0:npl, :], gout_ref[...])
                o_ref[npl:, :] = xs_ref[...] + 0.5 * _rms(o_ref[npl:, :], gout_ref[...])
        else:
            o_ref[...] = x_ref[...] + 0.5 * _rms(o_ref[...], gout_ref[...])
        if split_out:
            @pl.when(i == last_i)
            def _():
                ys_ref[...] = o_ref[npl:, :]


def _ffn(x, g_in, g_out, wg, wu, wd, layer, mp, ms, split_in=False, split_out=False):
    m = mp + ms
    n_tiles = m // FFN_ROWS
    n_prompt_last = mp - (n_tiles - 1) * FFN_ROWS
    assert m % FFN_ROWS == 0 and n_prompt_last + ms == FFN_ROWS and n_prompt_last % 16 == 0
    rows = pl.BlockSpec((FFN_ROWS, D_MODEL), lambda i, j: (i, 0))
    sample = pl.BlockSpec((ms, D_MODEL), lambda i, j: (0, 0))
    vec = pl.BlockSpec((1, D_MODEL), lambda i, j: (0, 0))
    xs = tuple(x) if split_in else (x,)
    return pl.pallas_call(
        functools.partial(_ffn_kernel, split_in=split_in, split_out=split_out, n_prompt_last=n_prompt_last),
        grid=(n_tiles, D_FF // FF_TILE),
        in_specs=[rows] + ([sample] if split_in else []) + [
            vec, vec,
            pl.BlockSpec((None, D_MODEL, FF_TILE), lambda i, j: (layer, 0, j)),
            pl.BlockSpec((None, D_MODEL, FF_TILE), lambda i, j: (layer, 0, j)),
            pl.BlockSpec((None, FF_TILE, D_MODEL), lambda i, j: (layer, j, 0)),
        ],
        out_specs=[rows, sample] if split_out else rows,
        out_shape=([jax.ShapeDtypeStruct((mp, D_MODEL), F32), jax.ShapeDtypeStruct((ms, D_MODEL), F32)]
                   if split_out else jax.ShapeDtypeStruct((m, D_MODEL), F32)),
        scratch_shapes=[pltpu.VMEM((FFN_ROWS, D_MODEL), BF16)],
        compiler_params=pltpu.CompilerParams(
            dimension_semantics=("parallel", "arbitrary"), vmem_limit_bytes=FFN_VMEM_LIMIT),
        name="ffn",
    )(*xs, g_in, g_out, wg, wu, wd)


def _inproj_kernel(x_ref, g_ref, w_ref, o_ref, xn_ref):
    j = pl.program_id(1)

    @pl.when(j == 0)
    def _():
        xn_ref[...] = _rms(x_ref[...], g_ref[...]).astype(BF16)

    col = lax.broadcasted_iota(jnp.int32, (1, IN_TILE), 1)
    w = jnp.where(col < D_IN_PROJ - j * IN_TILE, w_ref[...], 0.0).astype(BF16)
    o_ref[...] = jnp.dot(xn_ref[...], w, preferred_element_type=F32)


def _inproj(x, g, w_in, layer):
    m = x.shape[0]
    return pl.pallas_call(
        _inproj_kernel,
        grid=(m // IN_ROWS, D_IN_PAD // IN_TILE),
        in_specs=[
            pl.BlockSpec((IN_ROWS, D_MODEL), lambda i, j: (i, 0)),
            pl.BlockSpec((1, D_MODEL), lambda i, j: (0, 0)),
            pl.BlockSpec((None, D_MODEL, IN_TILE), lambda i, j: (layer, 0, j)),
        ],
        out_specs=pl.BlockSpec((IN_ROWS, IN_TILE), lambda i, j: (i, j)),
        out_shape=jax.ShapeDtypeStruct((m, D_IN_PAD), F32),
        scratch_shapes=[pltpu.VMEM((IN_ROWS, D_MODEL), BF16)],
        compiler_params=pltpu.CompilerParams(
            dimension_semantics=("parallel", "arbitrary"), vmem_limit_bytes=VMEM_LIMIT),
        name="inproj",
    )(x, g, w_in)


OUT_TILE = 512


def _outproj_kernel(x_ref, php_ref, prg_ref, pss_ref, shg_ref, srg_ref, sss_ref, g_ref, w_ref, y_ref, wb_ref):
    i = pl.program_id(0)
    last = pl.num_programs(0) - 1
    r0, r1 = HG_WIDTH, HG_WIDTH + RG_WIDTH

    @pl.when(i == 0)
    def _():
        wb_ref[...] = w_ref[...].astype(BF16)

    def mixed(ohg, org, oss):
        m = jnp.dot(ohg, wb_ref[0:r0, :], preferred_element_type=F32)
        m += jnp.dot(org, wb_ref[r0:r1, :], preferred_element_type=F32)
        m += jnp.dot(oss, wb_ref[r1:, :], preferred_element_type=F32)
        return _rms(m, g_ref[...])

    @pl.when(i < last)
    def _():
        y_ref[...] = x_ref[...] + mixed(php_ref[...], prg_ref[...], pss_ref[...])

    @pl.when(i == last)
    def _():
        ns = shg_ref.shape[0]
        y_ref[0:ns, :] = x_ref[0:ns, :] + mixed(shg_ref[...], srg_ref[...], sss_ref[...])


def _outproj(x, o_prompt, o_sample, g, w_out, layer):
    mp, ms = o_prompt[0].shape[0], o_sample[0].shape[0]
    assert mp % OUT_TILE == 0 and ms <= OUT_TILE and x.shape[0] == mp + ms
    n_p = mp // OUT_TILE
    widths = (HG_WIDTH, RG_WIDTH, SSM_WIDTH)
    return pl.pallas_call(
        _outproj_kernel,
        grid=(n_p + 1,),
        in_specs=[pl.BlockSpec((OUT_TILE, D_MODEL), lambda i: (i, 0))]
        + [pl.BlockSpec((OUT_TILE, w), lambda i: (jnp.minimum(i, n_p - 1), 0)) for w in widths]
        + [pl.BlockSpec((ms, w), lambda i: (0, 0)) for w in widths]
        + [pl.BlockSpec((1, D_MODEL), lambda i: (0, 0)),
           pl.BlockSpec((None, D_MIX, D_MODEL), lambda i: (layer, 0, 0), pipeline_mode=pl.Buffered(1))],
        out_specs=pl.BlockSpec((OUT_TILE, D_MODEL), lambda i: (i, 0)),
        out_shape=jax.ShapeDtypeStruct((mp + ms, D_MODEL), F32),
        scratch_shapes=[pltpu.VMEM((D_MIX, D_MODEL), BF16)],
        compiler_params=pltpu.CompilerParams(
            dimension_semantics=("arbitrary",), vmem_limit_bytes=VMEM_LIMIT),
        name="outproj",
    )(x, *o_prompt, *o_sample, g, w_out)


_COL = np.cumsum((0,) + IN_SIZES)
HGQ_OFF, HGF_OFF, HGV_OFF, HGG_OFF, RGX_OFF, RGG_OFF, SSZ_OFF, SSX_OFF, SSDT_OFF = (int(c) for c in _COL[:9])
PIECE = 256
assert all(off % PIECE == 0 for off in (RGX_OFF, RGG_OFF, SSZ_OFF, SSX_OFF)) and SSDT_OFF % LANES == 0
assert all(w % PIECE == 0 for w in (RG_WIDTH, SSM_WIDTH, SSM_CONV_DIM))


def _pieces(off, width, rows, row_map):
    return [pl.BlockSpec((rows, PIECE), functools.partial(lambda k, *g: (row_map(*g), k), off // PIECE + k))
            for k in range(width // PIECE)]


def _cat(refs, rows=slice(None)):
    return jnp.concatenate([r[rows, :] for r in refs], axis=1)


SUBLANES = 8
SSD_T = 256
NEG_BIG = -1e30


def _split3(x):
    hi = x.astype(BF16)
    r = x - hi.astype(F32)
    mid = r.astype(BF16)
    lo = (r - mid.astype(F32)).astype(BF16)
    return hi, mid, lo


def _cumsum_rows(tri, x):
    w = x.shape[1]
    parts = jnp.concatenate(_split3(x), axis=1)
    r = jnp.dot(tri, parts, preferred_element_type=F32)
    return r[:, :w] + r[:, w:2 * w] + r[:, 2 * w:]


def _silu(x):
    return x * jax.nn.sigmoid(x)


def _neg_expm1(x, exp_x):
    series = -x * (1.0 + x * (1 / 2 + x * (1 / 6 + x * (1 / 24 + x * (1 / 120)))))
    return jnp.where(x > -1 / 16, series, 1.0 - exp_x)


def _softplus(x):
    return jnp.maximum(x, 0.0) + jnp.log(1.0 + jnp.exp(-jnp.abs(x)))


def _conv4(xp_ref, tail_ref, x, w_ref, b_ref, first):
    t = x.shape[0]

    @pl.when(first)
    def _():
        tail_ref[...] = jnp.zeros_like(tail_ref)

    xp_ref[0:SUBLANES, :] = tail_ref[...]
    xp_ref[SUBLANES:, :] = x
    tail_ref[...] = x[t - SUBLANES:, :]
    y = b_ref[...] + w_ref[CONV_W - 1:CONV_W, :] * x
    for k in range(CONV_W - 1):
        off = SUBLANES - (CONV_W - 1) + k
        y = y + w_ref[k:k + 1, :] * xp_ref[off:off + t, :]
    return y


N_Z, N_XBC = SSM_WIDTH // PIECE, SSM_CONV_DIM // PIECE


def _ssd_prompt_kernel(*refs):
    z_refs, xbc_refs = refs[:N_Z], refs[N_Z:N_Z + N_XBC]
    (dt_ref, tri_ref, cw_ref, cb_ref, dtb_ref, alog_ref, dexp_ref, nw_ref,
     o_ref, st_ref, cst_ref, xp_ref, tail_ref, s_ref, y_ref) = refs[N_Z + N_XBC:]
    n = pl.program_id(1)
    t = SSD_T
    z = _cat(z_refs)
    xbc_raw = _cat(xbc_refs)
    dt_raw = dt_ref[...]

    @pl.when(n == 0)
    def _():
        s_ref[...] = jnp.zeros_like(s_ref)

    xbc = _silu(_conv4(xp_ref, tail_ref, xbc_raw, cw_ref, cb_ref, n == 0))
    cst_ref[...] = _cat(xbc_refs, slice(t - (CONV_W - 1), t))
    xs = xbc[:, :SSM_WIDTH]
    gs = SSM_GROUPS * SSM_DSTATE
    bm = xbc[:, SSM_WIDTH:SSM_WIDTH + gs].astype(BF16)
    cm = xbc[:, SSM_WIDTH + gs:].astype(BF16)

    dt = _softplus(dt_raw + dtb_ref[...])
    a = dt * -jnp.exp(alog_ref[...])
    cs = _cumsum_rows(tri_ref[...], a)
    cs_t = cs.T
    cs_last = cs[t - 1:t, :]
    e_cs = jnp.exp(cs)
    e_rem = jnp.exp(cs_last - cs)
    e_last = jnp.exp(cs_last)
    row = lax.broadcasted_iota(jnp.int32, (t, t), 0)
    col = lax.broadcasted_iota(jnp.int32, (t, t), 1)
    causal = row >= col

    for g in range(SSM_GROUPS):
        bg = bm[:, g * SSM_DSTATE:(g + 1) * SSM_DSTATE]
        cg = cm[:, g * SSM_DSTATE:(g + 1) * SSM_DSTATE]
        cb = lax.dot_general(cg, bg, (((1,), (1,)), ((), ())), preferred_element_type=F32)
        for hh in range(HEADS_PER_GROUP):
            h = g * HEADS_PER_GROUP + hh
            sl = slice(h * SSM_HEADDIM, (h + 1) * SSM_HEADDIM)
            seg = jnp.where(causal, cs[:, h:h + 1] - cs_t[h:h + 1, :], NEG_BIG)
            scores = (cb * jnp.exp(seg)).astype(BF16)
            xdt = xs[:, sl] * dt[:, h:h + 1]
            s_prev = s_ref[h]
            y = jnp.dot(scores, xdt.astype(BF16), preferred_element_type=F32)
            y_in = lax.dot_general(cg, s_prev.astype(BF16), (((1,), (1,)), ((), ())),
                                   preferred_element_type=F32)
            y_ref[:, sl] = y + e_cs[:, h:h + 1] * y_in
            w = (xdt * e_rem[:, h:h + 1]).astype(BF16)
            st = lax.dot_general(w, bg, (((0,), (0,)), ((), ())), preferred_element_type=F32)
            s_ref[h] = e_last[:, h:h + 1] * s_prev + st

    y = (y_ref[...] + dexp_ref[...] * xs) * _silu(z)
    gw = SSM_WIDTH // SSM_GROUPS
    for g in range(SSM_GROUPS):
        yg = y[:, g * gw:(g + 1) * gw]
        yg = yg * lax.rsqrt(jnp.mean(yg * yg, axis=-1, keepdims=True) + EPS)
        o_ref[:, g * gw:(g + 1) * gw] = (yg * nw_ref[:, g * gw:(g + 1) * gw]).astype(BF16)

    @pl.when(n == pl.num_programs(1) - 1)
    def _():
        st_ref[...] = s_ref[...]


def _tri(t):
    return jnp.tril(jnp.ones((t, t), F32)).astype(BF16)


def _ssd_prompt(proj, p, bp, lp):
    nc = lp // SSD_T
    full = lambda b, n: (0, 0)
    rows = lambda b, n: b * nc + n
    dtb = jnp.pad(p['ssm_dt_bias'].reshape(1, SSM_HEADS), ((0, 0), (0, LANES - SSM_HEADS)))
    alog = jnp.pad(p['ssm_a_log'].reshape(1, SSM_HEADS), ((0, 0), (0, LANES - SSM_HEADS)))
    dexp = jnp.repeat(p['ssm_d'], SSM_HEADDIM).reshape(1, SSM_WIDTH)
    return pl.pallas_call(
        _ssd_prompt_kernel,
        grid=(bp, nc),
        in_specs=_pieces(SSZ_OFF, SSM_WIDTH, SSD_T, rows) + _pieces(SSX_OFF, SSM_CONV_DIM, SSD_T, rows) + [
            pl.BlockSpec((SSD_T, LANES), lambda b, n: (rows(b, n), SSDT_OFF // LANES)),
            pl.BlockSpec((SSD_T, SSD_T), full),
            pl.BlockSpec((CONV_W, SSM_CONV_DIM), full),
            pl.BlockSpec((1, SSM_CONV_DIM), full),
            pl.BlockSpec((1, LANES), full),
            pl.BlockSpec((1, LANES), full),
            pl.BlockSpec((1, SSM_WIDTH), full),
            pl.BlockSpec((1, SSM_WIDTH), full),
        ],
        out_specs=[
            pl.BlockSpec((SSD_T, SSM_WIDTH), lambda b, n: (b * nc + n, 0)),
            pl.BlockSpec((None, SSM_HEADS, SSM_HEADDIM, SSM_DSTATE), lambda b, n: (b, 0, 0, 0)),
            pl.BlockSpec((None, CONV_W - 1, SSM_CONV_DIM), lambda b, n: (b, 0, 0)),
        ],
        out_shape=[
            jax.ShapeDtypeStruct((bp * lp, SSM_WIDTH), BF16),
            jax.ShapeDtypeStruct((bp, SSM_HEADS, SSM_HEADDIM, SSM_DSTATE), F32),
            jax.ShapeDtypeStruct((bp, CONV_W - 1, SSM_CONV_DIM), F32),
        ],
        scratch_shapes=[
            pltpu.VMEM((SSD_T + SUBLANES, SSM_CONV_DIM), F32),
            pltpu.VMEM((SUBLANES, SSM_CONV_DIM), F32),
            pltpu.VMEM((SSM_HEADS, SSM_HEADDIM, SSM_DSTATE), F32),
            pltpu.VMEM((SSD_T, SSM_WIDTH), F32),
        ],
        compiler_params=pltpu.CompilerParams(
            dimension_semantics=("parallel", "arbitrary"), vmem_limit_bytes=VMEM_LIMIT),
        name="ssd_prompt",
    )(*([proj] * (N_Z + N_XBC + 1)), _tri(SSD_T), p['ssm_conv_w'], p['ssm_conv_b'].reshape(1, SSM_CONV_DIM), dtb, alog, dexp,
      p['ssm_norm_w'].reshape(1, SSM_WIDTH))


HG_T = 256

_NT = (((1,), (1,)), ((), ()))
_TN = (((0,), (0,)), ((), ()))


HG_COLS = 4 * HG_WIDTH
assert (HGQ_OFF, HGF_OFF, HGV_OFF, HGG_OFF) == (0, HG_WIDTH, 2 * HG_WIDTH, 3 * HG_WIDTH)


def _hg_tile(p_ref, off, head, rows=slice(None)):
    return p_ref[rows, off + head * LANES:off + (head + 1) * LANES]


def _hgrn_prompt_kernel(hg_ref, lb_ref, nw_ref, tri_ref, o_ref, st_ref, s_ref):
    n = pl.program_id(1)
    t, c = HG_T, HG_CHUNK
    nch = t // c

    @pl.when(n == 0)
    def _():
        s_ref[...] = jnp.zeros_like(s_ref)

    row = lax.broadcasted_iota(jnp.int32, (t, t), 0)
    col = lax.broadcasted_iota(jnp.int32, (t, t), 1)
    keep = (row >= col) & (row // c == col // c)
    tri = tri_ref[...]

    v, o_intra, kd, qe, decay = [], [], [], [], []
    for h in range(HG_HEADS):
        sl = slice(h * LANES, (h + 1) * LANES)
        lb = lb_ref[:, sl]
        fz = _hg_tile(hg_ref, HGF_OFF, h)
        logf = jnp.log(lb + (1.0 - lb) * jax.nn.sigmoid(fz))
        kk3 = ((1.0 - lb) * jax.nn.sigmoid(-fz)).reshape(nch, c, HG_DK)
        qh3 = _silu(_hg_tile(hg_ref, HGQ_OFF, h)).reshape(nch, c, HG_DK)
        vh = _hg_tile(hg_ref, HGV_OFF, h).astype(BF16)
        b3 = _cumsum_rows(tri, logf).reshape(nch, c, HG_DK)
        b_mid = b3[:, c // 2:c // 2 + 1, :]
        b_last = b3[:, c - 1:c, :]
        q_in = (qh3 * jnp.exp(b3 - b_mid)).reshape(t, HG_DK).astype(BF16)
        k_in = (kk3 * jnp.exp(b_mid - b3)).reshape(t, HG_DK).astype(BF16)
        a = jnp.where(keep, lax.dot_general(q_in, k_in, _NT, preferred_element_type=F32), 0.0)
        v.append(vh)
        o_intra.append(jnp.dot(a.astype(BF16), vh, preferred_element_type=F32))
        kd.append((kk3 * jnp.exp(b_last - b3)).astype(BF16))
        qe.append((qh3 * jnp.exp(b3)).astype(BF16))
        decay.append(jnp.exp(b_last))

    outs = [[] for _ in range(HG_HEADS)]
    for ci in range(nch):
        rows = slice(ci * c, (ci + 1) * c)
        for h in range(HG_HEADS):
            s_t = s_ref[h]
            o_inter = lax.dot_general(qe[h][ci], s_t.astype(BF16), _NT, preferred_element_type=F32)
            outs[h].append(o_intra[h][rows, :] + o_inter)
            ds_t = lax.dot_general(v[h][rows, :], kd[h][ci], _TN, preferred_element_type=F32)
            s_ref[h] = s_t * decay[h][ci] + ds_t

    for h in range(HG_HEADS):
        sl = slice(h * LANES, (h + 1) * LANES)
        o = jnp.concatenate(outs[h], axis=0)
        o = o * lax.rsqrt(jnp.mean(o * o, axis=-1, keepdims=True) + EPS)
        g = _hg_tile(hg_ref, HGG_OFF, h)
        o_ref[:, sl] = (o * nw_ref[:, sl] * _silu(g)).astype(BF16)

    @pl.when(n == pl.num_programs(1) - 1)
    def _():
        for h in range(HG_HEADS):
            st_ref[h] = s_ref[h].T


def _tri_chunks(t, c):
    r = np.arange(t)
    return jnp.asarray((r[:, None] >= r[None, :]) & (r[:, None] // c == r[None, :] // c), BF16)


def _hgrn_prompt(proj, p, lb, bp, lp):
    nc = lp // HG_T
    const = lambda b, n: (0, 0)
    return pl.pallas_call(
        _hgrn_prompt_kernel,
        grid=(bp, nc),
        in_specs=[
            pl.BlockSpec((HG_T, HG_COLS), lambda b, n: (b * nc + n, 0)),
            pl.BlockSpec((1, HG_WIDTH), const),
            pl.BlockSpec((1, HG_WIDTH), const),
            pl.BlockSpec((HG_T, HG_T), const),
        ],
        out_specs=[
            pl.BlockSpec((HG_T, HG_WIDTH), lambda b, n: (b * nc + n, 0)),
            pl.BlockSpec((None, HG_HEADS, HG_DK, HG_DV), lambda b, n: (b, 0, 0, 0)),
        ],
        out_shape=[
            jax.ShapeDtypeStruct((bp * lp, HG_WIDTH), BF16),
            jax.ShapeDtypeStruct((bp, HG_HEADS, HG_DK, HG_DV), F32),
        ],
        scratch_shapes=[pltpu.VMEM((HG_HEADS, HG_DV, HG_DK), F32)],
        compiler_params=pltpu.CompilerParams(
            dimension_semantics=("parallel", "arbitrary"), vmem_limit_bytes=VMEM_LIMIT),
        name="hgrn_prompt",
    )(proj, lb.reshape(1, HG_WIDTH), p['hg_norm_w'].reshape(1, HG_WIDTH),
      _tri_chunks(HG_T, HG_CHUNK))


RG_T = 256


def _scan_rows(a, u, h_in):
    t, w = a.shape
    rows = lax.broadcasted_iota(jnp.int32, a.shape, 0) % SUBLANES
    s = 1
    while s < SUBLANES:
        keep = rows >= s
        a_sh = jnp.where(keep, pltpu.roll(a, s, axis=0), 1.0)
        u_sh = jnp.where(keep, pltpu.roll(u, s, axis=0), 0.0)
        u = u + a * u_sh
        a = a * a_sh
        s *= 2
    hs = []
    for g in range(t // SUBLANES):
        sl = slice(g * SUBLANES, (g + 1) * SUBLANES)
        h = u[sl, :] + a[sl, :] * h_in
        hs.append(h)
        h_in = h[SUBLANES - 1:, :]
    return jnp.concatenate(hs, axis=0)


def _rg_gates(xc, wa_ref, ba_ref, wx_ref, bx_ref, lam_ref):
    xb = xc.astype(BF16)
    ra, ri = [], []
    for h in range(RG_HEADS):
        sl = slice(h * RG_HEAD_DIM, (h + 1) * RG_HEAD_DIM)
        ra.append(jnp.dot(xb[:, sl], wa_ref[h], preferred_element_type=F32))
        ri.append(jnp.dot(xb[:, sl], wx_ref[h], preferred_element_type=F32))
    r = jax.nn.sigmoid(jnp.concatenate(ra, axis=1) + ba_ref[...])
    ig = jax.nn.sigmoid(jnp.concatenate(ri, axis=1) + bx_ref[...])
    log_a = -RG_C * r * _softplus(-lam_ref[...])
    a = jnp.exp(log_a)
    mult = jnp.sqrt(_neg_expm1(2.0 * log_a, a * a))
    return a, mult, ig


def _gelu_tanh(x):
    return 0.5 * x * (1.0 + jnp.tanh(math.sqrt(2.0 / math.pi) * (x + 0.044715 * (x * x * x))))


N_RG = RG_WIDTH // PIECE


def _rglru_prompt_kernel(*refs):
    x_refs, gate_refs = refs[:N_RG], refs[N_RG:2 * N_RG]
    (cw_ref, cb_ref, wa_ref, ba_ref, wx_ref, bx_ref, lam_ref,
     o_ref, h_ref, cst_ref, xp_ref, tail_ref, hprev_ref) = refs[2 * N_RG:]
    n = pl.program_id(1)
    t = RG_T

    @pl.when(n == 0)
    def _():
        hprev_ref[...] = jnp.zeros_like(hprev_ref)

    x = _cat(x_refs)
    xc = _conv4(xp_ref, tail_ref, x, cw_ref, cb_ref, n == 0)
    cst_ref[...] = _cat(x_refs, slice(t - (CONV_W - 1), t))
    a, mult, ig = _rg_gates(xc, wa_ref, ba_ref, wx_ref, bx_ref, lam_ref)
    rows = lax.broadcasted_iota(jnp.int32, a.shape, 0)
    mult = jnp.where((rows == 0) & (n == 0), 1.0, mult)
    h = _scan_rows(a, mult * ig * xc, hprev_ref[...])
    hprev_ref[...] = h[t - 1:t, :]
    h_ref[...] = h[t - 1:t, :]
    o_ref[...] = (h * _gelu_tanh(_cat(gate_refs))).astype(BF16)


def _rglru_prompt(proj, p, bp, lp):
    nc = lp // RG_T
    full2 = lambda b, n: (0, 0)
    full3 = lambda b, n: (0, 0, 0)
    rows = lambda b, n: b * nc + n
    row = lambda a: a.reshape(1, RG_WIDTH)
    return pl.pallas_call(
        _rglru_prompt_kernel,
        grid=(bp, nc),
        in_specs=_pieces(RGX_OFF, RG_WIDTH, RG_T, rows) + _pieces(RGG_OFF, RG_WIDTH, RG_T, rows) + [
            pl.BlockSpec((CONV_W, RG_WIDTH), full2),
            pl.BlockSpec((1, RG_WIDTH), full2),
            pl.BlockSpec((RG_HEADS, RG_HEAD_DIM, RG_HEAD_DIM), full3),
            pl.BlockSpec((1, RG_WIDTH), full2),
            pl.BlockSpec((RG_HEADS, RG_HEAD_DIM, RG_HEAD_DIM), full3),
            pl.BlockSpec((1, RG_WIDTH), full2),
            pl.BlockSpec((1, RG_WIDTH), full2),
        ],
        out_specs=[
            pl.BlockSpec((RG_T, RG_WIDTH), lambda b, n: (b * nc + n, 0)),
            pl.BlockSpec((None, 1, RG_WIDTH), lambda b, n: (b, 0, 0)),
            pl.BlockSpec((None, CONV_W - 1, RG_WIDTH), lambda b, n: (b, 0, 0)),
        ],
        out_shape=[
            jax.ShapeDtypeStruct((bp * lp, RG_WIDTH), BF16),
            jax.ShapeDtypeStruct((bp, 1, RG_WIDTH), F32),
            jax.ShapeDtypeStruct((bp, CONV_W - 1, RG_WIDTH), F32),
        ],
        scratch_shapes=[
            pltpu.VMEM((RG_T + SUBLANES, RG_WIDTH), F32),
            pltpu.VMEM((SUBLANES, RG_WIDTH), F32),
            pltpu.VMEM((1, RG_WIDTH), F32),
        ],
        compiler_params=pltpu.CompilerParams(
            dimension_semantics=("parallel", "arbitrary"), vmem_limit_bytes=VMEM_LIMIT),
        name="rglru_prompt",
    )(*([proj] * (2 * N_RG)), p['rg_conv_w'], row(p['rg_conv_b']), p['rg_wa'].astype(BF16), row(p['rg_ba']),
      p['rg_wx'].astype(BF16), row(p['rg_bx']), row(p['rg_lambda']))


SB = 8


def _col_bcast(row):
    return jnp.broadcast_to(row, (LANES, LANES)).T


def _sample_kernel(p_ref,
                   hs_ref, rh_ref, rc_ref, ss_ref, sc_ref,
                   lb_ref, hnw_ref,
                   rcw_ref, rcb_ref, wa_ref, ba_ref, wx_ref, bx_ref, lam_ref,
                   scw_ref, scb_ref, dtb_ref, aexp_ref, dexp_ref, snw_ref, expand_ref,
                   ohg_ref, org_ref, ossm_ref, hs_out, rh_out, rc_out, ss_out, sc_out,
                   f_sc, kk_sc, q_sc, o_sc, adec_sc, xdt_sc, b_sc, c_sc, y_sc):
    x = p_ref[:, RGX_OFF:RGX_OFF + RG_WIDTH]
    xc = rcb_ref[...] + rcw_ref[CONV_W - 1:CONV_W, :] * x
    for k in range(CONV_W - 1):
        xc = xc + rcw_ref[k:k + 1, :] * rc_ref[k]
        rc_out[k] = x if k == CONV_W - 2 else rc_ref[k + 1]
    a, mult, ig = _rg_gates(xc, wa_ref, ba_ref, wx_ref, bx_ref, lam_ref)
    h = a * rh_ref[...] + mult * ig * xc
    rh_out[...] = h
    org_ref[...] = (h * _gelu_tanh(p_ref[:, RGG_OFF:RGG_OFF + RG_WIDTH])).astype(BF16)

    z = p_ref[:, SSZ_OFF:SSZ_OFF + SSM_WIDTH]
    xbc_raw = p_ref[:, SSX_OFF:SSX_OFF + SSM_CONV_DIM]
    dt_raw = p_ref[:, SSDT_OFF:SSDT_OFF + LANES]
    xbc = scb_ref[...] + scw_ref[CONV_W - 1:CONV_W, :] * xbc_raw
    for k in range(CONV_W - 1):
        xbc = xbc + scw_ref[k:k + 1, :] * sc_ref[k]
        sc_out[k] = xbc_raw if k == CONV_W - 2 else sc_ref[k + 1]
    xbc = _silu(xbc)
    xs = xbc[:, :SSM_WIDTH]
    gs = SSM_GROUPS * SSM_DSTATE
    b_sc[...] = xbc[:, SSM_WIDTH:SSM_WIDTH + gs]
    c_sc[...] = xbc[:, SSM_WIDTH + gs:]
    dt = _softplus(dt_raw + dtb_ref[...])
    parts = jnp.concatenate(_split3(dt), axis=0)
    r = jnp.dot(parts, expand_ref[...], preferred_element_type=F32)
    dt_exp = r[0:SB] + r[SB:2 * SB] + r[2 * SB:]
    xdt_sc[...] = xs * dt_exp
    adec_sc[...] = jnp.exp(dt_exp * aexp_ref[...])

    hg_tile = functools.partial(_hg_tile, p_ref)
    for hd in range(HG_HEADS):
        sl = slice(hd * LANES, (hd + 1) * LANES)
        lb = lb_ref[:, sl]
        fz = hg_tile(HGF_OFF, hd)
        f_sc[:, sl] = lb + (1.0 - lb) * jax.nn.sigmoid(fz)
        kk_sc[:, sl] = (1.0 - lb) * jax.nn.sigmoid(-fz)
        q_sc[:, sl] = _silu(hg_tile(HGQ_OFF, hd))

    for j in range(SB):
        row = slice(j, j + 1)
        for hd in range(HG_HEADS):
            sl = slice(hd * LANES, (hd + 1) * LANES)
            v_row = hg_tile(HGV_OFF, hd, row)
            s_new = _col_bcast(f_sc[row, sl]) * hs_ref[j, hd] + _col_bcast(kk_sc[row, sl]) * v_row
            hs_out[j, hd] = s_new
            q8 = jnp.broadcast_to(q_sc[row, sl], (SB, LANES)).astype(BF16)
            o_sc[row, sl] = jnp.dot(q8, s_new.astype(BF16), preferred_element_type=F32)[0:1]
        for hp in range(SSM_HEADS // 2):
            sl = slice(hp * LANES, (hp + 1) * LANES)
            g = (2 * hp) // HEADS_PER_GROUP
            gsl = slice(g * SSM_DSTATE, (g + 1) * SSM_DSTATE)
            s_old = ss_ref[j, 2 * hp:2 * hp + 2].reshape(LANES, SSM_DSTATE)
            s_new = _col_bcast(adec_sc[row, sl]) * s_old + _col_bcast(xdt_sc[row, sl]) * b_sc[row, gsl]
            ss_out[j, 2 * hp:2 * hp + 2] = s_new.reshape(2, SSM_HEADDIM, SSM_DSTATE)
            c8 = jnp.broadcast_to(c_sc[row, gsl], (SB, SSM_DSTATE)).astype(BF16)
            y_sc[row, sl] = lax.dot_general(c8, s_new.astype(BF16), _NT, preferred_element_type=F32)[0:1]

    for hd in range(HG_HEADS):
        sl = slice(hd * LANES, (hd + 1) * LANES)
        o = o_sc[:, sl]
        o = o * lax.rsqrt(jnp.mean(o * o, axis=-1, keepdims=True) + EPS)
        ohg_ref[:, sl] = (o * hnw_ref[:, sl] * _silu(hg_tile(HGG_OFF, hd))).astype(BF16)
    y = (y_sc[...] + dexp_ref[...] * xs) * _silu(z)
    gw = SSM_WIDTH // SSM_GROUPS
    for g in range(SSM_GROUPS):
        yg = y[:, g * gw:(g + 1) * gw]
        yg = yg * lax.rsqrt(jnp.mean(yg * yg, axis=-1, keepdims=True) + EPS)
        ossm_ref[:, g * gw:(g + 1) * gw] = (yg * snw_ref[:, g * gw:(g + 1) * gw]).astype(BF16)


def _sample_mix(proj, states, layer, p, lb, mp):
    hg_s, rg_h, rg_c_t, ssm_s, ssm_c_t = states
    bs = hg_s.shape[1]
    rb = mp // SB
    const2 = lambda i: (0, 0)
    const3 = lambda i: (0, 0, 0)
    row = lambda a, w: a.reshape(1, w)
    dtb = jnp.pad(p['ssm_dt_bias'].reshape(1, SSM_HEADS), ((0, 0), (0, LANES - SSM_HEADS)))
    aexp = jnp.repeat(-jnp.exp(p['ssm_a_log']), SSM_HEADDIM).reshape(1, SSM_WIDTH)
    dexp = jnp.repeat(p['ssm_d'], SSM_HEADDIM).reshape(1, SSM_WIDTH)
    expand = jnp.asarray(np.arange(LANES)[:, None] == (np.arange(SSM_WIDTH)[None, :] // SSM_HEADDIM), BF16)
    in_specs = [
        pl.BlockSpec((SB, D_IN_PAD), lambda i: (rb + i, 0)),
        pl.BlockSpec((None, SB, HG_HEADS, HG_DK, HG_DV), lambda i: (layer, i, 0, 0, 0)),
        pl.BlockSpec((None, SB, RG_WIDTH), lambda i: (layer, i, 0)),
        pl.BlockSpec((None, CONV_W - 1, SB, RG_WIDTH), lambda i: (layer, 0, i, 0)),
        pl.BlockSpec((None, SB, SSM_HEADS, SSM_HEADDIM, SSM_DSTATE), lambda i: (layer, i, 0, 0, 0)),
        pl.BlockSpec((None, CONV_W - 1, SB, SSM_CONV_DIM), lambda i: (layer, 0, i, 0)),
        pl.BlockSpec((1, HG_WIDTH), const2),
        pl.BlockSpec((1, HG_WIDTH), const2),
        pl.BlockSpec((CONV_W, RG_WIDTH), const2),
        pl.BlockSpec((1, RG_WIDTH), const2),
        pl.BlockSpec((RG_HEADS, RG_HEAD_DIM, RG_HEAD_DIM), const3),
        pl.BlockSpec((1, RG_WIDTH), const2),
        pl.BlockSpec((RG_HEADS, RG_HEAD_DIM, RG_HEAD_DIM), const3),
        pl.BlockSpec((1, RG_WIDTH), const2),
        pl.BlockSpec((1, RG_WIDTH), const2),
        pl.BlockSpec((CONV_W, SSM_CONV_DIM), const2),
        pl.BlockSpec((1, SSM_CONV_DIM), const2),
        pl.BlockSpec((1, LANES), const2),
        pl.BlockSpec((1, SSM_WIDTH), const2),
        pl.BlockSpec((1, SSM_WIDTH), const2),
        pl.BlockSpec((1, SSM_WIDTH), const2),
        pl.BlockSpec((LANES, SSM_WIDTH), const2),
    ]
    out_specs = [
        pl.BlockSpec((SB, HG_WIDTH), lambda i: (i, 0)),
        pl.BlockSpec((SB, RG_WIDTH), lambda i: (i, 0)),
        pl.BlockSpec((SB, SSM_WIDTH), lambda i: (i, 0)),
        pl.BlockSpec((SB, HG_HEADS, HG_DK, HG_DV), lambda i: (i, 0, 0, 0)),
        pl.BlockSpec((SB, RG_WIDTH), lambda i: (i, 0)),
        pl.BlockSpec((CONV_W - 1, SB, RG_WIDTH), lambda i: (0, i, 0)),
        pl.BlockSpec((SB, SSM_HEADS, SSM_HEADDIM, SSM_DSTATE), lambda i: (i, 0, 0, 0)),
        pl.BlockSpec((CONV_W - 1, SB, SSM_CONV_DIM), lambda i: (0, i, 0)),
    ]
    out_shape = [
        jax.ShapeDtypeStruct((bs, HG_WIDTH), BF16),
        jax.ShapeDtypeStruct((bs, RG_WIDTH), BF16),
        jax.ShapeDtypeStruct((bs, SSM_WIDTH), BF16),
        jax.ShapeDtypeStruct(hg_s.shape[1:], F32),
        jax.ShapeDtypeStruct(rg_h.shape[1:], F32),
        jax.ShapeDtypeStruct(rg_c_t.shape[1:], F32),
        jax.ShapeDtypeStruct(ssm_s.shape[1:], F32),
        jax.ShapeDtypeStruct(ssm_c_t.shape[1:], F32),
    ]
    outs = pl.pallas_call(
        _sample_kernel,
        grid=(bs // SB,),
        in_specs=in_specs,
        out_specs=out_specs,
        out_shape=out_shape,
        scratch_shapes=[
            pltpu.VMEM((SB, HG_WIDTH), F32), pltpu.VMEM((SB, HG_WIDTH), F32), pltpu.VMEM((SB, HG_WIDTH), F32),
            pltpu.VMEM((SB, HG_WIDTH), F32),
            pltpu.VMEM((SB, SSM_WIDTH), F32), pltpu.VMEM((SB, SSM_WIDTH), F32),
            pltpu.VMEM((SB, SSM_GROUPS * SSM_DSTATE), F32), pltpu.VMEM((SB, SSM_GROUPS * SSM_DSTATE), F32),
            pltpu.VMEM((SB, SSM_WIDTH), F32),
        ],
        compiler_params=pltpu.CompilerParams(
            dimension_semantics=("parallel",), vmem_limit_bytes=VMEM_LIMIT),
        name="sample_mix",
    )(proj, hg_s, rg_h, rg_c_t, ssm_s, ssm_c_t,
      lb.reshape(1, HG_WIDTH), row(p['hg_norm_w'], HG_WIDTH),
      p['rg_conv_w'], row(p['rg_conv_b'], RG_WIDTH), p['rg_wa'].astype(BF16), row(p['rg_ba'], RG_WIDTH),
      p['rg_wx'].astype(BF16), row(p['rg_bx'], RG_WIDTH), row(p['rg_lambda'], RG_WIDTH),
      p['ssm_conv_w'], row(p['ssm_conv_b'], SSM_CONV_DIM), dtb, aexp, dexp, row(p['ssm_norm_w'], SSM_WIDTH),
      expand)
    return tuple(outs[:3]), tuple(outs[3:])


def kernel(x_prompt, x_sample, state_hgrn, state_rglru, state_rglru_conv, state_ssm, state_ssm_conv, norm_g, ffn1_w_gate, ffn1_w_up, ffn1_w_down, ffn2_w_gate, ffn2_w_up, ffn2_w_down, w_in, w_out, hg_lb_logits, hg_norm_w, rg_conv_w, rg_conv_b, rg_wa, rg_ba, rg_wx, rg_bx, rg_lambda, ssm_conv_w, ssm_conv_b, ssm_dt_bias, ssm_a_log, ssm_d, ssm_norm_w):
    bp, lp, _ = x_prompt.shape
    bs, ls, _ = x_sample.shape
    mp = bp * lp
    ms = bs * ls
    assert (mp + ms) % IN_ROWS == 0 and D_FF % FF_TILE == 0 and D_IN_PAD % IN_TILE == 0

    lw = {
        'hg_norm_w': hg_norm_w, 'rg_conv_w': rg_conv_w, 'rg_conv_b': rg_conv_b, 'rg_wa': rg_wa,
        'rg_ba': rg_ba, 'rg_wx': rg_wx, 'rg_bx': rg_bx, 'rg_lambda': rg_lambda,
        'ssm_conv_w': ssm_conv_w, 'ssm_conv_b': ssm_conv_b, 'ssm_dt_bias': ssm_dt_bias,
        'ssm_a_log': ssm_a_log, 'ssm_d': ssm_d, 'ssm_norm_w': ssm_norm_w,
    }
    lb_cum = jnp.cumsum(jax.nn.softmax(hg_lb_logits.astype(F32), axis=0), axis=0)
    lower_bounds = lb_cum - lb_cum[:1]

    taps_first = lambda a: jnp.transpose(a, (0, 2, 1, 3))
    sample_init = (state_hgrn, state_rglru, taps_first(state_rglru_conv), state_ssm, taps_first(state_ssm_conv))

    x = (x_prompt.reshape(mp, D_MODEL), x_sample.reshape(ms, D_MODEL))
    new_p = ([], [], [], [], [])
    new_s = ([], [], [], [], [])
    for l in range(DEPTH):
        g = norm_g[l].reshape(6, 1, D_MODEL)
        p = {name: arr[l] for name, arr in lw.items()}
        x = _ffn(x, g[0], g[1], ffn1_w_gate, ffn1_w_up, ffn1_w_down, l, mp, ms, split_in=(l == 0))
        proj = _inproj(x, g[2], w_in, l)
        o_hg, hg_new = _hgrn_prompt(proj, p, lower_bounds[l], bp, lp)
        o_rg, rg_h_new, rg_c_new = _rglru_prompt(proj, p, bp, lp)
        o_ssm, ssm_new, ssm_c_new = _ssd_prompt(proj, p, bp, lp)
        st_p = (hg_new, rg_h_new.reshape(bp, RG_WIDTH), rg_c_new, ssm_new, ssm_c_new)
        o_s, st_s = _sample_mix(proj, sample_init, l, p, lower_bounds[l], mp)
        x = _outproj(x, (o_hg, o_rg, o_ssm), o_s, g[3], w_out, l)
        x = _ffn(x, g[4], g[5], ffn2_w_gate, ffn2_w_up, ffn2_w_down, l, mp, ms, split_out=(l == DEPTH - 1))
        for acc, s in zip(new_p, st_p):
            acc.append(s)
        for acc, s in zip(new_s, st_s):
            acc.append(s)
    hg_p, rg_p, rgc_p, ssm_p, ssmc_p = (jnp.stack(a) for a in new_p)
    hg_s, rg_s, rgc_s, ssm_s, ssmc_s = (jnp.stack(a) for a in new_s)
    rgc_s, ssmc_s = taps_first(rgc_s), taps_first(ssmc_s)
    y_prompt = x[0].reshape(bp, lp, D_MODEL)
    y_sample = x[1].reshape(bs, ls, D_MODEL)
    return (y_prompt, y_sample, hg_p, hg_s, rg_p, rg_s, rgc_p, rgc_s, ssm_p, ssm_s, ssmc_p, ssmc_s)
```

```python
import functools
import math

import jax
import jax.numpy as jnp
import numpy as np
from jax import lax
from jax.experimental import pallas as pl
from jax.experimental.pallas import tpu as pltpu

F32 = jnp.float32
BF16 = jnp.bfloat16

D_MODEL = 2048
DEPTH = 2
EPS = 1e-6
CONV_W = 4
HG_HEADS = 4
HG_DK = 128
HG_DV = 128
HG_WIDTH = HG_HEADS * HG_DV
HG_CHUNK = 32
RG_HEADS = 6
RG_HEAD_DIM = 128
RG_WIDTH = RG_HEADS * RG_HEAD_DIM
RG_C = 8.0
SSM_HEADS = 12
SSM_HEADDIM = 64
SSM_WIDTH = SSM_HEADS * SSM_HEADDIM
SSM_GROUPS = 2
HEADS_PER_GROUP = SSM_HEADS // SSM_GROUPS
SSM_DSTATE = 128
SSM_CHUNK = 64
SSM_CONV_DIM = SSM_WIDTH + 2 * SSM_GROUPS * SSM_DSTATE
D_MIX = HG_WIDTH + RG_WIDTH + SSM_WIDTH
IN_SIZES = (HG_HEADS * HG_DK, HG_HEADS * HG_DK, HG_WIDTH, HG_WIDTH, RG_WIDTH, RG_WIDTH,
            SSM_WIDTH, SSM_CONV_DIM, SSM_HEADS)
D_IN_PROJ = sum(IN_SIZES)
D_FF = 5632

LANES = 128
D_IN_PAD = -(-D_IN_PROJ // LANES) * LANES
FFN_ROWS = 1040
FF_TILE = 256
IN_ROWS = 640
IN_TILE = 1152
VMEM_LIMIT = 56 * 1024 * 1024
FFN_VMEM_LIMIT = 62 * 1024 * 1024


def _rms(x, g):
    return x * lax.rsqrt(jnp.mean(x * x, axis=-1, keepdims=True) + EPS) * g


def _ffn_kernel(*refs, split_in, split_out, n_prompt_last):
    refs = list(refs)
    x_ref = refs.pop(0)
    xs_ref = refs.pop(0) if split_in else None
    gin_ref, gout_ref, wg_ref, wu_ref, wd_ref, o_ref = refs[:6]
    ys_ref = refs[6] if split_out else None
    xn_ref = refs[-1]
    i, j = pl.program_id(0), pl.program_id(1)
    last_i = pl.num_programs(0) - 1
    npl = n_prompt_last

    @pl.when(j == 0)
    def _():
        o_ref[...] = jnp.zeros_like(o_ref)
        if split_in:
            @pl.when(i < last_i)
            def _():
                xn_ref[...] = _rms(x_ref[...], gin_ref[...]).astype(BF16)

            @pl.when(i == last_i)
            def _():
                xn_ref[0:npl, :] = _rms(x_ref[0:npl, :], gin_ref[...]).astype(BF16)
                xn_ref[npl:, :] = _rms(xs_ref[...], gin_ref[...]).astype(BF16)
        else:
            xn_ref[...] = _rms(x_ref[...], gin_ref[...]).astype(BF16)

    xn = xn_ref[...]
    g = jnp.dot(xn, wg_ref[...].astype(BF16), preferred_element_type=F32)
    u = jnp.dot(xn, wu_ref[...].astype(BF16), preferred_element_type=F32)
    h = (g * jax.nn.sigmoid(g) * u).astype(BF16)
    o_ref[...] += jnp.dot(h, wd_ref[...].astype(BF16), preferred_element_type=F32)

    @pl.when(j == pl.num_programs(1) - 1)
    def _():
        if split_in:
            @pl.when(i < last_i)
            def _():
                o_ref[...] = x_ref[...] + 0.5 * _rms(o_ref[...], gout_ref[...])

            @pl.when(i == last_i)
            def _():
                o_ref[0:npl, :] = x_ref[0:npl, :] + 0.5 * _rms(o_ref[0:npl, :], gout_ref[...])
                o_ref[npl:, :] = xs_ref[...] + 0.5 * _rms(o_ref[npl:, :], gout_ref[...])
        else:
            o_ref[...] = x_ref[...] + 0.5 * _rms(o_ref[...], gout_ref[...])
        if split_out:
            @pl.when(i == last_i)
            def _():
                ys_ref[...] = o_ref[npl:, :]


def _ffn(x, g_in, g_out, wg, wu, wd, layer, mp, ms, split_in=False, split_out=False):
    m = mp + ms
    n_tiles = m // FFN_ROWS
    n_prompt_last = mp - (n_tiles - 1) * FFN_ROWS
    assert m % FFN_ROWS == 0 and n_prompt_last + ms == FFN_ROWS and n_prompt_last % 16 == 0
    rows = pl.BlockSpec((FFN_ROWS, D_MODEL), lambda i, j: (i, 0))
    sample = pl.BlockSpec((ms, D_MODEL), lambda i, j: (0, 0))
    vec = pl.BlockSpec((1, D_MODEL), lambda i, j: (0, 0))
    xs = tuple(x) if split_in else (x,)
    return pl.pallas_call(
        functools.partial(_ffn_kernel, split_in=split_in, split_out=split_out, n_prompt_last=n_prompt_last),
        grid=(n_tiles, D_FF // FF_TILE),
        in_specs=[rows] + ([sample] if split_in else []) + [
            vec, vec,
            pl.BlockSpec((None, D_MODEL, FF_TILE), lambda i, j: (layer, 0, j)),
            pl.BlockSpec((None, D_MODEL, FF_TILE), lambda i, j: (layer, 0, j)),
            pl.BlockSpec((None, FF_TILE, D_MODEL), lambda i, j: (layer, j, 0)),
        ],
        out_specs=[rows, sample] if split_out else rows,
        out_shape=([jax.ShapeDtypeStruct((mp, D_MODEL), F32), jax.ShapeDtypeStruct((ms, D_MODEL), F32)]
                   if split_out else jax.ShapeDtypeStruct((m, D_MODEL), F32)),
        scratch_shapes=[pltpu.VMEM((FFN_ROWS, D_MODEL), BF16)],
        compiler_params=pltpu.CompilerParams(
            dimension_semantics=("parallel", "arbitrary"), vmem_limit_bytes=FFN_VMEM_LIMIT),
        name="ffn",
    )(*xs, g_in, g_out, wg, wu, wd)


def _inproj_kernel(x_ref, g_ref, w_ref, o_ref, xn_ref):
    j = pl.program_id(1)

    @pl.when(j == 0)
    def _():
        xn_ref[...] = _rms(x_ref[...], g_ref[...]).astype(BF16)

    col = lax.broadcasted_iota(jnp.int32, (1, IN_TILE), 1)
    w = jnp.where(col < D_IN_PROJ - j * IN_TILE, w_ref[...], jnp.zeros((), BF16))
    o_ref[...] = jnp.dot(xn_ref[...], w, preferred_element_type=F32)


def _inproj(x, g, w_in, layer):
    m = x.shape[0]
    return pl.pallas_call(
        _inproj_kernel,
        grid=(m // IN_ROWS, D_IN_PAD // IN_TILE),
        in_specs=[
            pl.BlockSpec((IN_ROWS, D_MODEL), lambda i, j: (i, 0)),
            pl.BlockSpec((1, D_MODEL), lambda i, j: (0, 0)),
            pl.BlockSpec((None, D_MODEL, IN_TILE), lambda i, j: (layer, 0, j)),
        ],
        out_specs=pl.BlockSpec((IN_ROWS, IN_TILE), lambda i, j: (i, j)),
        out_shape=jax.ShapeDtypeStruct((m, D_IN_PAD), F32),
        scratch_shapes=[pltpu.VMEM((IN_ROWS, D_MODEL), BF16)],
        compiler_params=pltpu.CompilerParams(
            dimension_semantics=("parallel", "arbitrary"), vmem_limit_bytes=VMEM_LIMIT),
        name="inproj",
    )(x, g, w_in)


OUT_TILE = 512


def _outproj_kernel(x_ref, php_ref, prg_ref, pss_ref, shg_ref, srg_ref, sss_ref, g_ref, w_ref, y_ref, wb_ref):
    i = pl.program_id(0)
    last = pl.num_programs(0) - 1
    r0, r1 = HG_WIDTH, HG_WIDTH + RG_WIDTH

    @pl.when(i == 0)
    def _():
        wb_ref[...] = w_ref[...].astype(BF16)

    def mixed(ohg, org, oss):
        m = jnp.dot(ohg, wb_ref[0:r0, :], preferred_element_type=F32)
        m += jnp.dot(org, wb_ref[r0:r1, :], preferred_element_type=F32)
        m += jnp.dot(oss, wb_ref[r1:, :], preferred_element_type=F32)
        return _rms(m, g_ref[...])

    @pl.when(i < last)
    def _():
        y_ref[...] = x_ref[...] + mixed(php_ref[...], prg_ref[...], pss_ref[...])

    @pl.when(i == last)
    def _():
        ns = shg_ref.shape[0]
        y_ref[0:ns, :] = x_ref[0:ns, :] + mixed(shg_ref[...], srg_ref[...], sss_ref[...])


def _outproj(x, o_prompt, o_sample, g, w_out, layer):
    mp, ms = o_prompt[0].shape[0], o_sample[0].shape[0]
    assert mp % OUT_TILE == 0 and ms <= OUT_TILE and x.shape[0] == mp + ms
    n_p = mp // OUT_TILE
    widths = (HG_WIDTH, RG_WIDTH, SSM_WIDTH)
    return pl.pallas_call(
        _outproj_kernel,
        grid=(n_p + 1,),
        in_specs=[pl.BlockSpec((OUT_TILE, D_MODEL), lambda i: (i, 0))]
        + [pl.BlockSpec((OUT_TILE, w), lambda i: (jnp.minimum(i, n_p - 1), 0)) for w in widths]
        + [pl.BlockSpec((ms, w), lambda i: (0, 0)) for w in widths]
        + [pl.BlockSpec((1, D_MODEL), lambda i: (0, 0)),
           pl.BlockSpec((None, D_MIX, D_MODEL), lambda i: (layer, 0, 0), pipeline_mode=pl.Buffered(1))],
        out_specs=pl.BlockSpec((OUT_TILE, D_MODEL), lambda i: (i, 0)),
        out_shape=jax.ShapeDtypeStruct((mp + ms, D_MODEL), F32),
        scratch_shapes=[pltpu.VMEM((D_MIX, D_MODEL), BF16)],
        compiler_params=pltpu.CompilerParams(
            dimension_semantics=("arbitrary",), vmem_limit_bytes=VMEM_LIMIT),
        name="outproj",
    )(x, *o_prompt, *o_sample, g, w_out)


_COL = np.cumsum((0,) + IN_SIZES)
HGQ_OFF, HGF_OFF, HGV_OFF, HGG_OFF, RGX_OFF, RGG_OFF, SSZ_OFF, SSX_OFF, SSDT_OFF = (int(c) for c in _COL[:9])
PIECE = 256
assert all(off % PIECE == 0 for off in (RGX_OFF, RGG_OFF, SSZ_OFF, SSX_OFF)) and SSDT_OFF % LANES == 0
assert all(w % PIECE == 0 for w in (RG_WIDTH, SSM_WIDTH, SSM_CONV_DIM))


def _pieces(off, width, rows, row_map):
    return [pl.BlockSpec((rows, PIECE), functools.partial(lambda k, *g: (row_map(*g), k), off // PIECE + k))
            for k in range(width // PIECE)]


def _cat(refs, rows=slice(None)):
    return jnp.concatenate([r[rows, :] for r in refs], axis=1)


SUBLANES = 8
SSD_T = 256
NEG_BIG = -1e30


def _split3(x):
    hi = x.astype(BF16)
    r = x - hi.astype(F32)
    mid = r.astype(BF16)
    lo = (r - mid.astype(F32)).astype(BF16)
    return hi, mid, lo


def _cumsum_rows(tri, x):
    w = x.shape[1]
    parts = jnp.concatenate(_split3(x), axis=1)
    r = jnp.dot(tri, parts, preferred_element_type=F32)
    return r[:, :w] + r[:, w:2 * w] + r[:, 2 * w:]


def _silu(x):
    return x * jax.nn.sigmoid(x)


def _neg_expm1(x, exp_x):
    series = -x * (1.0 + x * (1 / 2 + x * (1 / 6 + x * (1 / 24 + x * (1 / 120)))))
    return jnp.where(x > -1 / 16, series, 1.0 - exp_x)


def _softplus(x):
    return jnp.maximum(x, 0.0) + jnp.log(1.0 + jnp.exp(-jnp.abs(x)))


def _conv4(xp_ref, tail_ref, x, w_ref, b_ref, first):
    t = x.shape[0]

    @pl.when(first)
    def _():
        tail_ref[...] = jnp.zeros_like(tail_ref)

    xp_ref[0:SUBLANES, :] = tail_ref[...]
    xp_ref[SUBLANES:, :] = x
    tail_ref[...] = x[t - SUBLANES:, :]
    y = b_ref[...] + w_ref[CONV_W - 1:CONV_W, :] * x
    for k in range(CONV_W - 1):
        off = SUBLANES - (CONV_W - 1) + k
        y = y + w_ref[k:k + 1, :] * xp_ref[off:off + t, :]
    return y


N_Z, N_XBC = SSM_WIDTH // PIECE, SSM_CONV_DIM // PIECE


def _ssd_prompt_kernel(*refs):
    z_refs, xbc_refs = refs[:N_Z], refs[N_Z:N_Z + N_XBC]
    (dt_ref, tri_ref, expand_ref, cw_ref, cb_ref, dtb_ref, alog_ref, dexp_ref, nw_ref,
     o_ref, st_ref, cst_ref, xp_ref, tail_ref, s_ref, y_ref) = refs[N_Z + N_XBC:]
    n = pl.program_id(1)
    t = SSD_T
    z = _cat(z_refs)
    xbc_raw = _cat(xbc_refs)
    dt_raw = dt_ref[...]

    @pl.when(n == 0)
    def _():
        s_ref[...] = jnp.zeros_like(s_ref)

    xbc = _silu(_conv4(xp_ref, tail_ref, xbc_raw, cw_ref, cb_ref, n == 0))
    cst_ref[...] = _cat(xbc_refs, slice(t - (CONV_W - 1), t))
    xs = xbc[:, :SSM_WIDTH]
    gs = SSM_GROUPS * SSM_DSTATE
    bm = xbc[:, SSM_WIDTH:SSM_WIDTH + gs].astype(BF16)
    cm = xbc[:, SSM_WIDTH + gs:].astype(BF16)

    dt = _softplus(dt_raw + dtb_ref[...])
    a = dt * -jnp.exp(alog_ref[...])
    cs = _cumsum_rows(tri_ref[...], a)
    cs_t = cs.T
    cs_last = cs[t - 1:t, :]
    e_last = jnp.exp(cs_last)
    row = lax.broadcasted_iota(jnp.int32, (t, t), 0)
    col = lax.broadcasted_iota(jnp.int32, (t, t), 1)
    causal = row >= col

    def per_lane(v):
        parts = jnp.concatenate(_split3(v), axis=0)
        r = jnp.dot(parts, expand_ref[...], preferred_element_type=F32)
        return r[0:t] + r[t:2 * t] + r[2 * t:]

    dt_w = per_lane(dt)
    cs_w = per_lane(cs)
    e_cs_w = jnp.exp(cs_w)
    xdt = xs * dt_w
    xdt_b = xdt.astype(BF16)
    w_b = (xdt * jnp.exp(cs_w[t - 1:t, :] - cs_w)).astype(BF16)
    pair_lane = lax.broadcasted_iota(jnp.int32, (t, LANES), 1) < SSM_HEADDIM
    gw = SSM_WIDTH // SSM_GROUPS

    for g in range(SSM_GROUPS):
        heads = range(g * HEADS_PER_GROUP, (g + 1) * HEADS_PER_GROUP)
        bg = bm[:, g * SSM_DSTATE:(g + 1) * SSM_DSTATE]
        cg = cm[:, g * SSM_DSTATE:(g + 1) * SSM_DSTATE]
        cb = lax.dot_general(cg, bg, _NT, preferred_element_type=F32)
        s_prev = s_ref[heads.start:heads.stop].reshape(gw, SSM_DSTATE)
        y_in = lax.dot_general(cg, s_prev.astype(BF16), _NT, preferred_element_type=F32)
        st = lax.dot_general(w_b[:, g * gw:(g + 1) * gw], bg, _TN, preferred_element_type=F32)
        decay = jnp.concatenate([jnp.broadcast_to(e_last[:, h:h + 1], (SSM_HEADDIM, SSM_DSTATE)) for h in heads],
                                axis=0)
        s_ref[heads.start:heads.stop] = (decay * s_prev + st).reshape(HEADS_PER_GROUP, SSM_HEADDIM, SSM_DSTATE)

        for k in range(HEADS_PER_GROUP // 2):
            lanes = slice(g * gw + k * LANES, g * gw + (k + 1) * LANES)
            res = []
            for h in (heads.start + 2 * k, heads.start + 2 * k + 1):
                seg = jnp.where(causal, cs[:, h:h + 1] - cs_t[h:h + 1, :], NEG_BIG)
                scores = (cb * jnp.exp(seg)).astype(BF16)
                res.append(jnp.dot(scores, xdt_b[:, lanes], preferred_element_type=F32))
            y_ref[:, lanes] = (jnp.where(pair_lane, res[0], res[1])
                               + e_cs_w[:, lanes] * y_in[:, k * LANES:(k + 1) * LANES])

    y = (y_ref[...] + dexp_ref[...] * xs) * _silu(z)
    gw = SSM_WIDTH // SSM_GROUPS
    for g in range(SSM_GROUPS):
        yg = y[:, g * gw:(g + 1) * gw]
        yg = yg * lax.rsqrt(jnp.mean(yg * yg, axis=-1, keepdims=True) + EPS)
        o_ref[:, g * gw:(g + 1) * gw] = (yg * nw_ref[:, g * gw:(g + 1) * gw]).astype(BF16)

    @pl.when(n == pl.num_programs(1) - 1)
    def _():
        st_ref[...] = s_ref[...]


def _tri(t):
    return jnp.tril(jnp.ones((t, t), F32)).astype(BF16)


def _head_to_lanes():
    return jnp.asarray(np.arange(LANES)[:, None] == (np.arange(SSM_WIDTH)[None, :] // SSM_HEADDIM), BF16)


def _ssd_prompt(proj, p, bp, lp):
    nc = lp // SSD_T
    full = lambda b, n: (0, 0)
    rows = lambda b, n: b * nc + n
    dtb = jnp.pad(p['ssm_dt_bias'].reshape(1, SSM_HEADS), ((0, 0), (0, LANES - SSM_HEADS)))
    alog = jnp.pad(p['ssm_a_log'].reshape(1, SSM_HEADS), ((0, 0), (0, LANES - SSM_HEADS)))
    dexp = jnp.repeat(p['ssm_d'], SSM_HEADDIM).reshape(1, SSM_WIDTH)
    return pl.pallas_call(
        _ssd_prompt_kernel,
        grid=(bp, nc),
        in_specs=_pieces(SSZ_OFF, SSM_WIDTH, SSD_T, rows) + _pieces(SSX_OFF, SSM_CONV_DIM, SSD_T, rows) + [
            pl.BlockSpec((SSD_T, LANES), lambda b, n: (rows(b, n), SSDT_OFF // LANES)),
            pl.BlockSpec((SSD_T, SSD_T), full),
            pl.BlockSpec((LANES, SSM_WIDTH), full),
            pl.BlockSpec((CONV_W, SSM_CONV_DIM), full),
            pl.BlockSpec((1, SSM_CONV_DIM), full),
            pl.BlockSpec((1, LANES), full),
            pl.BlockSpec((1, LANES), full),
            pl.BlockSpec((1, SSM_WIDTH), full),
            pl.BlockSpec((1, SSM_WIDTH), full),
        ],
        out_specs=[
            pl.BlockSpec((SSD_T, SSM_WIDTH), lambda b, n: (b * nc + n, 0)),
            pl.BlockSpec((None, SSM_HEADS, SSM_HEADDIM, SSM_DSTATE), lambda b, n: (b, 0, 0, 0)),
            pl.BlockSpec((None, CONV_W - 1, SSM_CONV_DIM), lambda b, n: (b, 0, 0)),
        ],
        out_shape=[
            jax.ShapeDtypeStruct((bp * lp, SSM_WIDTH), BF16),
            jax.ShapeDtypeStruct((bp, SSM_HEADS, SSM_HEADDIM, SSM_DSTATE), F32),
            jax.ShapeDtypeStruct((bp, CONV_W - 1, SSM_CONV_DIM), F32),
        ],
        scratch_shapes=[
            pltpu.VMEM((SSD_T + SUBLANES, SSM_CONV_DIM), F32),
            pltpu.VMEM((SUBLANES, SSM_CONV_DIM), F32),
            pltpu.VMEM((SSM_HEADS, SSM_HEADDIM, SSM_DSTATE), F32),
            pltpu.VMEM((SSD_T, SSM_WIDTH), F32),
        ],
        compiler_params=pltpu.CompilerParams(
            dimension_semantics=("parallel", "arbitrary"), vmem_limit_bytes=VMEM_LIMIT),
        name="ssd_prompt",
    )(*([proj] * (N_Z + N_XBC + 1)), _tri(SSD_T), _head_to_lanes(), p['ssm_conv_w'],
      p['ssm_conv_b'].reshape(1, SSM_CONV_DIM), dtb, alog, dexp, p['ssm_norm_w'].reshape(1, SSM_WIDTH))


HG_T = 256

_NT = (((1,), (1,)), ((), ()))
_TN = (((0,), (0,)), ((), ()))


HG_COLS = 4 * HG_WIDTH
assert (HGQ_OFF, HGF_OFF, HGV_OFF, HGG_OFF) == (0, HG_WIDTH, 2 * HG_WIDTH, 3 * HG_WIDTH)


def _hg_tile(p_ref, off, head, rows=slice(None)):
    return p_ref[rows, off + head * LANES:off + (head + 1) * LANES]


def _hgrn_prompt_kernel(hg_ref, lb_ref, nw_ref, tri_ref, o_ref, st_ref, s_ref):
    n = pl.program_id(1)
    t, c = HG_T, HG_CHUNK
    nch = t // c

    @pl.when(n == 0)
    def _():
        s_ref[...] = jnp.zeros_like(s_ref)

    row = lax.broadcasted_iota(jnp.int32, (t, t), 0)
    col = lax.broadcasted_iota(jnp.int32, (t, t), 1)
    keep = (row >= col) & (row // c == col // c)
    tri = tri_ref[...]

    v, o_intra, kd, qe, decay = [], [], [], [], []
    for h in range(HG_HEADS):
        sl = slice(h * LANES, (h + 1) * LANES)
        lb = lb_ref[:, sl]
        fz = _hg_tile(hg_ref, HGF_OFF, h)
        logf = jnp.log(lb + (1.0 - lb) * jax.nn.sigmoid(fz))
        kk3 = ((1.0 - lb) * jax.nn.sigmoid(-fz)).reshape(nch, c, HG_DK)
        qh3 = _silu(_hg_tile(hg_ref, HGQ_OFF, h)).reshape(nch, c, HG_DK)
        vh = _hg_tile(hg_ref, HGV_OFF, h).astype(BF16)
        b3 = _cumsum_rows(tri, logf).reshape(nch, c, HG_DK)
        b_mid = b3[:, c // 2:c // 2 + 1, :]
        b_last = b3[:, c - 1:c, :]
        q_in = (qh3 * jnp.exp(b3 - b_mid)).reshape(t, HG_DK).astype(BF16)
        k_in = (kk3 * jnp.exp(b_mid - b3)).reshape(t, HG_DK).astype(BF16)
        a = jnp.where(keep, lax.dot_general(q_in, k_in, _NT, preferred_element_type=F32), 0.0)
        v.append(vh)
        o_intra.append(jnp.dot(a.astype(BF16), vh, preferred_element_type=F32))
        kd.append((kk3 * jnp.exp(b_last - b3)).astype(BF16))
        qe.append((qh3 * jnp.exp(b3)).astype(BF16))
        decay.append(jnp.exp(b_last))

    outs = [[] for _ in range(HG_HEADS)]
    for ci in range(nch):
        rows = slice(ci * c, (ci + 1) * c)
        for h in range(HG_HEADS):
            s_t = s_ref[h]
            o_inter = lax.dot_general(qe[h][ci], s_t.astype(BF16), _NT, preferred_element_type=F32)
            outs[h].append(o_intra[h][rows, :] + o_inter)
            ds_t = lax.dot_general(v[h][rows, :], kd[h][ci], _TN, preferred_element_type=F32)
            s_ref[h] = s_t * decay[h][ci] + ds_t

    for h in range(HG_HEADS):
        sl = slice(h * LANES, (h + 1) * LANES)
        o = jnp.concatenate(outs[h], axis=0)
        o = o * lax.rsqrt(jnp.mean(o * o, axis=-1, keepdims=True) + EPS)
        g = _hg_tile(hg_ref, HGG_OFF, h)
        o_ref[:, sl] = (o * nw_ref[:, sl] * _silu(g)).astype(BF16)

    @pl.when(n == pl.num_programs(1) - 1)
    def _():
        for h in range(HG_HEADS):
            st_ref[h] = s_ref[h].T


def _tri_chunks(t, c):
    r = np.arange(t)
    return jnp.asarray((r[:, None] >= r[None, :]) & (r[:, None] // c == r[None, :] // c), BF16)


def _hgrn_prompt(proj, p, lb, bp, lp):
    nc = lp // HG_T
    const = lambda b, n: (0, 0)
    return pl.pallas_call(
        _hgrn_prompt_kernel,
        grid=(bp, nc),
        in_specs=[
            pl.BlockSpec((HG_T, HG_COLS), lambda b, n: (b * nc + n, 0)),
            pl.BlockSpec((1, HG_WIDTH), const),
            pl.BlockSpec((1, HG_WIDTH), const),
            pl.BlockSpec((HG_T, HG_T), const),
        ],
        out_specs=[
            pl.BlockSpec((HG_T, HG_WIDTH), lambda b, n: (b * nc + n, 0)),
            pl.BlockSpec((None, HG_HEADS, HG_DK, HG_DV), lambda b, n: (b, 0, 0, 0)),
        ],
        out_shape=[
            jax.ShapeDtypeStruct((bp * lp, HG_WIDTH), BF16),
            jax.ShapeDtypeStruct((bp, HG_HEADS, HG_DK, HG_DV), F32),
        ],
        scratch_shapes=[pltpu.VMEM((HG_HEADS, HG_DV, HG_DK), F32)],
        compiler_params=pltpu.CompilerParams(
            dimension_semantics=("parallel", "arbitrary"), vmem_limit_bytes=VMEM_LIMIT),
        name="hgrn_prompt",
    )(proj, lb.reshape(1, HG_WIDTH), p['hg_norm_w'].reshape(1, HG_WIDTH),
      _tri_chunks(HG_T, HG_CHUNK))


RG_T = 256


def _scan_rows(a, u, h_in):
    t, w = a.shape
    rows = lax.broadcasted_iota(jnp.int32, a.shape, 0) % SUBLANES
    s = 1
    while s < SUBLANES:
        keep = rows >= s
        a_sh = jnp.where(keep, pltpu.roll(a, s, axis=0), 1.0)
        u_sh = jnp.where(keep, pltpu.roll(u, s, axis=0), 0.0)
        u = u + a * u_sh
        a = a * a_sh
        s *= 2
    hs = []
    for g in range(t // SUBLANES):
        sl = slice(g * SUBLANES, (g + 1) * SUBLANES)
        h = u[sl, :] + a[sl, :] * h_in
        hs.append(h)
        h_in = h[SUBLANES - 1:, :]
    return jnp.concatenate(hs, axis=0)


def _rg_gates(xc, wa_ref, ba_ref, wx_ref, bx_ref, lam_ref):
    xb = xc.astype(BF16)
    ra, ri = [], []
    for h in range(RG_HEADS):
        sl = slice(h * RG_HEAD_DIM, (h + 1) * RG_HEAD_DIM)
        ra.append(jnp.dot(xb[:, sl], wa_ref[h], preferred_element_type=F32))
        ri.append(jnp.dot(xb[:, sl], wx_ref[h], preferred_element_type=F32))
    r = jax.nn.sigmoid(jnp.concatenate(ra, axis=1) + ba_ref[...])
    ig = jax.nn.sigmoid(jnp.concatenate(ri, axis=1) + bx_ref[...])
    log_a = -RG_C * r * _softplus(-lam_ref[...])
    a = jnp.exp(log_a)
    mult = jnp.sqrt(_neg_expm1(2.0 * log_a, a * a))
    return a, mult, ig


def _gelu_tanh(x):
    return 0.5 * x * (1.0 + jnp.tanh(math.sqrt(2.0 / math.pi) * (x + 0.044715 * (x * x * x))))


N_RG = RG_WIDTH // PIECE


def _rglru_prompt_kernel(*refs):
    x_refs, gate_refs = refs[:N_RG], refs[N_RG:2 * N_RG]
    (cw_ref, cb_ref, wa_ref, ba_ref, wx_ref, bx_ref, lam_ref,
     o_ref, h_ref, cst_ref, xp_ref, tail_ref, hprev_ref) = refs[2 * N_RG:]
    n = pl.program_id(1)
    t = RG_T

    @pl.when(n == 0)
    def _():
        hprev_ref[...] = jnp.zeros_like(hprev_ref)

    x = _cat(x_refs)
    xc = _conv4(xp_ref, tail_ref, x, cw_ref, cb_ref, n == 0)
    cst_ref[...] = _cat(x_refs, slice(t - (CONV_W - 1), t))
    a, mult, ig = _rg_gates(xc, wa_ref, ba_ref, wx_ref, bx_ref, lam_ref)
    rows = lax.broadcasted_iota(jnp.int32, a.shape, 0)
    mult = jnp.where((rows == 0) & (n == 0), 1.0, mult)
    h = _scan_rows(a, mult * ig * xc, hprev_ref[...])
    hprev_ref[...] = h[t - 1:t, :]
    h_ref[...] = h[t - 1:t, :]
    o_ref[...] = (h * _gelu_tanh(_cat(gate_refs))).astype(BF16)


def _rglru_prompt(proj, p, bp, lp):
    nc = lp // RG_T
    full2 = lambda b, n: (0, 0)
    full3 = lambda b, n: (0, 0, 0)
    rows = lambda b, n: b * nc + n
    row = lambda a: a.reshape(1, RG_WIDTH)
    return pl.pallas_call(
        _rglru_prompt_kernel,
        grid=(bp, nc),
        in_specs=_pieces(RGX_OFF, RG_WIDTH, RG_T, rows) + _pieces(RGG_OFF, RG_WIDTH, RG_T, rows) + [
            pl.BlockSpec((CONV_W, RG_WIDTH), full2),
            pl.BlockSpec((1, RG_WIDTH), full2),
            pl.BlockSpec((RG_HEADS, RG_HEAD_DIM, RG_HEAD_DIM), full3),
            pl.BlockSpec((1, RG_WIDTH), full2),
            pl.BlockSpec((RG_HEADS, RG_HEAD_DIM, RG_HEAD_DIM), full3),
            pl.BlockSpec((1, RG_WIDTH), full2),
            pl.BlockSpec((1, RG_WIDTH), full2),
        ],
        out_specs=[
            pl.BlockSpec((RG_T, RG_WIDTH), lambda b, n: (b * nc + n, 0)),
            pl.BlockSpec((None, 1, RG_WIDTH), lambda b, n: (b, 0, 0)),
            pl.BlockSpec((None, CONV_W - 1, RG_WIDTH), lambda b, n: (b, 0, 0)),
        ],
        out_shape=[
            jax.ShapeDtypeStruct((bp * lp, RG_WIDTH), BF16),
            jax.ShapeDtypeStruct((bp, 1, RG_WIDTH), F32),
            jax.ShapeDtypeStruct((bp, CONV_W - 1, RG_WIDTH), F32),
        ],
        scratch_shapes=[
            pltpu.VMEM((RG_T + SUBLANES, RG_WIDTH), F32),
            pltpu.VMEM((SUBLANES, RG_WIDTH), F32),
            pltpu.VMEM((1, RG_WIDTH), F32),
        ],
        compiler_params=pltpu.CompilerParams(
            dimension_semantics=("parallel", "arbitrary"), vmem_limit_bytes=VMEM_LIMIT),
        name="rglru_prompt",
    )(*([proj] * (2 * N_RG)), p['rg_conv_w'], row(p['rg_conv_b']), p['rg_wa'].astype(BF16), row(p['rg_ba']),
      p['rg_wx'].astype(BF16), row(p['rg_bx']), row(p['rg_lambda']))


SB = 8


def _col_bcast(row):
    return jnp.broadcast_to(row, (LANES, LANES)).T


def _sample_kernel(p_ref,
                   hs_ref, rh_ref, rc_ref, ss_ref, sc_ref,
                   lb_ref, hnw_ref,
                   rcw_ref, rcb_ref, wa_ref, ba_ref, wx_ref, bx_ref, lam_ref,
                   scw_ref, scb_ref, dtb_ref, aexp_ref, dexp_ref, snw_ref, expand_ref,
                   ohg_ref, org_ref, ossm_ref, hs_out, rh_out, rc_out, ss_out, sc_out,
                   f_sc, kk_sc, q_sc, o_sc, adec_sc, xdt_sc, b_sc, c_sc, y_sc):
    x = p_ref[:, RGX_OFF:RGX_OFF + RG_WIDTH]
    xc = rcb_ref[...] + rcw_ref[CONV_W - 1:CONV_W, :] * x
    for k in range(CONV_W - 1):
        xc = xc + rcw_ref[k:k + 1, :] * rc_ref[k]
        rc_out[k] = x if k == CONV_W - 2 else rc_ref[k + 1]
    a, mult, ig = _rg_gates(xc, wa_ref, ba_ref, wx_ref, bx_ref, lam_ref)
    h = a * rh_ref[...] + mult * ig * xc
    rh_out[...] = h
    org_ref[...] = (h * _gelu_tanh(p_ref[:, RGG_OFF:RGG_OFF + RG_WIDTH])).astype(BF16)

    z = p_ref[:, SSZ_OFF:SSZ_OFF + SSM_WIDTH]
    xbc_raw = p_ref[:, SSX_OFF:SSX_OFF + SSM_CONV_DIM]
    dt_raw = p_ref[:, SSDT_OFF:SSDT_OFF + LANES]
    xbc = scb_ref[...] + scw_ref[CONV_W - 1:CONV_W, :] * xbc_raw
    for k in range(CONV_W - 1):
        xbc = xbc + scw_ref[k:k + 1, :] * sc_ref[k]
        sc_out[k] = xbc_raw if k == CONV_W - 2 else sc_ref[k + 1]
    xbc = _silu(xbc)
    xs = xbc[:, :SSM_WIDTH]
    gs = SSM_GROUPS * SSM_DSTATE
    b_sc[...] = xbc[:, SSM_WIDTH:SSM_WIDTH + gs]
    c_sc[...] = xbc[:, SSM_WIDTH + gs:]
    dt = _softplus(dt_raw + dtb_ref[...])
    parts = jnp.concatenate(_split3(dt), axis=0)
    r = jnp.dot(parts, expand_ref[...], preferred_element_type=F32)
    dt_exp = r[0:SB] + r[SB:2 * SB] + r[2 * SB:]
    xdt_sc[...] = xs * dt_exp
    adec_sc[...] = jnp.exp(dt_exp * aexp_ref[...])

    hg_tile = functools.partial(_hg_tile, p_ref)
    for hd in range(HG_HEADS):
        sl = slice(hd * LANES, (hd + 1) * LANES)
        lb = lb_ref[:, sl]
        fz = hg_tile(HGF_OFF, hd)
        f_sc[:, sl] = lb + (1.0 - lb) * jax.nn.sigmoid(fz)
        kk_sc[:, sl] = (1.0 - lb) * jax.nn.sigmoid(-fz)
        q_sc[:, sl] = _silu(hg_tile(HGQ_OFF, hd))

    for j in range(SB):
        row = slice(j, j + 1)
        for hd in range(HG_HEADS):
            sl = slice(hd * LANES, (hd + 1) * LANES)
            v_row = hg_tile(HGV_OFF, hd, row)
            s_new = _col_bcast(f_sc[row, sl]) * hs_ref[j, hd] + _col_bcast(kk_sc[row, sl]) * v_row
            hs_out[j, hd] = s_new
            q8 = jnp.broadcast_to(q_sc[row, sl], (SB, LANES)).astype(BF16)
            o_sc[row, sl] = jnp.dot(q8, s_new.astype(BF16), preferred_element_type=F32)[0:1]
        for hp in range(SSM_HEADS // 2):
            sl = slice(hp * LANES, (hp + 1) * LANES)
            g = (2 * hp) // HEADS_PER_GROUP
            gsl = slice(g * SSM_DSTATE, (g + 1) * SSM_DSTATE)
            s_old = ss_ref[j, 2 * hp:2 * hp + 2].reshape(LANES, SSM_DSTATE)
            s_new = _col_bcast(adec_sc[row, sl]) * s_old + _col_bcast(xdt_sc[row, sl]) * b_sc[row, gsl]
            ss_out[j, 2 * hp:2 * hp + 2] = s_new.reshape(2, SSM_HEADDIM, SSM_DSTATE)
            c8 = jnp.broadcast_to(c_sc[row, gsl], (SB, SSM_DSTATE)).astype(BF16)
            y_sc[row, sl] = lax.dot_general(c8, s_new.astype(BF16), _NT, preferred_element_type=F32)[0:1]

    for hd in range(HG_HEADS):
        sl = slice(hd * LANES, (hd + 1) * LANES)
        o = o_sc[:, sl]
        o = o * lax.rsqrt(jnp.mean(o * o, axis=-1, keepdims=True) + EPS)
        ohg_ref[:, sl] = (o * hnw_ref[:, sl] * _silu(hg_tile(HGG_OFF, hd))).astype(BF16)
    y = (y_sc[...] + dexp_ref[...] * xs) * _silu(z)
    gw = SSM_WIDTH // SSM_GROUPS
    for g in range(SSM_GROUPS):
        yg = y[:, g * gw:(g + 1) * gw]
        yg = yg * lax.rsqrt(jnp.mean(yg * yg, axis=-1, keepdims=True) + EPS)
        ossm_ref[:, g * gw:(g + 1) * gw] = (yg * snw_ref[:, g * gw:(g + 1) * gw]).astype(BF16)


def _sample_mix(proj, states, layer, p, lb, mp):
    hg_s, rg_h, rg_c_t, ssm_s, ssm_c_t = states
    bs = hg_s.shape[1]
    rb = mp // SB
    const2 = lambda i: (0, 0)
    const3 = lambda i: (0, 0, 0)
    row = lambda a, w: a.reshape(1, w)
    dtb = jnp.pad(p['ssm_dt_bias'].reshape(1, SSM_HEADS), ((0, 0), (0, LANES - SSM_HEADS)))
    aexp = jnp.repeat(-jnp.exp(p['ssm_a_log']), SSM_HEADDIM).reshape(1, SSM_WIDTH)
    dexp = jnp.repeat(p['ssm_d'], SSM_HEADDIM).reshape(1, SSM_WIDTH)
    expand = _head_to_lanes()
    in_specs = [
        pl.BlockSpec((SB, D_IN_PAD), lambda i: (rb + i, 0)),
        pl.BlockSpec((None, SB, HG_HEADS, HG_DK, HG_DV), lambda i: (layer, i, 0, 0, 0)),
        pl.BlockSpec((None, SB, RG_WIDTH), lambda i: (layer, i, 0)),
        pl.BlockSpec((None, CONV_W - 1, SB, RG_WIDTH), lambda i: (layer, 0, i, 0)),
        pl.BlockSpec((None, SB, SSM_HEADS, SSM_HEADDIM, SSM_DSTATE), lambda i: (layer, i, 0, 0, 0)),
        pl.BlockSpec((None, CONV_W - 1, SB, SSM_CONV_DIM), lambda i: (layer, 0, i, 0)),
        pl.BlockSpec((1, HG_WIDTH), const2),
        pl.BlockSpec((1, HG_WIDTH), const2),
        pl.BlockSpec((CONV_W, RG_WIDTH), const2),
        pl.BlockSpec((1, RG_WIDTH), const2),
        pl.BlockSpec((RG_HEADS, RG_HEAD_DIM, RG_HEAD_DIM), const3),
        pl.BlockSpec((1, RG_WIDTH), const2),
        pl.BlockSpec((RG_HEADS, RG_HEAD_DIM, RG_HEAD_DIM), const3),
        pl.BlockSpec((1, RG_WIDTH), const2),
        pl.BlockSpec((1, RG_WIDTH), const2),
        pl.BlockSpec((CONV_W, SSM_CONV_DIM), const2),
        pl.BlockSpec((1, SSM_CONV_DIM), const2),
        pl.BlockSpec((1, LANES), const2),
        pl.BlockSpec((1, SSM_WIDTH), const2),
        pl.BlockSpec((1, SSM_WIDTH), const2),
        pl.BlockSpec((1, SSM_WIDTH), const2),
        pl.BlockSpec((LANES, SSM_WIDTH), const2),
    ]
    out_specs = [
        pl.BlockSpec((SB, HG_WIDTH), lambda i: (i, 0)),
        pl.BlockSpec((SB, RG_WIDTH), lambda i: (i, 0)),
        pl.BlockSpec((SB, SSM_WIDTH), lambda i: (i, 0)),
        pl.BlockSpec((SB, HG_HEADS, HG_DK, HG_DV), lambda i: (i, 0, 0, 0)),
        pl.BlockSpec((SB, RG_WIDTH), lambda i: (i, 0)),
        pl.BlockSpec((CONV_W - 1, SB, RG_WIDTH), lambda i: (0, i, 0)),
        pl.BlockSpec((SB, SSM_HEADS, SSM_HEADDIM, SSM_DSTATE), lambda i: (i, 0, 0, 0)),
        pl.BlockSpec((CONV_W - 1, SB, SSM_CONV_DIM), lambda i: (0, i, 0)),
    ]
    out_shape = [
        jax.ShapeDtypeStruct((bs, HG_WIDTH), BF16),
        jax.ShapeDtypeStruct((bs, RG_WIDTH), BF16),
        jax.ShapeDtypeStruct((bs, SSM_WIDTH), BF16),
        jax.ShapeDtypeStruct(hg_s.shape[1:], F32),
        jax.ShapeDtypeStruct(rg_h.shape[1:], F32),
        jax.ShapeDtypeStruct(rg_c_t.shape[1:], F32),
        jax.ShapeDtypeStruct(ssm_s.shape[1:], F32),
        jax.ShapeDtypeStruct(ssm_c_t.shape[1:], F32),
    ]
    outs = pl.pallas_call(
        _sample_kernel,
        grid=(bs // SB,),
        in_specs=in_specs,
        out_specs=out_specs,
        out_shape=out_shape,
        scratch_shapes=[
            pltpu.VMEM((SB, HG_WIDTH), F32), pltpu.VMEM((SB, HG_WIDTH), F32), pltpu.VMEM((SB, HG_WIDTH), F32),
            pltpu.VMEM((SB, HG_WIDTH), F32),
            pltpu.VMEM((SB, SSM_WIDTH), F32), pltpu.VMEM((SB, SSM_WIDTH), F32),
            pltpu.VMEM((SB, SSM_GROUPS * SSM_DSTATE), F32), pltpu.VMEM((SB, SSM_GROUPS * SSM_DSTATE), F32),
            pltpu.VMEM((SB, SSM_WIDTH), F32),
        ],
        compiler_params=pltpu.CompilerParams(
            dimension_semantics=("parallel",), vmem_limit_bytes=VMEM_LIMIT),
        name="sample_mix",
    )(proj, hg_s, rg_h, rg_c_t, ssm_s, ssm_c_t,
      lb.reshape(1, HG_WIDTH), row(p['hg_norm_w'], HG_WIDTH),
      p['rg_conv_w'], row(p['rg_conv_b'], RG_WIDTH), p['rg_wa'].astype(BF16), row(p['rg_ba'], RG_WIDTH),
      p['rg_wx'].astype(BF16), row(p['rg_bx'], RG_WIDTH), row(p['rg_lambda'], RG_WIDTH),
      p['ssm_conv_w'], row(p['ssm_conv_b'], SSM_CONV_DIM), dtb, aexp, dexp, row(p['ssm_norm_w'], SSM_WIDTH),
      expand)
    return tuple(outs[:3]), tuple(outs[3:])


def kernel(x_prompt, x_sample, state_hgrn, state_rglru, state_rglru_conv, state_ssm, state_ssm_conv, norm_g, ffn1_w_gate, ffn1_w_up, ffn1_w_down, ffn2_w_gate, ffn2_w_up, ffn2_w_down, w_in, w_out, hg_lb_logits, hg_norm_w, rg_conv_w, rg_conv_b, rg_wa, rg_ba, rg_wx, rg_bx, rg_lambda, ssm_conv_w, ssm_conv_b, ssm_dt_bias, ssm_a_log, ssm_d, ssm_norm_w):
    bp, lp, _ = x_prompt.shape
    bs, ls, _ = x_sample.shape
    mp = bp * lp
    ms = bs * ls
    assert (mp + ms) % IN_ROWS == 0 and D_FF % FF_TILE == 0 and D_IN_PAD % IN_TILE == 0

    lw = {
        'hg_norm_w': hg_norm_w, 'rg_conv_w': rg_conv_w, 'rg_conv_b': rg_conv_b, 'rg_wa': rg_wa,
        'rg_ba': rg_ba, 'rg_wx': rg_wx, 'rg_bx': rg_bx, 'rg_lambda': rg_lambda,
        'ssm_conv_w': ssm_conv_w, 'ssm_conv_b': ssm_conv_b, 'ssm_dt_bias': ssm_dt_bias,
        'ssm_a_log': ssm_a_log, 'ssm_d': ssm_d, 'ssm_norm_w': ssm_norm_w,
    }
    lb_cum = jnp.cumsum(jax.nn.softmax(hg_lb_logits.astype(F32), axis=0), axis=0)
    lower_bounds = lb_cum - lb_cum[:1]

    taps_first = lambda a: jnp.transpose(a, (0, 2, 1, 3))
    sample_init = (state_hgrn, state_rglru, taps_first(state_rglru_conv), state_ssm, taps_first(state_ssm_conv))

    w_in_b = w_in.astype(BF16)
    x = (x_prompt.reshape(mp, D_MODEL), x_sample.reshape(ms, D_MODEL))
    new_p = ([], [], [], [], [])
    new_s = ([], [], [], [], [])
    for l in range(DEPTH):
        g = norm_g[l].reshape(6, 1, D_MODEL)
        p = {name: arr[l] for name, arr in lw.items()}
        x = _ffn(x, g[0], g[1], ffn1_w_gate, ffn1_w_up, ffn1_w_down, l, mp, ms, split_in=(l == 0))
        proj = _inproj(x, g[2], w_in_b, l)
        o_hg, hg_new = _hgrn_prompt(proj, p, lower_bounds[l], bp, lp)
        o_rg, rg_h_new, rg_c_new = _rglru_prompt(proj, p, bp, lp)
        o_ssm, ssm_new, ssm_c_new = _ssd_prompt(proj, p, bp, lp)
        st_p = (hg_new, rg_h_new.reshape(bp, RG_WIDTH), rg_c_new, ssm_new, ssm_c_new)
        o_s, st_s = _sample_mix(proj, sample_init, l, p, lower_bounds[l], mp)
        x = _outproj(x, (o_hg, o_rg, o_ssm), o_s, g[3], w_out, l)
        x = _ffn(x, g[4], g[5], ffn2_w_gate, ffn2_w_up, ffn2_w_down, l, mp, ms, split_out=(l == DEPTH - 1))
        for acc, s in zip(new_p, st_p):
            acc.append(s)
        for acc, s in zip(new_s, st_s):
            acc.append(s)
    hg_p, rg_p, rgc_p, ssm_p, ssmc_p = (jnp.stack(a) for a in new_p)
    hg_s, rg_s, rgc_s, ssm_s, ssmc_s = (jnp.stack(a) for a in new_s)
    rgc_s, ssmc_s = taps_first(rgc_s), taps_first(ssmc_s)
    y_prompt = x[0].reshape(bp, lp, D_MODEL)
    y_sample = x[1].reshape(bs, ls, D_MODEL)
    return (y_prompt, y_sample, hg_p, hg_s, rg_p, rg_s, rgc_p, rgc_s, ssm_p, ssm_s, ssmc_p, ssmc_s)
```

```python
import functools
import math

import jax
import jax.numpy as jnp
import numpy as np
from jax import lax
from jax.experimental import pallas as pl
from jax.experimental.pallas import tpu as pltpu

F32 = jnp.float32
BF16 = jnp.bfloat16

D_MODEL = 2048
DEPTH = 2
EPS = 1e-6
CONV_W = 4
HG_HEADS = 4
HG_DK = 128
HG_DV = 128
HG_WIDTH = HG_HEADS * HG_DV
HG_CHUNK = 32
RG_HEADS = 6
RG_HEAD_DIM = 128
RG_WIDTH = RG_HEADS * RG_HEAD_DIM
RG_C = 8.0
SSM_HEADS = 12
SSM_HEADDIM = 64
SSM_WIDTH = SSM_HEADS * SSM_HEADDIM
SSM_GROUPS = 2
HEADS_PER_GROUP = SSM_HEADS // SSM_GROUPS
SSM_DSTATE = 128
SSM_CHUNK = 64
SSM_CONV_DIM = SSM_WIDTH + 2 * SSM_GROUPS * SSM_DSTATE
D_MIX = HG_WIDTH + RG_WIDTH + SSM_WIDTH
IN_SIZES = (HG_HEADS * HG_DK, HG_HEADS * HG_DK, HG_WIDTH, HG_WIDTH, RG_WIDTH, RG_WIDTH,
            SSM_WIDTH, SSM_CONV_DIM, SSM_HEADS)
D_IN_PROJ = sum(IN_SIZES)
D_FF = 5632

LANES = 128
D_IN_PAD = -(-D_IN_PROJ // LANES) * LANES
FFN_ROWS = 1040
FF_TILE = 256
IN_ROWS = 832
IN_TILE = 1152
VMEM_LIMIT = 56 * 1024 * 1024
FFN_VMEM_LIMIT = 62 * 1024 * 1024


def _rms(x, g):
    return x * lax.rsqrt(jnp.mean(x * x, axis=-1, keepdims=True) + EPS) * g


def _ffn_kernel(*refs, split_in, split_out, n_prompt_last, n_stack):
    refs = list(refs)
    x_ref = refs.pop(0)
    xs_ref = refs.pop(0) if split_in else None
    gin_ref, gout_ref, wg_ref, wu_ref, wd_ref = (refs.pop(0) for _ in range(5))
    stack_src = [[refs.pop(0) for _ in range(DEPTH)] for _ in range(n_stack)]
    o_ref = refs.pop(0)
    ys_ref = refs.pop(0) if split_out else None
    stack_dst = [refs.pop(0) for _ in range(n_stack)]
    xn_ref = refs.pop(0)
    sem = refs.pop(0) if n_stack else None
    i, j = pl.program_id(0), pl.program_id(1)
    last_i = pl.num_programs(0) - 1
    last_j = pl.num_programs(1) - 1
    npl = n_prompt_last

    def stack_copies():
        return [pltpu.make_async_copy(stack_src[a][l], stack_dst[a].at[l], sem.at[a * DEPTH + l])
                for a in range(n_stack) for l in range(DEPTH)]

    if n_stack:
        @pl.when((i == 0) & (j == 0))
        def _():
            for c in stack_copies():
                c.start()

    @pl.when(j == 0)
    def _():
        o_ref[...] = jnp.zeros_like(o_ref)
        if split_in:
            @pl.when(i < last_i)
            def _():
                xn_ref[...] = _rms(x_ref[...], gin_ref[...]).astype(BF16)

            @pl.when(i == last_i)
            def _():
                xn_ref[0:npl, :] = _rms(x_ref[0:npl, :], gin_ref[...]).astype(BF16)
                xn_ref[npl:, :] = _rms(xs_ref[...], gin_ref[...]).astype(BF16)
        else:
            xn_ref[...] = _rms(x_ref[...], gin_ref[...]).astype(BF16)

    xn = xn_ref[...]
    g = jnp.dot(xn, wg_ref[...].astype(BF16), preferred_element_type=F32)
    u = jnp.dot(xn, wu_ref[...].astype(BF16), preferred_element_type=F32)
    h = (g * jax.nn.sigmoid(g) * u).astype(BF16)
    o_ref[...] += jnp.dot(h, wd_ref[...].astype(BF16), preferred_element_type=F32)

    @pl.when(j == pl.num_programs(1) - 1)
    def _():
        if split_in:
            @pl.when(i < last_i)
            def _():
                o_ref[...] = x_ref[...] + 0.5 * _rms(o_ref[...], gout_ref[...])

            @pl.when(i == last_i)
            def _():
                o_ref[0:npl, :] = x_ref[0:npl, :] + 0.5 * _rms(o_ref[0:npl, :], gout_ref[...])
                o_ref[npl:, :] = xs_ref[...] + 0.5 * _rms(o_ref[npl:, :], gout_ref[...])
        else:
            o_ref[...] = x_ref[...] + 0.5 * _rms(o_ref[...], gout_ref[...])
        if split_out:
            @pl.when(i == last_i)
            def _():
                ys_ref[...] = o_ref[npl:, :]

    if n_stack:
        @pl.when((i == last_i) & (j == last_j))
        def _():
            for c in stack_copies():
                c.wait()


def _ffn(x, g_in, g_out, wg, wu, wd, layer, mp, ms, split_in=False, split_out=False, stack=()):
    m = mp + ms
    n_tiles = m // FFN_ROWS
    n_prompt_last = mp - (n_tiles - 1) * FFN_ROWS
    assert m % FFN_ROWS == 0 and n_prompt_last + ms == FFN_ROWS and n_prompt_last % 16 == 0
    rows = pl.BlockSpec((FFN_ROWS, D_MODEL), lambda i, j: (i, 0))
    sample = pl.BlockSpec((ms, D_MODEL), lambda i, j: (0, 0))
    vec = pl.BlockSpec((1, D_MODEL), lambda i, j: (0, 0))
    xs = tuple(x) if split_in else (x,)
    hbm = pl.BlockSpec(memory_space=pl.ANY)
    assert all(len(group) == DEPTH for group in stack)
    if split_out:
        out_specs = [rows, sample]
        out_shape = [jax.ShapeDtypeStruct((mp, D_MODEL), F32), jax.ShapeDtypeStruct((ms, D_MODEL), F32)]
    else:
        out_specs = [rows]
        out_shape = [jax.ShapeDtypeStruct((m, D_MODEL), F32)]
    out_specs += [hbm] * len(stack)
    out_shape += [jax.ShapeDtypeStruct((DEPTH,) + group[0].shape, group[0].dtype) for group in stack]
    outs = pl.pallas_call(
        functools.partial(_ffn_kernel, split_in=split_in, split_out=split_out, n_prompt_last=n_prompt_last,
                          n_stack=len(stack)),
        grid=(n_tiles, D_FF // FF_TILE),
        in_specs=[rows] + ([sample] if split_in else []) + [
            vec, vec,
            pl.BlockSpec((None, D_MODEL, FF_TILE), lambda i, j: (layer, 0, j)),
            pl.BlockSpec((None, D_MODEL, FF_TILE), lambda i, j: (layer, 0, j)),
            pl.BlockSpec((None, FF_TILE, D_MODEL), lambda i, j: (layer, j, 0)),
        ] + [hbm] * (DEPTH * len(stack)),
        out_specs=out_specs,
        out_shape=out_shape,
        scratch_shapes=[pltpu.VMEM((FFN_ROWS, D_MODEL), BF16)]
        + ([pltpu.SemaphoreType.DMA((DEPTH * len(stack),))] if stack else []),
        compiler_params=pltpu.CompilerParams(
            dimension_semantics=("arbitrary", "arbitrary"), vmem_limit_bytes=FFN_VMEM_LIMIT),
        name="ffn",
    )(*xs, g_in, g_out, wg, wu, wd, *(a for group in stack for a in group))
    return outs[0] if len(outs) == 1 else tuple(outs)


def _inproj_kernel(x_ref, g_ref, w_ref, o_ref, xn_ref):
    j, i = pl.program_id(0), pl.program_id(1)

    @pl.when(j == 0)
    def _():
        xn_ref[i] = _rms(x_ref[...], g_ref[...]).astype(BF16)

    o_ref[...] = jnp.dot(xn_ref[i], w_ref[...], preferred_element_type=F32)


def _inproj(x, g, w_in, layer):
    m = x.shape[0]
    n_i = m // IN_ROWS
    return pl.pallas_call(
        _inproj_kernel,
        grid=(D_IN_PAD // IN_TILE, n_i),
        in_specs=[
            pl.BlockSpec((IN_ROWS, D_MODEL), lambda j, i: (jnp.where(j == 0, i, n_i - 1), 0),
                         pipeline_mode=pl.Buffered(1)),
            pl.BlockSpec((1, D_MODEL), lambda j, i: (0, 0)),
            pl.BlockSpec((None, D_MODEL, IN_TILE), lambda j, i: (layer, 0, j), pipeline_mode=pl.Buffered(1)),
        ],
        out_specs=pl.BlockSpec((IN_ROWS, IN_TILE), lambda j, i: (i, j)),
        out_shape=jax.ShapeDtypeStruct((m, D_IN_PAD), F32),
        scratch_shapes=[pltpu.VMEM((n_i, IN_ROWS, D_MODEL), BF16)],
        compiler_params=pltpu.CompilerParams(
            dimension_semantics=("arbitrary", "arbitrary"), vmem_limit_bytes=VMEM_LIMIT),
        name="inproj",
    )(x, g, w_in)


OUT_TILE = 512


def _outproj_kernel(x_ref, php_ref, prg_ref, pss_ref, shg_ref, srg_ref, sss_ref, g_ref, w_ref, y_ref, wb_ref):
    i = pl.program_id(0)
    last = pl.num_programs(0) - 1
    r0, r1 = HG_WIDTH, HG_WIDTH + RG_WIDTH

    @pl.when(i == 0)
    def _():
        wb_ref[...] = w_ref[...].astype(BF16)

    def mixed(ohg, org, oss):
        m = jnp.dot(ohg, wb_ref[0:r0, :], preferred_element_type=F32)
        m += jnp.dot(org, wb_ref[r0:r1, :], preferred_element_type=F32)
        m += jnp.dot(oss, wb_ref[r1:, :], preferred_element_type=F32)
        return _rms(m, g_ref[...])

    @pl.when(i < last)
    def _():
        y_ref[...] = x_ref[...] + mixed(php_ref[...], prg_ref[...], pss_ref[...])

    @pl.when(i == last)
    def _():
        ns = shg_ref.shape[0]
        y_ref[0:ns, :] = x_ref[0:ns, :] + mixed(shg_ref[...], srg_ref[...], sss_ref[...])


def _outproj(x, o_prompt, o_sample, g, w_out, layer):
    mp, ms = o_prompt[0].shape[0], o_sample[0].shape[0]
    assert mp % OUT_TILE == 0 and ms <= OUT_TILE and x.shape[0] == mp + ms
    n_p = mp // OUT_TILE
    widths = (HG_WIDTH, RG_WIDTH, SSM_WIDTH)
    return pl.pallas_call(
        _outproj_kernel,
        grid=(n_p + 1,),
        in_specs=[pl.BlockSpec((OUT_TILE, D_MODEL), lambda i: (i, 0))]
        + [pl.BlockSpec((OUT_TILE, w), lambda i: (jnp.minimum(i, n_p - 1), 0)) for w in widths]
        + [pl.BlockSpec((ms, w), lambda i: (0, 0)) for w in widths]
        + [pl.BlockSpec((1, D_MODEL), lambda i: (0, 0)),
           pl.BlockSpec((None, D_MIX, D_MODEL), lambda i: (layer, 0, 0), pipeline_mode=pl.Buffered(1))],
        out_specs=pl.BlockSpec((OUT_TILE, D_MODEL), lambda i: (i, 0)),
        out_shape=jax.ShapeDtypeStruct((mp + ms, D_MODEL), F32),
        scratch_shapes=[pltpu.VMEM((D_MIX, D_MODEL), BF16)],
        compiler_params=pltpu.CompilerParams(
            dimension_semantics=("arbitrary",), vmem_limit_bytes=VMEM_LIMIT),
        name="outproj",
    )(x, *o_prompt, *o_sample, g, w_out)


_COL = np.cumsum((0,) + IN_SIZES)
HGQ_OFF, HGF_OFF, HGV_OFF, HGG_OFF, RGX_OFF, RGG_OFF, SSZ_OFF, SSX_OFF, SSDT_OFF = (int(c) for c in _COL[:9])
PIECE = 256
assert all(off % PIECE == 0 for off in (RGX_OFF, RGG_OFF, SSZ_OFF, SSX_OFF)) and SSDT_OFF % LANES == 0
assert all(w % PIECE == 0 for w in (RG_WIDTH, SSM_WIDTH, SSM_CONV_DIM))


def _pieces(off, width, rows, row_map):
    return [pl.BlockSpec((rows, PIECE), functools.partial(lambda k, *g: (row_map(*g), k), off // PIECE + k))
            for k in range(width // PIECE)]


def _cat(refs, rows=slice(None)):
    return jnp.concatenate([r[rows, :] for r in refs], axis=1)


SUBLANES = 8
SSD_T = 256
NEG_BIG = -1e30


def _split3(x):
    hi = x.astype(BF16)
    r = x - hi.astype(F32)
    mid = r.astype(BF16)
    lo = (r - mid.astype(F32)).astype(BF16)
    return hi, mid, lo


def _cumsum_rows(tri, x):
    w = x.shape[1]
    parts = jnp.concatenate(_split3(x), axis=1)
    r = jnp.dot(tri, parts, preferred_element_type=F32)
    return r[:, :w] + r[:, w:2 * w] + r[:, 2 * w:]


def _silu(x):
    return x * jax.nn.sigmoid(x)


def _neg_expm1(x, exp_x):
    series = -x * (1.0 + x * (1 / 2 + x * (1 / 6 + x * (1 / 24 + x * (1 / 120)))))
    return jnp.where(x > -1 / 16, series, 1.0 - exp_x)


def _softplus(x):
    return jnp.maximum(x, 0.0) + jnp.log(1.0 + jnp.exp(-jnp.abs(x)))


def _conv4(xp_ref, tail_ref, x, w_ref, b_ref, first):
    t = x.shape[0]

    @pl.when(first)
    def _():
        tail_ref[...] = jnp.zeros_like(tail_ref)

    xp_ref[0:SUBLANES, :] = tail_ref[...]
    xp_ref[SUBLANES:, :] = x
    tail_ref[...] = x[t - SUBLANES:, :]
    y = b_ref[...] + w_ref[CONV_W - 1:CONV_W, :] * x
    for k in range(CONV_W - 1):
        off = SUBLANES - (CONV_W - 1) + k
        y = y + w_ref[k:k + 1, :] * xp_ref[off:off + t, :]
    return y


N_Z, N_XBC = SSM_WIDTH // PIECE, SSM_CONV_DIM // PIECE


def _ssd_prompt_kernel(*refs):
    z_refs, xbc_refs = refs[:N_Z], refs[N_Z:N_Z + N_XBC]
    (dt_ref, tri_ref, expand_ref, cw_ref, cb_ref, dtb_ref, alog_ref, dexp_ref, nw_ref,
     o_ref, st_ref, cst_ref, xp_ref, tail_ref, s_ref, y_ref) = refs[N_Z + N_XBC:]
    n = pl.program_id(1)
    t = SSD_T
    z = _cat(z_refs)
    xbc_raw = _cat(xbc_refs)
    dt_raw = dt_ref[...]

    @pl.when(n == 0)
    def _():
        s_ref[...] = jnp.zeros_like(s_ref)

    xbc = _silu(_conv4(xp_ref, tail_ref, xbc_raw, cw_ref, cb_ref, n == 0))
    cst_ref[...] = _cat(xbc_refs, slice(t - (CONV_W - 1), t))
    xs = xbc[:, :SSM_WIDTH]
    gs = SSM_GROUPS * SSM_DSTATE
    bm = xbc[:, SSM_WIDTH:SSM_WIDTH + gs].astype(BF16)
    cm = xbc[:, SSM_WIDTH + gs:].astype(BF16)

    dt = _softplus(dt_raw + dtb_ref[...])
    a = dt * -jnp.exp(alog_ref[...])
    cs = _cumsum_rows(tri_ref[...], a)
    cs_t = cs.T
    cs_last = cs[t - 1:t, :]
    e_last = jnp.exp(cs_last)
    row = lax.broadcasted_iota(jnp.int32, (t, t), 0)
    col = lax.broadcasted_iota(jnp.int32, (t, t), 1)
    causal = row >= col

    def per_lane(v):
        parts = jnp.concatenate(_split3(v), axis=0)
        r = jnp.dot(parts, expand_ref[...], preferred_element_type=F32)
        return r[0:t] + r[t:2 * t] + r[2 * t:]

    dt_w = per_lane(dt)
    cs_w = per_lane(cs)
    e_cs_w = jnp.exp(cs_w)
    xdt = xs * dt_w
    xdt_b = xdt.astype(BF16)
    w_b = (xdt * jnp.exp(cs_w[t - 1:t, :] - cs_w)).astype(BF16)
    pair_lane = lax.broadcasted_iota(jnp.int32, (t, LANES), 1) < SSM_HEADDIM
    gw = SSM_WIDTH // SSM_GROUPS

    for g in range(SSM_GROUPS):
        heads = range(g * HEADS_PER_GROUP, (g + 1) * HEADS_PER_GROUP)
        bg = bm[:, g * SSM_DSTATE:(g + 1) * SSM_DSTATE]
        cg = cm[:, g * SSM_DSTATE:(g + 1) * SSM_DSTATE]
        cb = lax.dot_general(cg, bg, _NT, preferred_element_type=F32)
        s_prev = s_ref[heads.start:heads.stop].reshape(gw, SSM_DSTATE)
        y_in = lax.dot_general(cg, s_prev.astype(BF16), _NT, preferred_element_type=F32)
        st = lax.dot_general(w_b[:, g * gw:(g + 1) * gw], bg, _TN, preferred_element_type=F32)
        decay = jnp.concatenate([jnp.broadcast_to(e_last[:, h:h + 1], (SSM_HEADDIM, SSM_DSTATE)) for h in heads],
                                axis=0)
        s_ref[heads.start:heads.stop] = (decay * s_prev + st).reshape(HEADS_PER_GROUP, SSM_HEADDIM, SSM_DSTATE)

        for k in range(HEADS_PER_GROUP // 2):
            lanes = slice(g * gw + k * LANES, g * gw + (k + 1) * LANES)
            res = []
            for h in (heads.start + 2 * k, heads.start + 2 * k + 1):
                seg = jnp.where(causal, cs[:, h:h + 1] - cs_t[h:h + 1, :], NEG_BIG)
                scores = (cb * jnp.exp(seg)).astype(BF16)
                res.append(jnp.dot(scores, xdt_b[:, lanes], preferred_element_type=F32))
            y_ref[:, lanes] = (jnp.where(pair_lane, res[0], res[1])
                               + e_cs_w[:, lanes] * y_in[:, k * LANES:(k + 1) * LANES])

    y = (y_ref[...] + dexp_ref[...] * xs) * _silu(z)
    gw = SSM_WIDTH // SSM_GROUPS
    for g in range(SSM_GROUPS):
        yg = y[:, g * gw:(g + 1) * gw]
        yg = yg * lax.rsqrt(jnp.mean(yg * yg, axis=-1, keepdims=True) + EPS)
        o_ref[:, g * gw:(g + 1) * gw] = (yg * nw_ref[:, g * gw:(g + 1) * gw]).astype(BF16)

    @pl.when(n == pl.num_programs(1) - 1)
    def _():
        st_ref[...] = s_ref[...]


def _tri(t):
    return jnp.tril(jnp.ones((t, t), F32)).astype(BF16)


def _head_to_lanes():
    return jnp.asarray(np.arange(LANES)[:, None] == (np.arange(SSM_WIDTH)[None, :] // SSM_HEADDIM), BF16)


def _ssd_prompt(proj, p, bp, lp):
    nc = lp // SSD_T
    full = lambda b, n: (0, 0)
    rows = lambda b, n: b * nc + n
    dtb = jnp.pad(p['ssm_dt_bias'].reshape(1, SSM_HEADS), ((0, 0), (0, LANES - SSM_HEADS)))
    alog = jnp.pad(p['ssm_a_log'].reshape(1, SSM_HEADS), ((0, 0), (0, LANES - SSM_HEADS)))
    dexp = jnp.repeat(p['ssm_d'], SSM_HEADDIM).reshape(1, SSM_WIDTH)
    return pl.pallas_call(
        _ssd_prompt_kernel,
        grid=(bp, nc),
        in_specs=_pieces(SSZ_OFF, SSM_WIDTH, SSD_T, rows) + _pieces(SSX_OFF, SSM_CONV_DIM, SSD_T, rows) + [
            pl.BlockSpec((SSD_T, LANES), lambda b, n: (rows(b, n), SSDT_OFF // LANES)),
            pl.BlockSpec((SSD_T, SSD_T), full),
            pl.BlockSpec((LANES, SSM_WIDTH), full),
            pl.BlockSpec((CONV_W, SSM_CONV_DIM), full),
            pl.BlockSpec((1, SSM_CONV_DIM), full),
            pl.BlockSpec((1, LANES), full),
            pl.BlockSpec((1, LANES), full),
            pl.BlockSpec((1, SSM_WIDTH), full),
            pl.BlockSpec((1, SSM_WIDTH), full),
        ],
        out_specs=[
            pl.BlockSpec((SSD_T, SSM_WIDTH), lambda b, n: (b * nc + n, 0)),
            pl.BlockSpec((None, SSM_HEADS, SSM_HEADDIM, SSM_DSTATE), lambda b, n: (b, 0, 0, 0)),
            pl.BlockSpec((None, CONV_W - 1, SSM_CONV_DIM), lambda b, n: (b, 0, 0)),
        ],
        out_shape=[
            jax.ShapeDtypeStruct((bp * lp, SSM_WIDTH), BF16),
            jax.ShapeDtypeStruct((bp, SSM_HEADS, SSM_HEADDIM, SSM_DSTATE), F32),
            jax.ShapeDtypeStruct((bp, CONV_W - 1, SSM_CONV_DIM), F32),
        ],
        scratch_shapes=[
            pltpu.VMEM((SSD_T + SUBLANES, SSM_CONV_DIM), F32),
            pltpu.VMEM((SUBLANES, SSM_CONV_DIM), F32),
            pltpu.VMEM((SSM_HEADS, SSM_HEADDIM, SSM_DSTATE), F32),
            pltpu.VMEM((SSD_T, SSM_WIDTH), F32),
        ],
        compiler_params=pltpu.CompilerParams(
            dimension_semantics=("parallel", "arbitrary"), vmem_limit_bytes=VMEM_LIMIT),
        name="ssd_prompt",
    )(*([proj] * (N_Z + N_XBC + 1)), _tri(SSD_T), _head_to_lanes(), p['ssm_conv_w'],
      p['ssm_conv_b'].reshape(1, SSM_CONV_DIM), dtb, alog, dexp, p['ssm_norm_w'].reshape(1, SSM_WIDTH))


HG_T = 256

_NT = (((1,), (1,)), ((), ()))
_TN = (((0,), (0,)), ((), ()))


HG_COLS = 4 * HG_WIDTH
assert (HGQ_OFF, HGF_OFF, HGV_OFF, HGG_OFF) == (0, HG_WIDTH, 2 * HG_WIDTH, 3 * HG_WIDTH)


def _hg_tile(p_ref, off, head, rows=slice(None)):
    return p_ref[rows, off + head * LANES:off + (head + 1) * LANES]


def _hgrn_prompt_kernel(hg_ref, lb_ref, nw_ref, tri_ref, o_ref, st_ref, s_ref):
    n = pl.program_id(1)
    t, c = HG_T, HG_CHUNK
    nch = t // c

    @pl.when(n == 0)
    def _():
        s_ref[...] = jnp.zeros_like(s_ref)

    row = lax.broadcasted_iota(jnp.int32, (t, t), 0)
    col = lax.broadcasted_iota(jnp.int32, (t, t), 1)
    keep = (row >= col) & (row // c == col // c)
    tri = tri_ref[...]

    v, o_intra, kd, qe, decay = [], [], [], [], []
    for h in range(HG_HEADS):
        sl = slice(h * LANES, (h + 1) * LANES)
        lb = lb_ref[:, sl]
        fz = _hg_tile(hg_ref, HGF_OFF, h)
        logf = jnp.log(lb + (1.0 - lb) * jax.nn.sigmoid(fz))
        kk3 = ((1.0 - lb) * jax.nn.sigmoid(-fz)).reshape(nch, c, HG_DK)
        qh3 = _silu(_hg_tile(hg_ref, HGQ_OFF, h)).reshape(nch, c, HG_DK)
        vh = _hg_tile(hg_ref, HGV_OFF, h).astype(BF16)
        b3 = _cumsum_rows(tri, logf).reshape(nch, c, HG_DK)
        b_mid = b3[:, c // 2:c // 2 + 1, :]
        b_last = b3[:, c - 1:c, :]
        q_in = (qh3 * jnp.exp(b3 - b_mid)).reshape(t, HG_DK).astype(BF16)
        k_in = (kk3 * jnp.exp(b_mid - b3)).reshape(t, HG_DK).astype(BF16)
        a = jnp.where(keep, lax.dot_general(q_in, k_in, _NT, preferred_element_type=F32), 0.0)
        v.append(vh)
        o_intra.append(jnp.dot(a.astype(BF16), vh, preferred_element_type=F32))
        kd.append((kk3 * jnp.exp(b_last - b3)).astype(BF16))
        qe.append((qh3 * jnp.exp(b3)).astype(BF16))
        decay.append(jnp.exp(b_last))

    outs = [[] for _ in range(HG_HEADS)]
    for ci in range(nch):
        rows = slice(ci * c, (ci + 1) * c)
        for h in range(HG_HEADS):
            s_t = s_ref[h]
            o_inter = lax.dot_general(qe[h][ci], s_t.astype(BF16), _NT, preferred_element_type=F32)
            outs[h].append(o_intra[h][rows, :] + o_inter)
            ds_t = lax.dot_general(v[h][rows, :], kd[h][ci], _TN, preferred_element_type=F32)
            s_ref[h] = s_t * decay[h][ci] + ds_t

    for h in range(HG_HEADS):
        sl = slice(h * LANES, (h + 1) * LANES)
        o = jnp.concatenate(outs[h], axis=0)
        o = o * lax.rsqrt(jnp.mean(o * o, axis=-1, keepdims=True) + EPS)
        g = _hg_tile(hg_ref, HGG_OFF, h)
        o_ref[:, sl] = (o * nw_ref[:, sl] * _silu(g)).astype(BF16)

    @pl.when(n == pl.num_programs(1) - 1)
    def _():
        for h in range(HG_HEADS):
            st_ref[h] = s_ref[h].T


def _tri_chunks(t, c):
    r = np.arange(t)
    return jnp.asarray((r[:, None] >= r[None, :]) & (r[:, None] // c == r[None, :] // c), BF16)


def _hgrn_prompt(proj, p, lb, bp, lp):
    nc = lp // HG_T
    const = lambda b, n: (0, 0)
    return pl.pallas_call(
        _hgrn_prompt_kernel,
        grid=(bp, nc),
        in_specs=[
            pl.BlockSpec((HG_T, HG_COLS), lambda b, n: (b * nc + n, 0)),
            pl.BlockSpec((1, HG_WIDTH), const),
            pl.BlockSpec((1, HG_WIDTH), const),
            pl.BlockSpec((HG_T, HG_T), const),
        ],
        out_specs=[
            pl.BlockSpec((HG_T, HG_WIDTH), lambda b, n: (b * nc + n, 0)),
            pl.BlockSpec((None, HG_HEADS, HG_DK, HG_DV), lambda b, n: (b, 0, 0, 0)),
        ],
        out_shape=[
            jax.ShapeDtypeStruct((bp * lp, HG_WIDTH), BF16),
            jax.ShapeDtypeStruct((bp, HG_HEADS, HG_DK, HG_DV), F32),
        ],
        scratch_shapes=[pltpu.VMEM((HG_HEADS, HG_DV, HG_DK), F32)],
        compiler_params=pltpu.CompilerParams(
            dimension_semantics=("parallel", "arbitrary"), vmem_limit_bytes=VMEM_LIMIT),
        name="hgrn_prompt",
    )(proj, lb.reshape(1, HG_WIDTH), p['hg_norm_w'].reshape(1, HG_WIDTH),
      _tri_chunks(HG_T, HG_CHUNK))


RG_T = 256


def _scan_rows(a, u, h_in):
    t, w = a.shape
    rows = lax.broadcasted_iota(jnp.int32, a.shape, 0) % SUBLANES
    s = 1
    while s < SUBLANES:
        keep = rows >= s
        a_sh = jnp.where(keep, pltpu.roll(a, s, axis=0), 1.0)
        u_sh = jnp.where(keep, pltpu.roll(u, s, axis=0), 0.0)
        u = u + a * u_sh
        a = a * a_sh
        s *= 2
    hs = []
    for g in range(t // SUBLANES):
        sl = slice(g * SUBLANES, (g + 1) * SUBLANES)
        h = u[sl, :] + a[sl, :] * h_in
        hs.append(h)
        h_in = h[SUBLANES - 1:, :]
    return jnp.concatenate(hs, axis=0)


def _rg_gates(xc, wa_ref, ba_ref, wx_ref, bx_ref, lam_ref):
    xb = xc.astype(BF16)
    ra, ri = [], []
    for h in range(RG_HEADS):
        sl = slice(h * RG_HEAD_DIM, (h + 1) * RG_HEAD_DIM)
        ra.append(jnp.dot(xb[:, sl], wa_ref[h], preferred_element_type=F32))
        ri.append(jnp.dot(xb[:, sl], wx_ref[h], preferred_element_type=F32))
    r = jax.nn.sigmoid(jnp.concatenate(ra, axis=1) + ba_ref[...])
    ig = jax.nn.sigmoid(jnp.concatenate(ri, axis=1) + bx_ref[...])
    log_a = -RG_C * r * _softplus(-lam_ref[...])
    a = jnp.exp(log_a)
    mult = jnp.sqrt(_neg_expm1(2.0 * log_a, a * a))
    return a, mult, ig


def _gelu_tanh(x):
    return 0.5 * x * (1.0 + jnp.tanh(math.sqrt(2.0 / math.pi) * (x + 0.044715 * (x * x * x))))


N_RG = RG_WIDTH // PIECE


def _rglru_prompt_kernel(*refs):
    x_refs, gate_refs = refs[:N_RG], refs[N_RG:2 * N_RG]
    (cw_ref, cb_ref, wa_ref, ba_ref, wx_ref, bx_ref, lam_ref,
     o_ref, h_ref, cst_ref, xp_ref, tail_ref, hprev_ref) = refs[2 * N_RG:]
    n = pl.program_id(1)
    t = RG_T

    @pl.when(n == 0)
    def _():
        hprev_ref[...] = jnp.zeros_like(hprev_ref)

    x = _cat(x_refs)
    xc = _conv4(xp_ref, tail_ref, x, cw_ref, cb_ref, n == 0)
    cst_ref[...] = _cat(x_refs, slice(t - (CONV_W - 1), t))
    a, mult, ig = _rg_gates(xc, wa_ref, ba_ref, wx_ref, bx_ref, lam_ref)
    rows = lax.broadcasted_iota(jnp.int32, a.shape, 0)
    mult = jnp.where((rows == 0) & (n == 0), 1.0, mult)
    h = _scan_rows(a, mult * ig * xc, hprev_ref[...])
    hprev_ref[...] = h[t - 1:t, :]
    h_ref[...] = h[t - 1:t, :]
    o_ref[...] = (h * _gelu_tanh(_cat(gate_refs))).astype(BF16)


def _rglru_prompt(proj, p, bp, lp):
    nc = lp // RG_T
    full2 = lambda b, n: (0, 0)
    full3 = lambda b, n: (0, 0, 0)
    rows = lambda b, n: b * nc + n
    row = lambda a: a.reshape(1, RG_WIDTH)
    return pl.pallas_call(
        _rglru_prompt_kernel,
        grid=(bp, nc),
        in_specs=_pieces(RGX_OFF, RG_WIDTH, RG_T, rows) + _pieces(RGG_OFF, RG_WIDTH, RG_T, rows) + [
            pl.BlockSpec((CONV_W, RG_WIDTH), full2),
            pl.BlockSpec((1, RG_WIDTH), full2),
            pl.BlockSpec((RG_HEADS, RG_HEAD_DIM, RG_HEAD_DIM), full3),
            pl.BlockSpec((1, RG_WIDTH), full2),
            pl.BlockSpec((RG_HEADS, RG_HEAD_DIM, RG_HEAD_DIM), full3),
            pl.BlockSpec((1, RG_WIDTH), full2),
            pl.BlockSpec((1, RG_WIDTH), full2),
        ],
        out_specs=[
            pl.BlockSpec((RG_T, RG_WIDTH), lambda b, n: (b * nc + n, 0)),
            pl.BlockSpec((None, 1, RG_WIDTH), lambda b, n: (b, 0, 0)),
            pl.BlockSpec((None, CONV_W - 1, RG_WIDTH), lambda b, n: (b, 0, 0)),
        ],
        out_shape=[
            jax.ShapeDtypeStruct((bp * lp, RG_WIDTH), BF16),
            jax.ShapeDtypeStruct((bp, 1, RG_WIDTH), F32),
            jax.ShapeDtypeStruct((bp, CONV_W - 1, RG_WIDTH), F32),
        ],
        scratch_shapes=[
            pltpu.VMEM((RG_T + SUBLANES, RG_WIDTH), F32),
            pltpu.VMEM((SUBLANES, RG_WIDTH), F32),
            pltpu.VMEM((1, RG_WIDTH), F32),
        ],
        compiler_params=pltpu.CompilerParams(
            dimension_semantics=("parallel", "arbitrary"), vmem_limit_bytes=VMEM_LIMIT),
        name="rglru_prompt",
    )(*([proj] * (2 * N_RG)), p['rg_conv_w'], row(p['rg_conv_b']), p['rg_wa'].astype(BF16), row(p['rg_ba']),
      p['rg_wx'].astype(BF16), row(p['rg_bx']), row(p['rg_lambda']))


SB = 8


def _col_bcast(row):
    return jnp.broadcast_to(row, (LANES, LANES)).T


def _sample_kernel(p_ref,
                   hs_ref, rh_ref, rc_ref, ss_ref, sc_ref,
                   lb_ref, hnw_ref,
                   rcw_ref, rcb_ref, wa_ref, ba_ref, wx_ref, bx_ref, lam_ref,
                   scw_ref, scb_ref, dtb_ref, aexp_ref, dexp_ref, snw_ref, expand_ref,
                   ohg_ref, org_ref, ossm_ref, hs_out, rh_out, rc_out, ss_out, sc_out,
                   f_sc, kk_sc, q_sc, o_sc, adec_sc, xdt_sc, b_sc, c_sc, y_sc):
    x = p_ref[:, RGX_OFF:RGX_OFF + RG_WIDTH]
    xc = rcb_ref[...] + rcw_ref[CONV_W - 1:CONV_W, :] * x
    for k in range(CONV_W - 1):
        xc = xc + rcw_ref[k:k + 1, :] * rc_ref[k]
        rc_out[k] = x if k == CONV_W - 2 else rc_ref[k + 1]
    a, mult, ig = _rg_gates(xc, wa_ref, ba_ref, wx_ref, bx_ref, lam_ref)
    h = a * rh_ref[...] + mult * ig * xc
    rh_out[...] = h
    org_ref[...] = (h * _gelu_tanh(p_ref[:, RGG_OFF:RGG_OFF + RG_WIDTH])).astype(BF16)

    z = p_ref[:, SSZ_OFF:SSZ_OFF + SSM_WIDTH]
    xbc_raw = p_ref[:, SSX_OFF:SSX_OFF + SSM_CONV_DIM]
    dt_raw = p_ref[:, SSDT_OFF:SSDT_OFF + LANES]
    xbc = scb_ref[...] + scw_ref[CONV_W - 1:CONV_W, :] * xbc_raw
    for k in range(CONV_W - 1):
        xbc = xbc + scw_ref[k:k + 1, :] * sc_ref[k]
        sc_out[k] = xbc_raw if k == CONV_W - 2 else sc_ref[k + 1]
    xbc = _silu(xbc)
    xs = xbc[:, :SSM_WIDTH]
    gs = SSM_GROUPS * SSM_DSTATE
    b_sc[...] = xbc[:, SSM_WIDTH:SSM_WIDTH + gs]
    c_sc[...] = xbc[:, SSM_WIDTH + gs:]
    dt = _softplus(dt_raw + dtb_ref[...])
    parts = jnp.concatenate(_split3(dt), axis=0)
    r = jnp.dot(parts, expand_ref[...], preferred_element_type=F32)
    dt_exp = r[0:SB] + r[SB:2 * SB] + r[2 * SB:]
    xdt_sc[...] = xs * dt_exp
    adec_sc[...] = jnp.exp(dt_exp * aexp_ref[...])

    hg_tile = functools.partial(_hg_tile, p_ref)
    for hd in range(HG_HEADS):
        sl = slice(hd * LANES, (hd + 1) * LANES)
        lb = lb_ref[:, sl]
        fz = hg_tile(HGF_OFF, hd)
        f_sc[:, sl] = lb + (1.0 - lb) * jax.nn.sigmoid(fz)
        kk_sc[:, sl] = (1.0 - lb) * jax.nn.sigmoid(-fz)
        q_sc[:, sl] = _silu(hg_tile(HGQ_OFF, hd))

    for j in range(SB):
        row = slice(j, j + 1)
        for hd in range(HG_HEADS):
            sl = slice(hd * LANES, (hd + 1) * LANES)
            v_row = hg_tile(HGV_OFF, hd, row)
            s_new = _col_bcast(f_sc[row, sl]) * hs_ref[j, hd] + _col_bcast(kk_sc[row, sl]) * v_row
            hs_out[j, hd] = s_new
            q8 = jnp.broadcast_to(q_sc[row, sl], (SB, LANES)).astype(BF16)
            o_sc[row, sl] = jnp.dot(q8, s_new.astype(BF16), preferred_element_type=F32)[0:1]
        for hp in range(SSM_HEADS // 2):
            sl = slice(hp * LANES, (hp + 1) * LANES)
            g = (2 * hp) // HEADS_PER_GROUP
            gsl = slice(g * SSM_DSTATE, (g + 1) * SSM_DSTATE)
            s_old = ss_ref[j, 2 * hp:2 * hp + 2].reshape(LANES, SSM_DSTATE)
            s_new = _col_bcast(adec_sc[row, sl]) * s_old + _col_bcast(xdt_sc[row, sl]) * b_sc[row, gsl]
            ss_out[j, 2 * hp:2 * hp + 2] = s_new.reshape(2, SSM_HEADDIM, SSM_DSTATE)
            c8 = jnp.broadcast_to(c_sc[row, gsl], (SB, SSM_DSTATE)).astype(BF16)
            y_sc[row, sl] = lax.dot_general(c8, s_new.astype(BF16), _NT, preferred_element_type=F32)[0:1]

    for hd in range(HG_HEADS):
        sl = slice(hd * LANES, (hd + 1) * LANES)
        o = o_sc[:, sl]
        o = o * lax.rsqrt(jnp.mean(o * o, axis=-1, keepdims=True) + EPS)
        ohg_ref[:, sl] = (o * hnw_ref[:, sl] * _silu(hg_tile(HGG_OFF, hd))).astype(BF16)
    y = (y_sc[...] + dexp_ref[...] * xs) * _silu(z)
    gw = SSM_WIDTH // SSM_GROUPS
    for g in range(SSM_GROUPS):
        yg = y[:, g * gw:(g + 1) * gw]
        yg = yg * lax.rsqrt(jnp.mean(yg * yg, axis=-1, keepdims=True) + EPS)
        ossm_ref[:, g * gw:(g + 1) * gw] = (yg * snw_ref[:, g * gw:(g + 1) * gw]).astype(BF16)


def _sample_mix(proj, states, layer, p, lb, mp):
    hg_s, rg_h, rg_c_t, ssm_s, ssm_c_t = states
    bs = hg_s.shape[1]
    rb = mp // SB
    const2 = lambda i: (0, 0)
    const3 = lambda i: (0, 0, 0)
    row = lambda a, w: a.reshape(1, w)
    dtb = jnp.pad(p['ssm_dt_bias'].reshape(1, SSM_HEADS), ((0, 0), (0, LANES - SSM_HEADS)))
    aexp = jnp.repeat(-jnp.exp(p['ssm_a_log']), SSM_HEADDIM).reshape(1, SSM_WIDTH)
    dexp = jnp.repeat(p['ssm_d'], SSM_HEADDIM).reshape(1, SSM_WIDTH)
    expand = _head_to_lanes()
    in_specs = [
        pl.BlockSpec((SB, D_IN_PAD), lambda i: (rb + i, 0)),
        pl.BlockSpec((None, SB, HG_HEADS, HG_DK, HG_DV), lambda i: (layer, i, 0, 0, 0)),
        pl.BlockSpec((None, SB, RG_WIDTH), lambda i: (layer, i, 0)),
        pl.BlockSpec((None, CONV_W - 1, SB, RG_WIDTH), lambda i: (layer, 0, i, 0)),
        pl.BlockSpec((None, SB, SSM_HEADS, SSM_HEADDIM, SSM_DSTATE), lambda i: (layer, i, 0, 0, 0)),
        pl.BlockSpec((None, CONV_W - 1, SB, SSM_CONV_DIM), lambda i: (layer, 0, i, 0)),
        pl.BlockSpec((1, HG_WIDTH), const2),
        pl.BlockSpec((1, HG_WIDTH), const2),
        pl.BlockSpec((CONV_W, RG_WIDTH), const2),
        pl.BlockSpec((1, RG_WIDTH), const2),
        pl.BlockSpec((RG_HEADS, RG_HEAD_DIM, RG_HEAD_DIM), const3),
        pl.BlockSpec((1, RG_WIDTH), const2),
        pl.BlockSpec((RG_HEADS, RG_HEAD_DIM, RG_HEAD_DIM), const3),
        pl.BlockSpec((1, RG_WIDTH), const2),
        pl.BlockSpec((1, RG_WIDTH), const2),
        pl.BlockSpec((CONV_W, SSM_CONV_DIM), const2),
        pl.BlockSpec((1, SSM_CONV_DIM), const2),
        pl.BlockSpec((1, LANES), const2),
        pl.BlockSpec((1, SSM_WIDTH), const2),
        pl.BlockSpec((1, SSM_WIDTH), const2),
        pl.BlockSpec((1, SSM_WIDTH), const2),
        pl.BlockSpec((LANES, SSM_WIDTH), const2),
    ]
    out_specs = [
        pl.BlockSpec((SB, HG_WIDTH), lambda i: (i, 0)),
        pl.BlockSpec((SB, RG_WIDTH), lambda i: (i, 0)),
        pl.BlockSpec((SB, SSM_WIDTH), lambda i: (i, 0)),
        pl.BlockSpec((SB, HG_HEADS, HG_DK, HG_DV), lambda i: (i, 0, 0, 0)),
        pl.BlockSpec((SB, RG_WIDTH), lambda i: (i, 0)),
        pl.BlockSpec((CONV_W - 1, SB, RG_WIDTH), lambda i: (0, i, 0)),
        pl.BlockSpec((SB, SSM_HEADS, SSM_HEADDIM, SSM_DSTATE), lambda i: (i, 0, 0, 0)),
        pl.BlockSpec((CONV_W - 1, SB, SSM_CONV_DIM), lambda i: (0, i, 0)),
    ]
    out_shape = [
        jax.ShapeDtypeStruct((bs, HG_WIDTH), BF16),
        jax.ShapeDtypeStruct((bs, RG_WIDTH), BF16),
        jax.ShapeDtypeStruct((bs, SSM_WIDTH), BF16),
        jax.ShapeDtypeStruct(hg_s.shape[1:], F32),
        jax.ShapeDtypeStruct(rg_h.shape[1:], F32),
        jax.ShapeDtypeStruct(rg_c_t.shape[1:], F32),
        jax.ShapeDtypeStruct(ssm_s.shape[1:], F32),
        jax.ShapeDtypeStruct(ssm_c_t.shape[1:], F32),
    ]
    outs = pl.pallas_call(
        _sample_kernel,
        grid=(bs // SB,),
        in_specs=in_specs,
        out_specs=out_specs,
        out_shape=out_shape,
        scratch_shapes=[
            pltpu.VMEM((SB, HG_WIDTH), F32), pltpu.VMEM((SB, HG_WIDTH), F32), pltpu.VMEM((SB, HG_WIDTH), F32),
            pltpu.VMEM((SB, HG_WIDTH), F32),
            pltpu.VMEM((SB, SSM_WIDTH), F32), pltpu.VMEM((SB, SSM_WIDTH), F32),
            pltpu.VMEM((SB, SSM_GROUPS * SSM_DSTATE), F32), pltpu.VMEM((SB, SSM_GROUPS * SSM_DSTATE), F32),
            pltpu.VMEM((SB, SSM_WIDTH), F32),
        ],
        compiler_params=pltpu.CompilerParams(
            dimension_semantics=("parallel",), vmem_limit_bytes=VMEM_LIMIT),
        name="sample_mix",
    )(proj, hg_s, rg_h, rg_c_t, ssm_s, ssm_c_t,
      lb.reshape(1, HG_WIDTH), row(p['hg_norm_w'], HG_WIDTH),
      p['rg_conv_w'], row(p['rg_conv_b'], RG_WIDTH), p['rg_wa'].astype(BF16), row(p['rg_ba'], RG_WIDTH),
      p['rg_wx'].astype(BF16), row(p['rg_bx'], RG_WIDTH), row(p['rg_lambda'], RG_WIDTH),
      p['ssm_conv_w'], row(p['ssm_conv_b'], SSM_CONV_DIM), dtb, aexp, dexp, row(p['ssm_norm_w'], SSM_WIDTH),
      expand)
    return tuple(outs[:3]), tuple(outs[3:])


def kernel(x_prompt, x_sample, state_hgrn, state_rglru, state_rglru_conv, state_ssm, state_ssm_conv, norm_g, ffn1_w_gate, ffn1_w_up, ffn1_w_down, ffn2_w_gate, ffn2_w_up, ffn2_w_down, w_in, w_out, hg_lb_logits, hg_norm_w, rg_conv_w, rg_conv_b, rg_wa, rg_ba, rg_wx, rg_bx, rg_lambda, ssm_conv_w, ssm_conv_b, ssm_dt_bias, ssm_a_log, ssm_d, ssm_norm_w):
    bp, lp, _ = x_prompt.shape
    bs, ls, _ = x_sample.shape
    mp = bp * lp
    ms = bs * ls
    assert (mp + ms) % IN_ROWS == 0 and D_FF % FF_TILE == 0 and D_IN_PAD % IN_TILE == 0

    lw = {
        'hg_norm_w': hg_norm_w, 'rg_conv_w': rg_conv_w, 'rg_conv_b': rg_conv_b, 'rg_wa': rg_wa,
        'rg_ba': rg_ba, 'rg_wx': rg_wx, 'rg_bx': rg_bx, 'rg_lambda': rg_lambda,
        'ssm_conv_w': ssm_conv_w, 'ssm_conv_b': ssm_conv_b, 'ssm_dt_bias': ssm_dt_bias,
        'ssm_a_log': ssm_a_log, 'ssm_d': ssm_d, 'ssm_norm_w': ssm_norm_w,
    }
    lb_cum = jnp.cumsum(jax.nn.softmax(hg_lb_logits.astype(F32), axis=0), axis=0)
    lower_bounds = lb_cum - lb_cum[:1]

    taps_first = lambda a: jnp.transpose(a, (0, 2, 1, 3))
    sample_init = (state_hgrn, state_rglru, taps_first(state_rglru_conv), state_ssm, taps_first(state_ssm_conv))

    w_in_b = jnp.pad(w_in.astype(BF16), ((0, 0), (0, 0), (0, D_IN_PAD - D_IN_PROJ)))
    x = (x_prompt.reshape(mp, D_MODEL), x_sample.reshape(ms, D_MODEL))
    new_p = ([], [], [], [], [])
    new_s = ([], [], [], [], [])
    for l in range(DEPTH):
        g = norm_g[l].reshape(6, 1, D_MODEL)
        p = {name: arr[l] for name, arr in lw.items()}
        x = _ffn(x, g[0], g[1], ffn1_w_gate, ffn1_w_up, ffn1_w_down, l, mp, ms, split_in=(l == 0))
        proj = _inproj(x, g[2], w_in_b, l)
        o_hg, hg_new = _hgrn_prompt(proj, p, lower_bounds[l], bp, lp)
        o_rg, rg_h_new, rg_c_new = _rglru_prompt(proj, p, bp, lp)
        o_ssm, ssm_new, ssm_c_new = _ssd_prompt(proj, p, bp, lp)
        st_p = (hg_new, rg_h_new.reshape(bp, RG_WIDTH), rg_c_new, ssm_new, ssm_c_new)
        o_s, st_s = _sample_mix(proj, sample_init, l, p, lower_bounds[l], mp)
        x = _outproj(x, (o_hg, o_rg, o_ssm), o_s, g[3], w_out, l)
        for acc, s in zip(new_p, st_p):
            acc.append(s)
        for acc, s in zip(new_s, st_s):
            acc.append(s)
        if l < DEPTH - 1:
            x = _ffn(x, g[4], g[5], ffn2_w_gate, ffn2_w_up, ffn2_w_down, l, mp, ms)
        else:
            y_p, y_s, hg_s, ssm_s = _ffn(x, g[4], g[5], ffn2_w_gate, ffn2_w_up, ffn2_w_down, l, mp, ms,
                                         split_out=True, stack=(tuple(new_s[0]), tuple(new_s[3])))
    hg_p, rg_p, rgc_p, ssm_p, ssmc_p = (jnp.stack(a) for a in new_p)
    rg_s, rgc_s, ssmc_s = (jnp.stack(new_s[k]) for k in (1, 2, 4))
    rgc_s, ssmc_s = taps_first(rgc_s), taps_first(ssmc_s)
    y_prompt = y_p.reshape(bp, lp, D_MODEL)
    y_sample = y_s.reshape(bs, ls, D_MODEL)
    return (y_prompt, y_sample, hg_p, hg_s, rg_p, rg_s, rgc_p, rgc_s, ssm_p, ssm_s, ssmc_p, ssmc_s)
```

```python
import functools
import math

import jax
import jax.numpy as jnp
import numpy as np
from jax import lax
from jax.experimental import pallas as pl
from jax.experimental.pallas import tpu as pltpu

F32 = jnp.float32
BF16 = jnp.bfloat16

D_MODEL = 2048
DEPTH = 2
EPS = 1e-6
CONV_W = 4
HG_HEADS = 4
HG_DK = 128
HG_DV = 128
HG_WIDTH = HG_HEADS * HG_DV
HG_CHUNK = 32
RG_HEADS = 6
RG_HEAD_DIM = 128
RG_WIDTH = RG_HEADS * RG_HEAD_DIM
RG_C = 8.0
SSM_HEADS = 12
SSM_HEADDIM = 64
SSM_WIDTH = SSM_HEADS * SSM_HEADDIM
SSM_GROUPS = 2
HEADS_PER_GROUP = SSM_HEADS // SSM_GROUPS
SSM_DSTATE = 128
SSM_CHUNK = 64
SSM_CONV_DIM = SSM_WIDTH + 2 * SSM_GROUPS * SSM_DSTATE
D_MIX = HG_WIDTH + RG_WIDTH + SSM_WIDTH
IN_SIZES = (HG_HEADS * HG_DK, HG_HEADS * HG_DK, HG_WIDTH, HG_WIDTH, RG_WIDTH, RG_WIDTH,
            SSM_WIDTH, SSM_CONV_DIM, SSM_HEADS)
D_IN_PROJ = sum(IN_SIZES)
D_FF = 5632

LANES = 128
D_IN_PAD = -(-D_IN_PROJ // LANES) * LANES
FFN_ROWS = 1040
FF_TILE = 256
IN_ROWS = 640
IN_TILE = 1152
VMEM_LIMIT = 56 * 1024 * 1024
FFN_VMEM_LIMIT = 62 * 1024 * 1024


def _rms(x, g):
    return x * lax.rsqrt(jnp.mean(x * x, axis=-1, keepdims=True) + EPS) * g


def _ffn_kernel(*refs, split_in, split_out, n_prompt_last):
    refs = list(refs)
    x_ref = refs.pop(0)
    xs_ref = refs.pop(0) if split_in else None
    gin_ref, gout_ref, wg_ref, wu_ref, wd_ref = (refs.pop(0) for _ in range(5))
    o_ref = refs.pop(0)
    ys_ref = refs.pop(0) if split_out else None
    xn_ref = refs.pop(0)
    i, j = pl.program_id(0), pl.program_id(1)
    last_i = pl.num_programs(0) - 1
    npl = n_prompt_last

    @pl.when(j == 0)
    def _():
        o_ref[...] = jnp.zeros_like(o_ref)
        if split_in:
            @pl.when(i < last_i)
            def _():
                xn_ref[...] = _rms(x_ref[...], gin_ref[...]).astype(BF16)

            @pl.when(i == last_i)
            def _():
                xn_ref[0:npl, :] = _rms(x_ref[0:npl, :], gin_ref[...]).astype(BF16)
                xn_ref[npl:, :] = _rms(xs_ref[...], gin_ref[...]).astype(BF16)
        else:
            xn_ref[...] = _rms(x_ref[...], gin_ref[...]).astype(BF16)

    xn = xn_ref[...]
    g = jnp.dot(xn, wg_ref[...].astype(BF16), preferred_element_type=F32)
    u = jnp.dot(xn, wu_ref[...].astype(BF16), preferred_element_type=F32)
    h = (g * jax.nn.sigmoid(g) * u).astype(BF16)
    o_ref[...] += jnp.dot(h, wd_ref[...].astype(BF16), preferred_element_type=F32)

    @pl.when(j == pl.num_programs(1) - 1)
    def _():
        if split_in:
            @pl.when(i < last_i)
            def _():
                o_ref[...] = x_ref[...] + 0.5 * _rms(o_ref[...], gout_ref[...])

            @pl.when(i == last_i)
            def _():
                o_ref[0:npl, :] = x_ref[0:npl, :] + 0.5 * _rms(o_ref[0:npl, :], gout_ref[...])
                o_ref[npl:, :] = xs_ref[...] + 0.5 * _rms(o_ref[npl:, :], gout_ref[...])
        else:
            o_ref[...] = x_ref[...] + 0.5 * _rms(o_ref[...], gout_ref[...])
        if split_out:
            @pl.when(i == last_i)
            def _():
                ys_ref[...] = o_ref[npl:, :]


def _ffn(x, g_in, g_out, wg, wu, wd, layer, mp, ms, split_in=False, split_out=False):
    m = mp + ms
    n_tiles = m // FFN_ROWS
    n_prompt_last = mp - (n_tiles - 1) * FFN_ROWS
    assert m % FFN_ROWS == 0 and n_prompt_last + ms == FFN_ROWS and n_prompt_last % 16 == 0
    rows = pl.BlockSpec((FFN_ROWS, D_MODEL), lambda i, j: (i, 0))
    sample = pl.BlockSpec((ms, D_MODEL), lambda i, j: (0, 0))
    vec = pl.BlockSpec((1, D_MODEL), lambda i, j: (0, 0))
    xs = tuple(x) if split_in else (x,)
    if split_out:
        out_specs = [rows, sample]
        out_shape = [jax.ShapeDtypeStruct((mp, D_MODEL), F32), jax.ShapeDtypeStruct((ms, D_MODEL), F32)]
    else:
        out_specs = rows
        out_shape = jax.ShapeDtypeStruct((m, D_MODEL), F32)
    return pl.pallas_call(
        functools.partial(_ffn_kernel, split_in=split_in, split_out=split_out, n_prompt_last=n_prompt_last),
        grid=(n_tiles, D_FF // FF_TILE),
        in_specs=[rows] + ([sample] if split_in else []) + [
            vec, vec,
            pl.BlockSpec((None, D_MODEL, FF_TILE), lambda i, j: (layer, 0, j)),
            pl.BlockSpec((None, D_MODEL, FF_TILE), lambda i, j: (layer, 0, j)),
            pl.BlockSpec((None, FF_TILE, D_MODEL), lambda i, j: (layer, j, 0)),
        ],
        out_specs=out_specs,
        out_shape=out_shape,
        scratch_shapes=[pltpu.VMEM((FFN_ROWS, D_MODEL), BF16)],
        compiler_params=pltpu.CompilerParams(
            dimension_semantics=("parallel", "arbitrary"), vmem_limit_bytes=FFN_VMEM_LIMIT),
        name="ffn",
    )(*xs, g_in, g_out, wg, wu, wd)


def _inproj_kernel(x_ref, g_ref, w_ref, o_ref, xn_ref):
    @pl.when(pl.program_id(1) == 0)
    def _():
        xn_ref[...] = _rms(x_ref[...], g_ref[...]).astype(BF16)

    o_ref[...] = jnp.dot(xn_ref[...], w_ref[...], preferred_element_type=F32)


def _inproj(x, g, w_in, layer):
    m = x.shape[0]
    return pl.pallas_call(
        _inproj_kernel,
        grid=(m // IN_ROWS, D_IN_PAD // IN_TILE),
        in_specs=[
            pl.BlockSpec((IN_ROWS, D_MODEL), lambda i, j: (i, 0)),
            pl.BlockSpec((1, D_MODEL), lambda i, j: (0, 0)),
            pl.BlockSpec((None, D_MODEL, IN_TILE), lambda i, j: (layer, 0, j)),
        ],
        out_specs=pl.BlockSpec((IN_ROWS, IN_TILE), lambda i, j: (i, j)),
        out_shape=jax.ShapeDtypeStruct((m, D_IN_PAD), F32),
        scratch_shapes=[pltpu.VMEM((IN_ROWS, D_MODEL), BF16)],
        compiler_params=pltpu.CompilerParams(
            dimension_semantics=("parallel", "arbitrary"), vmem_limit_bytes=VMEM_LIMIT),
        name="inproj",
    )(x, g, w_in)


OUT_TILE = 512


def _outproj_kernel(x_ref, php_ref, prg_ref, pss_ref, shg_ref, srg_ref, sss_ref, g_ref, w_ref, y_ref, wb_ref):
    i = pl.program_id(0)
    last = pl.num_programs(0) - 1
    r0, r1 = HG_WIDTH, HG_WIDTH + RG_WIDTH

    @pl.when(i == 0)
    def _():
        wb_ref[...] = w_ref[...].astype(BF16)

    def mixed(ohg, org, oss):
        m = jnp.dot(ohg, wb_ref[0:r0, :], preferred_element_type=F32)
        m += jnp.dot(org, wb_ref[r0:r1, :], preferred_element_type=F32)
        m += jnp.dot(oss, wb_ref[r1:, :], preferred_element_type=F32)
        return _rms(m, g_ref[...])

    @pl.when(i < last)
    def _():
        y_ref[...] = x_ref[...] + mixed(php_ref[...], prg_ref[...], pss_ref[...])

    @pl.when(i == last)
    def _():
        ns = shg_ref.shape[0]
        y_ref[0:ns, :] = x_ref[0:ns, :] + mixed(shg_ref[...], srg_ref[...], sss_ref[...])


def _outproj(x, o_prompt, o_sample, g, w_out, layer):
    mp, ms = o_prompt[0].shape[0], o_sample[0].shape[0]
    assert mp % OUT_TILE == 0 and ms <= OUT_TILE and x.shape[0] == mp + ms
    n_p = mp // OUT_TILE
    widths = (HG_WIDTH, RG_WIDTH, SSM_WIDTH)
    return pl.pallas_call(
        _outproj_kernel,
        grid=(n_p + 1,),
        in_specs=[pl.BlockSpec((OUT_TILE, D_MODEL), lambda i: (i, 0))]
        + [pl.BlockSpec((OUT_TILE, w), lambda i: (jnp.minimum(i, n_p - 1), 0)) for w in widths]
        + [pl.BlockSpec((ms, w), lambda i: (0, 0)) for w in widths]
        + [pl.BlockSpec((1, D_MODEL), lambda i: (0, 0)),
           pl.BlockSpec((None, D_MIX, D_MODEL), lambda i: (layer, 0, 0), pipeline_mode=pl.Buffered(1))],
        out_specs=pl.BlockSpec((OUT_TILE, D_MODEL), lambda i: (i, 0)),
        out_shape=jax.ShapeDtypeStruct((mp + ms, D_MODEL), F32),
        scratch_shapes=[pltpu.VMEM((D_MIX, D_MODEL), BF16)],
        compiler_params=pltpu.CompilerParams(
            dimension_semantics=("arbitrary",), vmem_limit_bytes=VMEM_LIMIT),
        name="outproj",
    )(x, *o_prompt, *o_sample, g, w_out)


_COL = np.cumsum((0,) + IN_SIZES)
HGQ_OFF, HGF_OFF, HGV_OFF, HGG_OFF, RGX_OFF, RGG_OFF, SSZ_OFF, SSX_OFF, SSDT_OFF = (int(c) for c in _COL[:9])
PIECE = 256
assert all(off % PIECE == 0 for off in (RGX_OFF, RGG_OFF, SSZ_OFF, SSX_OFF)) and SSDT_OFF % LANES == 0
assert all(w % PIECE == 0 for w in (RG_WIDTH, SSM_WIDTH, SSM_CONV_DIM))


def _pieces(off, width, rows, row_map):
    return [pl.BlockSpec((rows, PIECE), functools.partial(lambda k, *g: (row_map(*g), k), off // PIECE + k))
            for k in range(width // PIECE)]


def _cat(refs, rows=slice(None)):
    return jnp.concatenate([r[rows, :] for r in refs], axis=1)


SUBLANES = 8
SSD_T = 256
NEG_BIG = -1e30


def _split3(x):
    hi = x.astype(BF16)
    r = x - hi.astype(F32)
    mid = r.astype(BF16)
    lo = (r - mid.astype(F32)).astype(BF16)
    return hi, mid, lo


def _cumsum_rows(tri, x):
    w = x.shape[1]
    parts = jnp.concatenate(_split3(x), axis=1)
    r = jnp.dot(tri, parts, preferred_element_type=F32)
    return r[:, :w] + r[:, w:2 * w] + r[:, 2 * w:]


def _silu(x):
    return x * jax.nn.sigmoid(x)


def _neg_expm1(x, exp_x):
    series = -x * (1.0 + x * (1 / 2 + x * (1 / 6 + x * (1 / 24 + x * (1 / 120)))))
    return jnp.where(x > -1 / 16, series, 1.0 - exp_x)


def _softplus(x):
    return jnp.maximum(x, 0.0) + jnp.log(1.0 + jnp.exp(-jnp.abs(x)))


def _conv4(xp_ref, tail_ref, x, w_ref, b_ref, first):
    t = x.shape[0]

    @pl.when(first)
    def _():
        tail_ref[...] = jnp.zeros_like(tail_ref)

    xp_ref[0:SUBLANES, :] = tail_ref[...]
    xp_ref[SUBLANES:, :] = x
    tail_ref[...] = x[t - SUBLANES:, :]
    y = b_ref[...] + w_ref[CONV_W - 1:CONV_W, :] * x
    for k in range(CONV_W - 1):
        off = SUBLANES - (CONV_W - 1) + k
        y = y + w_ref[k:k + 1, :] * xp_ref[off:off + t, :]
    return y


N_Z, N_XBC = SSM_WIDTH // PIECE, SSM_CONV_DIM // PIECE


def _ssd_prompt_kernel(*refs):
    z_refs, xbc_refs = refs[:N_Z], refs[N_Z:N_Z + N_XBC]
    (dt_ref, tri_ref, expand_ref, cw_ref, cb_ref, dtb_ref, alog_ref, dexp_ref, nw_ref,
     o_ref, st_ref, cst_ref, xp_ref, tail_ref, s_ref, y_ref) = refs[N_Z + N_XBC:]
    n = pl.program_id(1)
    t = SSD_T
    z = _cat(z_refs)
    xbc_raw = _cat(xbc_refs)
    dt_raw = dt_ref[...]

    @pl.when(n == 0)
    def _():
        s_ref[...] = jnp.zeros_like(s_ref)

    xbc = _silu(_conv4(xp_ref, tail_ref, xbc_raw, cw_ref, cb_ref, n == 0))
    cst_ref[...] = _cat(xbc_refs, slice(t - (CONV_W - 1), t))
    xs = xbc[:, :SSM_WIDTH]
    gs = SSM_GROUPS * SSM_DSTATE
    bm = xbc[:, SSM_WIDTH:SSM_WIDTH + gs].astype(BF16)
    cm = xbc[:, SSM_WIDTH + gs:].astype(BF16)

    dt = _softplus(dt_raw + dtb_ref[...])
    a = dt * -jnp.exp(alog_ref[...])
    cs = _cumsum_rows(tri_ref[...], a)
    cs_t = cs.T
    cs_last = cs[t - 1:t, :]
    e_last = jnp.exp(cs_last)
    row = lax.broadcasted_iota(jnp.int32, (t, t), 0)
    col = lax.broadcasted_iota(jnp.int32, (t, t), 1)
    causal = row >= col

    def per_lane(v):
        parts = jnp.concatenate(_split3(v), axis=0)
        r = jnp.dot(parts, expand_ref[...], preferred_element_type=F32)
        return r[0:t] + r[t:2 * t] + r[2 * t:]

    dt_w = per_lane(dt)
    cs_w = per_lane(cs)
    e_cs_w = jnp.exp(cs_w)
    xdt = xs * dt_w
    xdt_b = xdt.astype(BF16)
    w_b = (xdt * jnp.exp(cs_w[t - 1:t, :] - cs_w)).astype(BF16)
    pair_lane = lax.broadcasted_iota(jnp.int32, (t, LANES), 1) < SSM_HEADDIM
    gw = SSM_WIDTH // SSM_GROUPS

    for g in range(SSM_GROUPS):
        heads = range(g * HEADS_PER_GROUP, (g + 1) * HEADS_PER_GROUP)
        bg = bm[:, g * SSM_DSTATE:(g + 1) * SSM_DSTATE]
        cg = cm[:, g * SSM_DSTATE:(g + 1) * SSM_DSTATE]
        cb = lax.dot_general(cg, bg, _NT, preferred_element_type=F32)
        s_prev = s_ref[heads.start:heads.stop].reshape(gw, SSM_DSTATE)
        y_in = lax.dot_general(cg, s_prev.astype(BF16), _NT, preferred_element_type=F32)
        st = lax.dot_general(w_b[:, g * gw:(g + 1) * gw], bg, _TN, preferred_element_type=F32)
        decay = jnp.concatenate([jnp.broadcast_to(e_last[:, h:h + 1], (SSM_HEADDIM, SSM_DSTATE)) for h in heads],
                                axis=0)
        s_ref[heads.start:heads.stop] = (decay * s_prev + st).reshape(HEADS_PER_GROUP, SSM_HEADDIM, SSM_DSTATE)

        for k in range(HEADS_PER_GROUP // 2):
            lanes = slice(g * gw + k * LANES, g * gw + (k + 1) * LANES)
            res = []
            for h in (heads.start + 2 * k, heads.start + 2 * k + 1):
                seg = jnp.where(causal, cs[:, h:h + 1] - cs_t[h:h + 1, :], NEG_BIG)
                scores = (cb * jnp.exp(seg)).astype(BF16)
                res.append(jnp.dot(scores, xdt_b[:, lanes], preferred_element_type=F32))
            y_ref[:, lanes] = (jnp.where(pair_lane, res[0], res[1])
                               + e_cs_w[:, lanes] * y_in[:, k * LANES:(k + 1) * LANES])

    y = (y_ref[...] + dexp_ref[...] * xs) * _silu(z)
    gw = SSM_WIDTH // SSM_GROUPS
    for g in range(SSM_GROUPS):
        yg = y[:, g * gw:(g + 1) * gw]
        yg = yg * lax.rsqrt(jnp.mean(yg * yg, axis=-1, keepdims=True) + EPS)
        o_ref[:, g * gw:(g + 1) * gw] = (yg * nw_ref[:, g * gw:(g + 1) * gw]).astype(BF16)

    @pl.when(n == pl.num_programs(1) - 1)
    def _():
        st_ref[...] = s_ref[...]


def _tri(t):
    return jnp.tril(jnp.ones((t, t), F32)).astype(BF16)


def _head_to_lanes():
    return jnp.asarray(np.arange(LANES)[:, None] == (np.arange(SSM_WIDTH)[None, :] // SSM_HEADDIM), BF16)


def _ssd_prompt(proj, p, bp, lp):
    nc = lp // SSD_T
    full = lambda b, n: (0, 0)
    rows = lambda b, n: b * nc + n
    dtb = jnp.pad(p['ssm_dt_bias'].reshape(1, SSM_HEADS), ((0, 0), (0, LANES - SSM_HEADS)))
    alog = jnp.pad(p['ssm_a_log'].reshape(1, SSM_HEADS), ((0, 0), (0, LANES - SSM_HEADS)))
    dexp = jnp.repeat(p['ssm_d'], SSM_HEADDIM).reshape(1, SSM_WIDTH)
    return pl.pallas_call(
        _ssd_prompt_kernel,
        grid=(bp, nc),
        in_specs=_pieces(SSZ_OFF, SSM_WIDTH, SSD_T, rows) + _pieces(SSX_OFF, SSM_CONV_DIM, SSD_T, rows) + [
            pl.BlockSpec((SSD_T, LANES), lambda b, n: (rows(b, n), SSDT_OFF // LANES)),
            pl.BlockSpec((SSD_T, SSD_T), full),
            pl.BlockSpec((LANES, SSM_WIDTH), full),
            pl.BlockSpec((CONV_W, SSM_CONV_DIM), full),
            pl.BlockSpec((1, SSM_CONV_DIM), full),
            pl.BlockSpec((1, LANES), full),
            pl.BlockSpec((1, LANES), full),
            pl.BlockSpec((1, SSM_WIDTH), full),
            pl.BlockSpec((1, SSM_WIDTH), full),
        ],
        out_specs=[
            pl.BlockSpec((SSD_T, SSM_WIDTH), lambda b, n: (b * nc + n, 0)),
            pl.BlockSpec((None, SSM_HEADS, SSM_HEADDIM, SSM_DSTATE), lambda b, n: (b, 0, 0, 0)),
            pl.BlockSpec((None, CONV_W - 1, SSM_CONV_DIM), lambda b, n: (b, 0, 0)),
        ],
        out_shape=[
            jax.ShapeDtypeStruct((bp * lp, SSM_WIDTH), BF16),
            jax.ShapeDtypeStruct((bp, SSM_HEADS, SSM_HEADDIM, SSM_DSTATE), F32),
            jax.ShapeDtypeStruct((bp, CONV_W - 1, SSM_CONV_DIM), F32),
        ],
        scratch_shapes=[
            pltpu.VMEM((SSD_T + SUBLANES, SSM_CONV_DIM), F32),
            pltpu.VMEM((SUBLANES, SSM_CONV_DIM), F32),
            pltpu.VMEM((SSM_HEADS, SSM_HEADDIM, SSM_DSTATE), F32),
            pltpu.VMEM((SSD_T, SSM_WIDTH), F32),
        ],
        compiler_params=pltpu.CompilerParams(
            dimension_semantics=("parallel", "arbitrary"), vmem_limit_bytes=VMEM_LIMIT),
        name="ssd_prompt",
    )(*([proj] * (N_Z + N_XBC + 1)), _tri(SSD_T), _head_to_lanes(), p['ssm_conv_w'],
      p['ssm_conv_b'].reshape(1, SSM_CONV_DIM), dtb, alog, dexp, p['ssm_norm_w'].reshape(1, SSM_WIDTH))


HG_T = 256

_NT = (((1,), (1,)), ((), ()))
_TN = (((0,), (0,)), ((), ()))


HG_COLS = 4 * HG_WIDTH
assert (HGQ_OFF, HGF_OFF, HGV_OFF, HGG_OFF) == (0, HG_WIDTH, 2 * HG_WIDTH, 3 * HG_WIDTH)


def _hg_tile(p_ref, off, head, rows=slice(None)):
    return p_ref[rows, off + head * LANES:off + (head + 1) * LANES]


def _hgrn_prompt_kernel(hg_ref, lb_ref, nw_ref, tri_ref, o_ref, st_ref, s_ref):
    n = pl.program_id(1)
    t, c = HG_T, HG_CHUNK
    nch = t // c

    @pl.when(n == 0)
    def _():
        s_ref[...] = jnp.zeros_like(s_ref)

    row = lax.broadcasted_iota(jnp.int32, (t, t), 0)
    col = lax.broadcasted_iota(jnp.int32, (t, t), 1)
    keep = (row >= col) & (row // c == col // c)
    tri = tri_ref[...]

    v, o_intra, kd, qe, decay = [], [], [], [], []
    for h in range(HG_HEADS):
        sl = slice(h * LANES, (h + 1) * LANES)
        lb = lb_ref[:, sl]
        fz = _hg_tile(hg_ref, HGF_OFF, h)
        logf = jnp.log(lb + (1.0 - lb) * jax.nn.sigmoid(fz))
        kk3 = ((1.0 - lb) * jax.nn.sigmoid(-fz)).reshape(nch, c, HG_DK)
        qh3 = _silu(_hg_tile(hg_ref, HGQ_OFF, h)).reshape(nch, c, HG_DK)
        vh = _hg_tile(hg_ref, HGV_OFF, h).astype(BF16)
        b3 = _cumsum_rows(tri, logf).reshape(nch, c, HG_DK)
        b_mid = b3[:, c // 2:c // 2 + 1, :]
        b_last = b3[:, c - 1:c, :]
        q_in = (qh3 * jnp.exp(b3 - b_mid)).reshape(t, HG_DK).astype(BF16)
        k_in = (kk3 * jnp.exp(b_mid - b3)).reshape(t, HG_DK).astype(BF16)
        a = jnp.where(keep, lax.dot_general(q_in, k_in, _NT, preferred_element_type=F32), 0.0)
        v.append(vh)
        o_intra.append(jnp.dot(a.astype(BF16), vh, preferred_element_type=F32))
        kd.append((kk3 * jnp.exp(b_last - b3)).astype(BF16))
        qe.append((qh3 * jnp.exp(b3)).astype(BF16))
        decay.append(jnp.exp(b_last))

    outs = [[] for _ in range(HG_HEADS)]
    for ci in range(nch):
        rows = slice(ci * c, (ci + 1) * c)
        for h in range(HG_HEADS):
            s_t = s_ref[h]
            o_inter = lax.dot_general(qe[h][ci], s_t.astype(BF16), _NT, preferred_element_type=F32)
            outs[h].append(o_intra[h][rows, :] + o_inter)
            ds_t = lax.dot_general(v[h][rows, :], kd[h][ci], _TN, preferred_element_type=F32)
            s_ref[h] = s_t * decay[h][ci] + ds_t

    for h in range(HG_HEADS):
        sl = slice(h * LANES, (h + 1) * LANES)
        o = jnp.concatenate(outs[h], axis=0)
        o = o * lax.rsqrt(jnp.mean(o * o, axis=-1, keepdims=True) + EPS)
        g = _hg_tile(hg_ref, HGG_OFF, h)
        o_ref[:, sl] = (o * nw_ref[:, sl] * _silu(g)).astype(BF16)

    @pl.when(n == pl.num_programs(1) - 1)
    def _():
        for h in range(HG_HEADS):
            st_ref[h] = s_ref[h].T


def _tri_chunks(t, c):
    r = np.arange(t)
    return jnp.asarray((r[:, None] >= r[None, :]) & (r[:, None] // c == r[None, :] // c), BF16)


def _hgrn_prompt(proj, p, lb, bp, lp):
    nc = lp // HG_T
    const = lambda b, n: (0, 0)
    return pl.pallas_call(
        _hgrn_prompt_kernel,
        grid=(bp, nc),
        in_specs=[
            pl.BlockSpec((HG_T, HG_COLS), lambda b, n: (b * nc + n, 0)),
            pl.BlockSpec((1, HG_WIDTH), const),
            pl.BlockSpec((1, HG_WIDTH), const),
            pl.BlockSpec((HG_T, HG_T), const),
        ],
        out_specs=[
            pl.BlockSpec((HG_T, HG_WIDTH), lambda b, n: (b * nc + n, 0)),
            pl.BlockSpec((None, HG_HEADS, HG_DK, HG_DV), lambda b, n: (b, 0, 0, 0)),
        ],
        out_shape=[
            jax.ShapeDtypeStruct((bp * lp, HG_WIDTH), BF16),
            jax.ShapeDtypeStruct((bp, HG_HEADS, HG_DK, HG_DV), F32),
        ],
        scratch_shapes=[pltpu.VMEM((HG_HEADS, HG_DV, HG_DK), F32)],
        compiler_params=pltpu.CompilerParams(
            dimension_semantics=("parallel", "arbitrary"), vmem_limit_bytes=VMEM_LIMIT),
        name="hgrn_prompt",
    )(proj, lb.reshape(1, HG_WIDTH), p['hg_norm_w'].reshape(1, HG_WIDTH),
      _tri_chunks(HG_T, HG_CHUNK))


RG_T = 256


def _scan_rows(a, u, h_in):
    t, w = a.shape
    rows = lax.broadcasted_iota(jnp.int32, a.shape, 0) % SUBLANES
    s = 1
    while s < SUBLANES:
        keep = rows >= s
        a_sh = jnp.where(keep, pltpu.roll(a, s, axis=0), 1.0)
        u_sh = jnp.where(keep, pltpu.roll(u, s, axis=0), 0.0)
        u = u + a * u_sh
        a = a * a_sh
        s *= 2
    hs = []
    for g in range(t // SUBLANES):
        sl = slice(g * SUBLANES, (g + 1) * SUBLANES)
        h = u[sl, :] + a[sl, :] * h_in
        hs.append(h)
        h_in = h[SUBLANES - 1:, :]
    return jnp.concatenate(hs, axis=0)


def _rg_gates(xc, wa_ref, ba_ref, wx_ref, bx_ref, lam_ref):
    xb = xc.astype(BF16)
    ra, ri = [], []
    for h in range(RG_HEADS):
        sl = slice(h * RG_HEAD_DIM, (h + 1) * RG_HEAD_DIM)
        ra.append(jnp.dot(xb[:, sl], wa_ref[h], preferred_element_type=F32))
        ri.append(jnp.dot(xb[:, sl], wx_ref[h], preferred_element_type=F32))
    r = jax.nn.sigmoid(jnp.concatenate(ra, axis=1) + ba_ref[...])
    ig = jax.nn.sigmoid(jnp.concatenate(ri, axis=1) + bx_ref[...])
    log_a = -RG_C * r * _softplus(-lam_ref[...])
    a = jnp.exp(log_a)
    mult = jnp.sqrt(_neg_expm1(2.0 * log_a, a * a))
    return a, mult, ig


def _gelu_tanh(x):
    return 0.5 * x * (1.0 + jnp.tanh(math.sqrt(2.0 / math.pi) * (x + 0.044715 * (x * x * x))))


N_RG = RG_WIDTH // PIECE


def _rglru_prompt_kernel(*refs):
    x_refs, gate_refs = refs[:N_RG], refs[N_RG:2 * N_RG]
    (cw_ref, cb_ref, wa_ref, ba_ref, wx_ref, bx_ref, lam_ref,
     o_ref, h_ref, cst_ref, xp_ref, tail_ref, hprev_ref) = refs[2 * N_RG:]
    n = pl.program_id(1)
    t = RG_T

    @pl.when(n == 0)
    def _():
        hprev_ref[...] = jnp.zeros_like(hprev_ref)

    x = _cat(x_refs)
    xc = _conv4(xp_ref, tail_ref, x, cw_ref, cb_ref, n == 0)
    cst_ref[...] = _cat(x_refs, slice(t - (CONV_W - 1), t))
    a, mult, ig = _rg_gates(xc, wa_ref, ba_ref, wx_ref, bx_ref, lam_ref)
    rows = lax.broadcasted_iota(jnp.int32, a.shape, 0)
    mult = jnp.where((rows == 0) & (n == 0), 1.0, mult)
    h = _scan_rows(a, mult * ig * xc, hprev_ref[...])
    hprev_ref[...] = h[t - 1:t, :]
    h_ref[...] = h[t - 1:t, :]
    o_ref[...] = (h * _gelu_tanh(_cat(gate_refs))).astype(BF16)


def _rglru_prompt(proj, p, bp, lp):
    nc = lp // RG_T
    full2 = lambda b, n: (0, 0)
    full3 = lambda b, n: (0, 0, 0)
    rows = lambda b, n: b * nc + n
    row = lambda a: a.reshape(1, RG_WIDTH)
    return pl.pallas_call(
        _rglru_prompt_kernel,
        grid=(bp, nc),
        in_specs=_pieces(RGX_OFF, RG_WIDTH, RG_T, rows) + _pieces(RGG_OFF, RG_WIDTH, RG_T, rows) + [
            pl.BlockSpec((CONV_W, RG_WIDTH), full2),
            pl.BlockSpec((1, RG_WIDTH), full2),
            pl.BlockSpec((RG_HEADS, RG_HEAD_DIM, RG_HEAD_DIM), full3),
            pl.BlockSpec((1, RG_WIDTH), full2),
            pl.BlockSpec((RG_HEADS, RG_HEAD_DIM, RG_HEAD_DIM), full3),
            pl.BlockSpec((1, RG_WIDTH), full2),
            pl.BlockSpec((1, RG_WIDTH), full2),
        ],
        out_specs=[
            pl.BlockSpec((RG_T, RG_WIDTH), lambda b, n: (b * nc + n, 0)),
            pl.BlockSpec((None, 1, RG_WIDTH), lambda b, n: (b, 0, 0)),
            pl.BlockSpec((None, CONV_W - 1, RG_WIDTH), lambda b, n: (b, 0, 0)),
        ],
        out_shape=[
            jax.ShapeDtypeStruct((bp * lp, RG_WIDTH), BF16),
            jax.ShapeDtypeStruct((bp, 1, RG_WIDTH), F32),
            jax.ShapeDtypeStruct((bp, CONV_W - 1, RG_WIDTH), F32),
        ],
        scratch_shapes=[
            pltpu.VMEM((RG_T + SUBLANES, RG_WIDTH), F32),
            pltpu.VMEM((SUBLANES, RG_WIDTH), F32),
            pltpu.VMEM((1, RG_WIDTH), F32),
        ],
        compiler_params=pltpu.CompilerParams(
            dimension_semantics=("parallel", "arbitrary"), vmem_limit_bytes=VMEM_LIMIT),
        name="rglru_prompt",
    )(*([proj] * (2 * N_RG)), p['rg_conv_w'], row(p['rg_conv_b']), p['rg_wa'].astype(BF16), row(p['rg_ba']),
      p['rg_wx'].astype(BF16), row(p['rg_bx']), row(p['rg_lambda']))


SB = 8


def _col_bcast(row):
    return jnp.broadcast_to(row, (LANES, LANES)).T


N_SAMPLE_IN = 22


def _sample_kernel(*refs, n_prev):
    (p_ref, hs_ref, rh_ref, rc_ref, ss_ref, sc_ref, lb_ref, hnw_ref,
     rcw_ref, rcb_ref, wa_ref, ba_ref, wx_ref, bx_ref, lam_ref,
     scw_ref, scb_ref, dtb_ref, aexp_ref, dexp_ref, snw_ref, expand_ref) = refs[:N_SAMPLE_IN]
    prev = refs[N_SAMPLE_IN:N_SAMPLE_IN + 2 * n_prev]
    (ohg_ref, org_ref, ossm_ref, hs_all, rh_out, rc_out, ss_all, sc_out,
     f_sc, kk_sc, q_sc, o_sc, adec_sc, xdt_sc, b_sc, c_sc, y_sc) = refs[N_SAMPLE_IN + 2 * n_prev:]
    if n_prev:
        for l in range(n_prev):
            hs_all[l] = prev[2 * l][...]
            ss_all[l] = prev[2 * l + 1][...]
        hs_out, ss_out = hs_all.at[n_prev], ss_all.at[n_prev]
    else:
        hs_out, ss_out = hs_all, ss_all

    x = p_ref[:, RGX_OFF:RGX_OFF + RG_WIDTH]
    xc = rcb_ref[...] + rcw_ref[CONV_W - 1:CONV_W, :] * x
    for k in range(CONV_W - 1):
        xc = xc + rcw_ref[k:k + 1, :] * rc_ref[k]
        rc_out[k] = x if k == CONV_W - 2 else rc_ref[k + 1]
    a, mult, ig = _rg_gates(xc, wa_ref, ba_ref, wx_ref, bx_ref, lam_ref)
    h = a * rh_ref[...] + mult * ig * xc
    rh_out[...] = h
    org_ref[...] = (h * _gelu_tanh(p_ref[:, RGG_OFF:RGG_OFF + RG_WIDTH])).astype(BF16)

    z = p_ref[:, SSZ_OFF:SSZ_OFF + SSM_WIDTH]
    xbc_raw = p_ref[:, SSX_OFF:SSX_OFF + SSM_CONV_DIM]
    dt_raw = p_ref[:, SSDT_OFF:SSDT_OFF + LANES]
    xbc = scb_ref[...] + scw_ref[CONV_W - 1:CONV_W, :] * xbc_raw
    for k in range(CONV_W - 1):
        xbc = xbc + scw_ref[k:k + 1, :] * sc_ref[k]
        sc_out[k] = xbc_raw if k == CONV_W - 2 else sc_ref[k + 1]
    xbc = _silu(xbc)
    xs = xbc[:, :SSM_WIDTH]
    gs = SSM_GROUPS * SSM_DSTATE
    b_sc[...] = xbc[:, SSM_WIDTH:SSM_WIDTH + gs]
    c_sc[...] = xbc[:, SSM_WIDTH + gs:]
    dt = _softplus(dt_raw + dtb_ref[...])
    parts = jnp.concatenate(_split3(dt), axis=0)
    r = jnp.dot(parts, expand_ref[...], preferred_element_type=F32)
    dt_exp = r[0:SB] + r[SB:2 * SB] + r[2 * SB:]
    xdt_sc[...] = xs * dt_exp
    adec_sc[...] = jnp.exp(dt_exp * aexp_ref[...])

    hg_tile = functools.partial(_hg_tile, p_ref)
    for hd in range(HG_HEADS):
        sl = slice(hd * LANES, (hd + 1) * LANES)
        lb = lb_ref[:, sl]
        fz = hg_tile(HGF_OFF, hd)
        f_sc[:, sl] = lb + (1.0 - lb) * jax.nn.sigmoid(fz)
        kk_sc[:, sl] = (1.0 - lb) * jax.nn.sigmoid(-fz)
        q_sc[:, sl] = _silu(hg_tile(HGQ_OFF, hd))

    for j in range(SB):
        row = slice(j, j + 1)
        for hd in range(HG_HEADS):
            sl = slice(hd * LANES, (hd + 1) * LANES)
            v_row = hg_tile(HGV_OFF, hd, row)
            s_new = _col_bcast(f_sc[row, sl]) * hs_ref[j, hd] + _col_bcast(kk_sc[row, sl]) * v_row
            hs_out[j, hd] = s_new
            q8 = jnp.broadcast_to(q_sc[row, sl], (SB, LANES)).astype(BF16)
            o_sc[row, sl] = jnp.dot(q8, s_new.astype(BF16), preferred_element_type=F32)[0:1]
        for hp in range(SSM_HEADS // 2):
            sl = slice(hp * LANES, (hp + 1) * LANES)
            g = (2 * hp) // HEADS_PER_GROUP
            gsl = slice(g * SSM_DSTATE, (g + 1) * SSM_DSTATE)
            s_old = ss_ref[j, 2 * hp:2 * hp + 2].reshape(LANES, SSM_DSTATE)
            s_new = _col_bcast(adec_sc[row, sl]) * s_old + _col_bcast(xdt_sc[row, sl]) * b_sc[row, gsl]
            ss_out[j, 2 * hp:2 * hp + 2] = s_new.reshape(2, SSM_HEADDIM, SSM_DSTATE)
            c8 = jnp.broadcast_to(c_sc[row, gsl], (SB, SSM_DSTATE)).astype(BF16)
            y_sc[row, sl] = lax.dot_general(c8, s_new.astype(BF16), _NT, preferred_element_type=F32)[0:1]

    for hd in range(HG_HEADS):
        sl = slice(hd * LANES, (hd + 1) * LANES)
        o = o_sc[:, sl]
        o = o * lax.rsqrt(jnp.mean(o * o, axis=-1, keepdims=True) + EPS)
        ohg_ref[:, sl] = (o * hnw_ref[:, sl] * _silu(hg_tile(HGG_OFF, hd))).astype(BF16)
    y = (y_sc[...] + dexp_ref[...] * xs) * _silu(z)
    gw = SSM_WIDTH // SSM_GROUPS
    for g in range(SSM_GROUPS):
        yg = y[:, g * gw:(g + 1) * gw]
        yg = yg * lax.rsqrt(jnp.mean(yg * yg, axis=-1, keepdims=True) + EPS)
        ossm_ref[:, g * gw:(g + 1) * gw] = (yg * snw_ref[:, g * gw:(g + 1) * gw]).astype(BF16)


def _sample_mix(proj, states, layer, p, lb, mp, prev=()):
    hg_s, rg_h, rg_c_t, ssm_s, ssm_c_t = states
    bs = hg_s.shape[1]
    rb = mp // SB
    const2 = lambda i: (0, 0)
    const3 = lambda i: (0, 0, 0)
    row = lambda a, w: a.reshape(1, w)
    dtb = jnp.pad(p['ssm_dt_bias'].reshape(1, SSM_HEADS), ((0, 0), (0, LANES - SSM_HEADS)))
    aexp = jnp.repeat(-jnp.exp(p['ssm_a_log']), SSM_HEADDIM).reshape(1, SSM_WIDTH)
    dexp = jnp.repeat(p['ssm_d'], SSM_HEADDIM).reshape(1, SSM_WIDTH)
    expand = _head_to_lanes()
    in_specs = [
        pl.BlockSpec((SB, D_IN_PAD), lambda i: (rb + i, 0)),
        pl.BlockSpec((None, SB, HG_HEADS, HG_DK, HG_DV), lambda i: (layer, i, 0, 0, 0)),
        pl.BlockSpec((None, SB, RG_WIDTH), lambda i: (layer, i, 0)),
        pl.BlockSpec((None, CONV_W - 1, SB, RG_WIDTH), lambda i: (layer, 0, i, 0)),
        pl.BlockSpec((None, SB, SSM_HEADS, SSM_HEADDIM, SSM_DSTATE), lambda i: (layer, i, 0, 0, 0)),
        pl.BlockSpec((None, CONV_W - 1, SB, SSM_CONV_DIM), lambda i: (layer, 0, i, 0)),
        pl.BlockSpec((1, HG_WIDTH), const2),
        pl.BlockSpec((1, HG_WIDTH), const2),
        pl.BlockSpec((CONV_W, RG_WIDTH), const2),
        pl.BlockSpec((1, RG_WIDTH), const2),
        pl.BlockSpec((RG_HEADS, RG_HEAD_DIM, RG_HEAD_DIM), const3),
        pl.BlockSpec((1, RG_WIDTH), const2),
        pl.BlockSpec((RG_HEADS, RG_HEAD_DIM, RG_HEAD_DIM), const3),
        pl.BlockSpec((1, RG_WIDTH), const2),
        pl.BlockSpec((1, RG_WIDTH), const2),
        pl.BlockSpec((CONV_W, SSM_CONV_DIM), const2),
        pl.BlockSpec((1, SSM_CONV_DIM), const2),
        pl.BlockSpec((1, LANES), const2),
        pl.BlockSpec((1, SSM_WIDTH), const2),
        pl.BlockSpec((1, SSM_WIDTH), const2),
        pl.BlockSpec((1, SSM_WIDTH), const2),
        pl.BlockSpec((LANES, SSM_WIDTH), const2),
    ]
    hg_blk, ssm_blk = (SB, HG_HEADS, HG_DK, HG_DV), (SB, SSM_HEADS, SSM_HEADDIM, SSM_DSTATE)
    n_prev = len(prev)
    for _ in prev:
        in_specs += [pl.BlockSpec(hg_blk, lambda i: (i, 0, 0, 0)), pl.BlockSpec(ssm_blk, lambda i: (i, 0, 0, 0))]
    stacked = lambda blk: (pl.BlockSpec((n_prev + 1,) + blk, lambda i: (0, i, 0, 0, 0)) if n_prev
                           else pl.BlockSpec(blk, lambda i: (i, 0, 0, 0)))
    lead = (n_prev + 1,) if n_prev else ()
    out_specs = [
        pl.BlockSpec((SB, HG_WIDTH), lambda i: (i, 0)),
        pl.BlockSpec((SB, RG_WIDTH), lambda i: (i, 0)),
        pl.BlockSpec((SB, SSM_WIDTH), lambda i: (i, 0)),
        stacked(hg_blk),
        pl.BlockSpec((SB, RG_WIDTH), lambda i: (i, 0)),
        pl.BlockSpec((CONV_W - 1, SB, RG_WIDTH), lambda i: (0, i, 0)),
        stacked(ssm_blk),
        pl.BlockSpec((CONV_W - 1, SB, SSM_CONV_DIM), lambda i: (0, i, 0)),
    ]
    out_shape = [
        jax.ShapeDtypeStruct((bs, HG_WIDTH), BF16),
        jax.ShapeDtypeStruct((bs, RG_WIDTH), BF16),
        jax.ShapeDtypeStruct((bs, SSM_WIDTH), BF16),
        jax.ShapeDtypeStruct(lead + hg_s.shape[1:], F32),
        jax.ShapeDtypeStruct(rg_h.shape[1:], F32),
        jax.ShapeDtypeStruct(rg_c_t.shape[1:], F32),
        jax.ShapeDtypeStruct(lead + ssm_s.shape[1:], F32),
        jax.ShapeDtypeStruct(ssm_c_t.shape[1:], F32),
    ]
    outs = pl.pallas_call(
        functools.partial(_sample_kernel, n_prev=n_prev),
        grid=(bs // SB,),
        in_specs=in_specs,
        out_specs=out_specs,
        out_shape=out_shape,
        scratch_shapes=[
            pltpu.VMEM((SB, HG_WIDTH), F32), pltpu.VMEM((SB, HG_WIDTH), F32), pltpu.VMEM((SB, HG_WIDTH), F32),
            pltpu.VMEM((SB, HG_WIDTH), F32),
            pltpu.VMEM((SB, SSM_WIDTH), F32), pltpu.VMEM((SB, SSM_WIDTH), F32),
            pltpu.VMEM((SB, SSM_GROUPS * SSM_DSTATE), F32), pltpu.VMEM((SB, SSM_GROUPS * SSM_DSTATE), F32),
            pltpu.VMEM((SB, SSM_WIDTH), F32),
        ],
        compiler_params=pltpu.CompilerParams(
            dimension_semantics=("parallel",), vmem_limit_bytes=VMEM_LIMIT),
        name="sample_mix",
    )(proj, hg_s, rg_h, rg_c_t, ssm_s, ssm_c_t,
      lb.reshape(1, HG_WIDTH), row(p['hg_norm_w'], HG_WIDTH),
      p['rg_conv_w'], row(p['rg_conv_b'], RG_WIDTH), p['rg_wa'].astype(BF16), row(p['rg_ba'], RG_WIDTH),
      p['rg_wx'].astype(BF16), row(p['rg_bx'], RG_WIDTH), row(p['rg_lambda'], RG_WIDTH),
      p['ssm_conv_w'], row(p['ssm_conv_b'], SSM_CONV_DIM), dtb, aexp, dexp, row(p['ssm_norm_w'], SSM_WIDTH),
      expand, *(a for pair in prev for a in pair))
    return tuple(outs[:3]), tuple(outs[3:])


def kernel(x_prompt, x_sample, state_hgrn, state_rglru, state_rglru_conv, state_ssm, state_ssm_conv, norm_g, ffn1_w_gate, ffn1_w_up, ffn1_w_down, ffn2_w_gate, ffn2_w_up, ffn2_w_down, w_in, w_out, hg_lb_logits, hg_norm_w, rg_conv_w, rg_conv_b, rg_wa, rg_ba, rg_wx, rg_bx, rg_lambda, ssm_conv_w, ssm_conv_b, ssm_dt_bias, ssm_a_log, ssm_d, ssm_norm_w):
    bp, lp, _ = x_prompt.shape
    bs, ls, _ = x_sample.shape
    mp = bp * lp
    ms = bs * ls
    assert (mp + ms) % IN_ROWS == 0 and D_FF % FF_TILE == 0 and D_IN_PAD % IN_TILE == 0

    lw = {
        'hg_norm_w': hg_norm_w, 'rg_conv_w': rg_conv_w, 'rg_conv_b': rg_conv_b, 'rg_wa': rg_wa,
        'rg_ba': rg_ba, 'rg_wx': rg_wx, 'rg_bx': rg_bx, 'rg_lambda': rg_lambda,
        'ssm_conv_w': ssm_conv_w, 'ssm_conv_b': ssm_conv_b, 'ssm_dt_bias': ssm_dt_bias,
        'ssm_a_log': ssm_a_log, 'ssm_d': ssm_d, 'ssm_norm_w': ssm_norm_w,
    }
    lb_cum = jnp.cumsum(jax.nn.softmax(hg_lb_logits.astype(F32), axis=0), axis=0)
    lower_bounds = lb_cum - lb_cum[:1]

    taps_first = lambda a: jnp.transpose(a, (0, 2, 1, 3))
    sample_init = (state_hgrn, state_rglru, taps_first(state_rglru_conv), state_ssm, taps_first(state_ssm_conv))

    w_in_b = jnp.pad(w_in.astype(BF16), ((0, 0), (0, 0), (0, D_IN_PAD - D_IN_PROJ)))
    x = (x_prompt.reshape(mp, D_MODEL), x_sample.reshape(ms, D_MODEL))
    new_p = ([], [], [], [], [])
    new_s = ([], [], [], [], [])
    for l in range(DEPTH):
        g = norm_g[l].reshape(6, 1, D_MODEL)
        p = {name: arr[l] for name, arr in lw.items()}
        x = _ffn(x, g[0], g[1], ffn1_w_gate, ffn1_w_up, ffn1_w_down, l, mp, ms, split_in=(l == 0))
        proj = _inproj(x, g[2], w_in_b, l)
        o_hg, hg_new = _hgrn_prompt(proj, p, lower_bounds[l], bp, lp)
        o_rg, rg_h_new, rg_c_new = _rglru_prompt(proj, p, bp, lp)
        o_ssm, ssm_new, ssm_c_new = _ssd_prompt(proj, p, bp, lp)
        st_p = (hg_new, rg_h_new.reshape(bp, RG_WIDTH), rg_c_new, ssm_new, ssm_c_new)
        last = l == DEPTH - 1
        prev = tuple(zip(new_s[0], new_s[3])) if last else ()
        o_s, st_s = _sample_mix(proj, sample_init, l, p, lower_bounds[l], mp, prev)
        x = _outproj(x, (o_hg, o_rg, o_ssm), o_s, g[3], w_out, l)
        x = _ffn(x, g[4], g[5], ffn2_w_gate, ffn2_w_up, ffn2_w_down, l, mp, ms, split_out=last)
        for acc, s in zip(new_p, st_p):
            acc.append(s)
        for acc, s in zip(new_s, st_s):
            acc.append(s)
    hg_p, rg_p, rgc_p, ssm_p, ssmc_p = (jnp.stack(a) for a in new_p)
    hg_s, ssm_s = new_s[0][-1], new_s[3][-1]
    rg_s, rgc_s, ssmc_s = (jnp.stack(new_s[k]) for k in (1, 2, 4))
    rgc_s, ssmc_s = taps_first(rgc_s), taps_first(ssmc_s)
    y_prompt = x[0].reshape(bp, lp, D_MODEL)
    y_sample = x[1].reshape(bs, ls, D_MODEL)
    return (y_prompt, y_sample, hg_p, hg_s, rg_p, rg_s, rgc_p, rgc_s, ssm_p, ssm_s, ssmc_p, ssmc_s)
```

```python
import functools
import math

import jax
import jax.numpy as jnp
import numpy as np
from jax import lax
from jax.experimental import pallas as pl
from jax.experimental.pallas import tpu as pltpu

F32 = jnp.float32
BF16 = jnp.bfloat16

D_MODEL = 2048
DEPTH = 2
EPS = 1e-6
CONV_W = 4
HG_HEADS = 4
HG_DK = 128
HG_DV = 128
HG_WIDTH = HG_HEADS * HG_DV
HG_CHUNK = 32
RG_HEADS = 6
RG_HEAD_DIM = 128
RG_WIDTH = RG_HEADS * RG_HEAD_DIM
RG_C = 8.0
SSM_HEADS = 12
SSM_HEADDIM = 64
SSM_WIDTH = SSM_HEADS * SSM_HEADDIM
SSM_GROUPS = 2
HEADS_PER_GROUP = SSM_HEADS // SSM_GROUPS
SSM_DSTATE = 128
SSM_CHUNK = 64
SSM_CONV_DIM = SSM_WIDTH + 2 * SSM_GROUPS * SSM_DSTATE
D_MIX = HG_WIDTH + RG_WIDTH + SSM_WIDTH
IN_SIZES = (HG_HEADS * HG_DK, HG_HEADS * HG_DK, HG_WIDTH, HG_WIDTH, RG_WIDTH, RG_WIDTH,
            SSM_WIDTH, SSM_CONV_DIM, SSM_HEADS)
D_IN_PROJ = sum(IN_SIZES)
D_FF = 5632

LANES = 128
D_IN_PAD = -(-D_IN_PROJ // LANES) * LANES
FFN_ROWS = 1040
FF_TILE = 256
IN_ROWS = 640
IN_TILE = 1920
VMEM_LIMIT = 56 * 1024 * 1024
FFN_VMEM_LIMIT = 62 * 1024 * 1024


def _rms(x, g):
    return x * lax.rsqrt(jnp.mean(x * x, axis=-1, keepdims=True) + EPS) * g


def _ffn_kernel(*refs, split_in, split_out, n_prompt_last):
    refs = list(refs)
    x_ref = refs.pop(0)
    xs_ref = refs.pop(0) if split_in else None
    gin_ref, gout_ref, wg_ref, wu_ref, wd_ref = (refs.pop(0) for _ in range(5))
    o_ref = refs.pop(0)
    ys_ref = refs.pop(0) if split_out else None
    xn_ref = refs.pop(0)
    i, j = pl.program_id(0), pl.program_id(1)
    last_i = pl.num_programs(0) - 1
    npl = n_prompt_last

    @pl.when(j == 0)
    def _():
        o_ref[...] = jnp.zeros_like(o_ref)
        if split_in:
            @pl.when(i < last_i)
            def _():
                xn_ref[...] = _rms(x_ref[...], gin_ref[...]).astype(BF16)

            @pl.when(i == last_i)
            def _():
                xn_ref[0:npl, :] = _rms(x_ref[0:npl, :], gin_ref[...]).astype(BF16)
                xn_ref[npl:, :] = _rms(xs_ref[...], gin_ref[...]).astype(BF16)
        else:
            xn_ref[...] = _rms(x_ref[...], gin_ref[...]).astype(BF16)

    xn = xn_ref[...]
    g = jnp.dot(xn, wg_ref[...].astype(BF16), preferred_element_type=F32)
    u = jnp.dot(xn, wu_ref[...].astype(BF16), preferred_element_type=F32)
    h = (g * jax.nn.sigmoid(g) * u).astype(BF16)
    o_ref[...] += jnp.dot(h, wd_ref[...].astype(BF16), preferred_element_type=F32)

    @pl.when(j == pl.num_programs(1) - 1)
    def _():
        if split_in:
            @pl.when(i < last_i)
            def _():
                o_ref[...] = x_ref[...] + 0.5 * _rms(o_ref[...], gout_ref[...])

            @pl.when(i == last_i)
            def _():
                o_ref[0:npl, :] = x_ref[0:npl, :] + 0.5 * _rms(o_ref[0:npl, :], gout_ref[...])
                o_ref[npl:, :] = xs_ref[...] + 0.5 * _rms(o_ref[npl:, :], gout_ref[...])
        else:
            o_ref[...] = x_ref[...] + 0.5 * _rms(o_ref[...], gout_ref[...])
        if split_out:
            @pl.when(i == last_i)
            def _():
                ys_ref[...] = o_ref[npl:, :]


def _ffn(x, g_in, g_out, wg, wu, wd, layer, mp, ms, split_in=False, split_out=False):
    m = mp + ms
    n_tiles = m // FFN_ROWS
    n_prompt_last = mp - (n_tiles - 1) * FFN_ROWS
    assert m % FFN_ROWS == 0 and n_prompt_last + ms == FFN_ROWS and n_prompt_last % 16 == 0
    rows = pl.BlockSpec((FFN_ROWS, D_MODEL), lambda i, j: (i, 0))
    sample = pl.BlockSpec((ms, D_MODEL), lambda i, j: (0, 0))
    vec = pl.BlockSpec((1, D_MODEL), lambda i, j: (0, 0))
    xs = tuple(x) if split_in else (x,)
    if split_out:
        out_specs = [rows, sample]
        out_shape = [jax.ShapeDtypeStruct((mp, D_MODEL), F32), jax.ShapeDtypeStruct((ms, D_MODEL), F32)]
    else:
        out_specs = rows
        out_shape = jax.ShapeDtypeStruct((m, D_MODEL), F32)
    return pl.pallas_call(
        functools.partial(_ffn_kernel, split_in=split_in, split_out=split_out, n_prompt_last=n_prompt_last),
        grid=(n_tiles, D_FF // FF_TILE),
        in_specs=[rows] + ([sample] if split_in else []) + [
            vec, vec,
            pl.BlockSpec((None, D_MODEL, FF_TILE), lambda i, j: (layer, 0, j)),
            pl.BlockSpec((None, D_MODEL, FF_TILE), lambda i, j: (layer, 0, j)),
            pl.BlockSpec((None, FF_TILE, D_MODEL), lambda i, j: (layer, j, 0)),
        ],
        out_specs=out_specs,
        out_shape=out_shape,
        scratch_shapes=[pltpu.VMEM((FFN_ROWS, D_MODEL), BF16)],
        compiler_params=pltpu.CompilerParams(
            dimension_semantics=("parallel", "arbitrary"), vmem_limit_bytes=FFN_VMEM_LIMIT),
        name="ffn",
    )(*xs, g_in, g_out, wg, wu, wd)


def _inproj_kernel(x_ref, g_ref, w_ref, o_ref, xn_ref):
    @pl.when(pl.program_id(1) == 0)
    def _():
        xn_ref[...] = _rms(x_ref[...], g_ref[...]).astype(BF16)

    o_ref[...] = jnp.dot(xn_ref[...], w_ref[...], preferred_element_type=F32)


def _inproj(x, g, w_in, layer):
    m = x.shape[0]
    return pl.pallas_call(
        _inproj_kernel,
        grid=(m // IN_ROWS, D_IN_PAD // IN_TILE),
        in_specs=[
            pl.BlockSpec((IN_ROWS, D_MODEL), lambda i, j: (i, 0)),
            pl.BlockSpec((1, D_MODEL), lambda i, j: (0, 0)),
            pl.BlockSpec((None, D_MODEL, IN_TILE), lambda i, j: (layer, 0, j)),
        ],
        out_specs=pl.BlockSpec((IN_ROWS, IN_TILE), lambda i, j: (i, j)),
        out_shape=jax.ShapeDtypeStruct((m, D_IN_PAD), F32),
        scratch_shapes=[pltpu.VMEM((IN_ROWS, D_MODEL), BF16)],
        compiler_params=pltpu.CompilerParams(
            dimension_semantics=("parallel", "arbitrary"), vmem_limit_bytes=VMEM_LIMIT),
        name="inproj",
    )(x, g, w_in)


OUT_TILE = 512


def _outproj_kernel(x_ref, php_ref, prg_ref, pss_ref, shg_ref, srg_ref, sss_ref, g_ref, w_ref, y_ref, wb_ref):
    i = pl.program_id(0)
    last = pl.num_programs(0) - 1
    r0, r1 = HG_WIDTH, HG_WIDTH + RG_WIDTH

    @pl.when(i == 0)
    def _():
        wb_ref[...] = w_ref[...].astype(BF16)

    def mixed(ohg, org, oss):
        m = jnp.dot(ohg, wb_ref[0:r0, :], preferred_element_type=F32)
        m += jnp.dot(org, wb_ref[r0:r1, :], preferred_element_type=F32)
        m += jnp.dot(oss, wb_ref[r1:, :], preferred_element_type=F32)
        return _rms(m, g_ref[...])

    @pl.when(i < last)
    def _():
        y_ref[...] = x_ref[...] + mixed(php_ref[...], prg_ref[...], pss_ref[...])

    @pl.when(i == last)
    def _():
        ns = shg_ref.shape[0]
        y_ref[0:ns, :] = x_ref[0:ns, :] + mixed(shg_ref[...], srg_ref[...], sss_ref[...])


def _outproj(x, o_prompt, o_sample, g, w_out, layer):
    mp, ms = o_prompt[0].shape[0], o_sample[0].shape[0]
    assert mp % OUT_TILE == 0 and ms <= OUT_TILE and x.shape[0] == mp + ms
    n_p = mp // OUT_TILE
    widths = (HG_WIDTH, RG_WIDTH, SSM_WIDTH)
    return pl.pallas_call(
        _outproj_kernel,
        grid=(n_p + 1,),
        in_specs=[pl.BlockSpec((OUT_TILE, D_MODEL), lambda i: (i, 0))]
        + [pl.BlockSpec((OUT_TILE, w), lambda i: (jnp.minimum(i, n_p - 1), 0)) for w in widths]
        + [pl.BlockSpec((ms, w), lambda i: (0, 0)) for w in widths]
        + [pl.BlockSpec((1, D_MODEL), lambda i: (0, 0)),
           pl.BlockSpec((None, D_MIX, D_MODEL), lambda i: (layer, 0, 0), pipeline_mode=pl.Buffered(1))],
        out_specs=pl.BlockSpec((OUT_TILE, D_MODEL), lambda i: (i, 0)),
        out_shape=jax.ShapeDtypeStruct((mp + ms, D_MODEL), F32),
        scratch_shapes=[pltpu.VMEM((D_MIX, D_MODEL), BF16)],
        compiler_params=pltpu.CompilerParams(
            dimension_semantics=("arbitrary",), vmem_limit_bytes=VMEM_LIMIT),
        name="outproj",
    )(x, *o_prompt, *o_sample, g, w_out)


_COL = np.cumsum((0,) + IN_SIZES)
HGQ_OFF, HGF_OFF, HGV_OFF, HGG_OFF, RGX_OFF, RGG_OFF, SSZ_OFF, SSX_OFF, SSDT_OFF = (int(c) for c in _COL[:9])
PIECE = 256
assert all(off % PIECE == 0 for off in (RGX_OFF, RGG_OFF, SSZ_OFF, SSX_OFF)) and SSDT_OFF % LANES == 0
assert all(w % PIECE == 0 for w in (RG_WIDTH, SSM_WIDTH, SSM_CONV_DIM))


def _pieces(off, width, rows, row_map):
    return [pl.BlockSpec((rows, PIECE), functools.partial(lambda k, *g: (row_map(*g), k), off // PIECE + k))
            for k in range(width // PIECE)]


def _cat(refs, rows=slice(None)):
    return jnp.concatenate([r[rows, :] for r in refs], axis=1)


SUBLANES = 8
SSD_T = 256
NEG_BIG = -1e30


def _split3(x):
    hi = x.astype(BF16)
    r = x - hi.astype(F32)
    mid = r.astype(BF16)
    lo = (r - mid.astype(F32)).astype(BF16)
    return hi, mid, lo


def _cumsum_rows(tri, x):
    w = x.shape[1]
    parts = jnp.concatenate(_split3(x), axis=1)
    r = jnp.dot(tri, parts, preferred_element_type=F32)
    return r[:, :w] + r[:, w:2 * w] + r[:, 2 * w:]


def _silu(x):
    return x * jax.nn.sigmoid(x)


def _neg_expm1(x, exp_x):
    series = -x * (1.0 + x * (1 / 2 + x * (1 / 6 + x * (1 / 24 + x * (1 / 120)))))
    return jnp.where(x > -1 / 16, series, 1.0 - exp_x)


def _softplus(x):
    return jnp.maximum(x, 0.0) + jnp.log(1.0 + jnp.exp(-jnp.abs(x)))


def _conv4(xp_ref, tail_ref, x, w_ref, b_ref):
    t = x.shape[0]
    xp_ref[0:SUBLANES, :] = tail_ref[...]
    xp_ref[SUBLANES:, :] = x
    tail_ref[...] = x[t - SUBLANES:, :]
    y = b_ref[...] + w_ref[CONV_W - 1:CONV_W, :] * x
    for k in range(CONV_W - 1):
        off = SUBLANES - (CONV_W - 1) + k
        y = y + w_ref[k:k + 1, :] * xp_ref[off:off + t, :]
    return y


N_Z, N_XBC = SSM_WIDTH // PIECE, SSM_CONV_DIM // PIECE


SSD_N_IN = N_Z + N_XBC + 9
SSD_N_OUT, SSD_N_SCRATCH = 3, 4


def _ssd_prompt_body(ins, outs, scratch):
    z_refs, xbc_refs = ins[:N_Z], ins[N_Z:N_Z + N_XBC]
    dt_ref, tri_ref, expand_ref, cw_ref, cb_ref, dtb_ref, alog_ref, dexp_ref, nw_ref = ins[N_Z + N_XBC:]
    o_ref, _, cst_ref = outs
    xp_ref, tail_ref, s_ref, y_ref = scratch
    t = SSD_T
    z = _cat(z_refs)
    xbc_raw = _cat(xbc_refs)
    dt_raw = dt_ref[...]

    xbc = _silu(_conv4(xp_ref, tail_ref, xbc_raw, cw_ref, cb_ref))
    cst_ref[...] = _cat(xbc_refs, slice(t - (CONV_W - 1), t))
    xs = xbc[:, :SSM_WIDTH]
    gs = SSM_GROUPS * SSM_DSTATE
    bm = xbc[:, SSM_WIDTH:SSM_WIDTH + gs].astype(BF16)
    cm = xbc[:, SSM_WIDTH + gs:].astype(BF16)

    dt = _softplus(dt_raw + dtb_ref[...])
    a = dt * -jnp.exp(alog_ref[...])
    cs = _cumsum_rows(tri_ref[...], a)
    cs_t = cs.T
    cs_last = cs[t - 1:t, :]
    e_last = jnp.exp(cs_last)
    row = lax.broadcasted_iota(jnp.int32, (t, t), 0)
    col = lax.broadcasted_iota(jnp.int32, (t, t), 1)
    causal = row >= col

    def per_lane(v):
        parts = jnp.concatenate(_split3(v), axis=0)
        r = jnp.dot(parts, expand_ref[...], preferred_element_type=F32)
        return r[0:t] + r[t:2 * t] + r[2 * t:]

    dt_w = per_lane(dt)
    cs_w = per_lane(cs)
    e_cs_w = jnp.exp(cs_w)
    xdt = xs * dt_w
    xdt_b = xdt.astype(BF16)
    w_b = (xdt * jnp.exp(cs_w[t - 1:t, :] - cs_w)).astype(BF16)
    pair_lane = lax.broadcasted_iota(jnp.int32, (t, LANES), 1) < SSM_HEADDIM
    gw = SSM_WIDTH // SSM_GROUPS

    for g in range(SSM_GROUPS):
        heads = range(g * HEADS_PER_GROUP, (g + 1) * HEADS_PER_GROUP)
        bg = bm[:, g * SSM_DSTATE:(g + 1) * SSM_DSTATE]
        cg = cm[:, g * SSM_DSTATE:(g + 1) * SSM_DSTATE]
        cb = lax.dot_general(cg, bg, _NT, preferred_element_type=F32)
        s_prev = s_ref[heads.start:heads.stop].reshape(gw, SSM_DSTATE)
        y_in = lax.dot_general(cg, s_prev.astype(BF16), _NT, preferred_element_type=F32)
        st = lax.dot_general(w_b[:, g * gw:(g + 1) * gw], bg, _TN, preferred_element_type=F32)
        decay = jnp.concatenate([jnp.broadcast_to(e_last[:, h:h + 1], (SSM_HEADDIM, SSM_DSTATE)) for h in heads],
                                axis=0)
        s_ref[heads.start:heads.stop] = (decay * s_prev + st).reshape(HEADS_PER_GROUP, SSM_HEADDIM, SSM_DSTATE)

        for k in range(HEADS_PER_GROUP // 2):
            lanes = slice(g * gw + k * LANES, g * gw + (k + 1) * LANES)
            res = []
            for h in (heads.start + 2 * k, heads.start + 2 * k + 1):
                seg = jnp.where(causal, cs[:, h:h + 1] - cs_t[h:h + 1, :], NEG_BIG)
                scores = (cb * jnp.exp(seg)).astype(BF16)
                res.append(jnp.dot(scores, xdt_b[:, lanes], preferred_element_type=F32))
            y_ref[:, lanes] = (jnp.where(pair_lane, res[0], res[1])
                               + e_cs_w[:, lanes] * y_in[:, k * LANES:(k + 1) * LANES])

    y = (y_ref[...] + dexp_ref[...] * xs) * _silu(z)
    gw = SSM_WIDTH // SSM_GROUPS
    for g in range(SSM_GROUPS):
        yg = y[:, g * gw:(g + 1) * gw]
        yg = yg * lax.rsqrt(jnp.mean(yg * yg, axis=-1, keepdims=True) + EPS)
        o_ref[:, g * gw:(g + 1) * gw] = (yg * nw_ref[:, g * gw:(g + 1) * gw]).astype(BF16)


def _tri(t):
    return jnp.tril(jnp.ones((t, t), F32)).astype(BF16)


def _head_to_lanes():
    return jnp.asarray(np.arange(LANES)[:, None] == (np.arange(SSM_WIDTH)[None, :] // SSM_HEADDIM), BF16)


def _ssd_prompt_parts(proj, p, bp, lp):
    nc = lp // SSD_T
    full = lambda b, n: (0, 0)
    rows = lambda b, n: b * nc + n
    dtb = jnp.pad(p['ssm_dt_bias'].reshape(1, SSM_HEADS), ((0, 0), (0, LANES - SSM_HEADS)))
    alog = jnp.pad(p['ssm_a_log'].reshape(1, SSM_HEADS), ((0, 0), (0, LANES - SSM_HEADS)))
    dexp = jnp.repeat(p['ssm_d'], SSM_HEADDIM).reshape(1, SSM_WIDTH)
    return dict(
        in_specs=_pieces(SSZ_OFF, SSM_WIDTH, SSD_T, rows) + _pieces(SSX_OFF, SSM_CONV_DIM, SSD_T, rows) + [
            pl.BlockSpec((SSD_T, LANES), lambda b, n: (rows(b, n), SSDT_OFF // LANES)),
            pl.BlockSpec((SSD_T, SSD_T), full),
            pl.BlockSpec((LANES, SSM_WIDTH), full),
            pl.BlockSpec((CONV_W, SSM_CONV_DIM), full),
            pl.BlockSpec((1, SSM_CONV_DIM), full),
            pl.BlockSpec((1, LANES), full),
            pl.BlockSpec((1, LANES), full),
            pl.BlockSpec((1, SSM_WIDTH), full),
            pl.BlockSpec((1, SSM_WIDTH), full),
        ],
        out_specs=[
            pl.BlockSpec((SSD_T, SSM_WIDTH), lambda b, n: (b * nc + n, 0)),
            pl.BlockSpec((None, SSM_HEADS, SSM_HEADDIM, SSM_DSTATE), lambda b, n: (b, 0, 0, 0)),
            pl.BlockSpec((None, CONV_W - 1, SSM_CONV_DIM), lambda b, n: (b, 0, 0)),
        ],
        out_shape=[
            jax.ShapeDtypeStruct((bp * lp, SSM_WIDTH), BF16),
            jax.ShapeDtypeStruct((bp, SSM_HEADS, SSM_HEADDIM, SSM_DSTATE), F32),
            jax.ShapeDtypeStruct((bp, CONV_W - 1, SSM_CONV_DIM), F32),
        ],
        scratch_shapes=[
            pltpu.VMEM((SSD_T + SUBLANES, SSM_CONV_DIM), F32),
            pltpu.VMEM((SUBLANES, SSM_CONV_DIM), F32),
            pltpu.VMEM((SSM_HEADS, SSM_HEADDIM, SSM_DSTATE), F32),
            pltpu.VMEM((SSD_T, SSM_WIDTH), F32),
        ],
        args=[proj] * (N_Z + N_XBC + 1) + [
            _tri(SSD_T), _head_to_lanes(), p['ssm_conv_w'], p['ssm_conv_b'].reshape(1, SSM_CONV_DIM), dtb, alog,
            dexp, p['ssm_norm_w'].reshape(1, SSM_WIDTH)],
    )


HG_T = 256

_NT = (((1,), (1,)), ((), ()))
_TN = (((0,), (0,)), ((), ()))


HG_COLS = 4 * HG_WIDTH
assert (HGQ_OFF, HGF_OFF, HGV_OFF, HGG_OFF) == (0, HG_WIDTH, 2 * HG_WIDTH, 3 * HG_WIDTH)


def _hg_tile(p_ref, off, head, rows=slice(None)):
    return p_ref[rows, off + head * LANES:off + (head + 1) * LANES]


HG_N_IN, HG_N_OUT, HG_N_SCRATCH = 4, 2, 1


def _hgrn_prompt_body(ins, outs, scratch):
    hg_ref, lb_ref, nw_ref, tri_ref = ins
    o_ref, _ = outs
    s_ref, = scratch
    t, c = HG_T, HG_CHUNK
    nch = t // c

    row = lax.broadcasted_iota(jnp.int32, (t, t), 0)
    col = lax.broadcasted_iota(jnp.int32, (t, t), 1)
    keep = (row >= col) & (row // c == col // c)
    tri = tri_ref[...]

    v, o_intra, kd, qe, decay = [], [], [], [], []
    for h in range(HG_HEADS):
        sl = slice(h * LANES, (h + 1) * LANES)
        lb = lb_ref[:, sl]
        fz = _hg_tile(hg_ref, HGF_OFF, h)
        logf = jnp.log(lb + (1.0 - lb) * jax.nn.sigmoid(fz))
        kk3 = ((1.0 - lb) * jax.nn.sigmoid(-fz)).reshape(nch, c, HG_DK)
        qh3 = _silu(_hg_tile(hg_ref, HGQ_OFF, h)).reshape(nch, c, HG_DK)
        vh = _hg_tile(hg_ref, HGV_OFF, h).astype(BF16)
        b3 = _cumsum_rows(tri, logf).reshape(nch, c, HG_DK)
        b_mid = b3[:, c // 2:c // 2 + 1, :]
        b_last = b3[:, c - 1:c, :]
        q_in = (qh3 * jnp.exp(b3 - b_mid)).reshape(t, HG_DK).astype(BF16)
        k_in = (kk3 * jnp.exp(b_mid - b3)).reshape(t, HG_DK).astype(BF16)
        a = jnp.where(keep, lax.dot_general(q_in, k_in, _NT, preferred_element_type=F32), 0.0)
        v.append(vh)
        o_intra.append(jnp.dot(a.astype(BF16), vh, preferred_element_type=F32))
        kd.append((kk3 * jnp.exp(b_last - b3)).astype(BF16))
        qe.append((qh3 * jnp.exp(b3)).astype(BF16))
        decay.append(jnp.exp(b_last))

    outs = [[] for _ in range(HG_HEADS)]
    for ci in range(nch):
        rows = slice(ci * c, (ci + 1) * c)
        for h in range(HG_HEADS):
            s_t = s_ref[h]
            o_inter = lax.dot_general(qe[h][ci], s_t.astype(BF16), _NT, preferred_element_type=F32)
            outs[h].append(o_intra[h][rows, :] + o_inter)
            ds_t = lax.dot_general(v[h][rows, :], kd[h][ci], _TN, preferred_element_type=F32)
            s_ref[h] = s_t * decay[h][ci] + ds_t

    for h in range(HG_HEADS):
        sl = slice(h * LANES, (h + 1) * LANES)
        o = jnp.concatenate(outs[h], axis=0)
        o = o * lax.rsqrt(jnp.mean(o * o, axis=-1, keepdims=True) + EPS)
        g = _hg_tile(hg_ref, HGG_OFF, h)
        o_ref[:, sl] = (o * nw_ref[:, sl] * _silu(g)).astype(BF16)


def _tri_chunks(t, c):
    r = np.arange(t)
    return jnp.asarray((r[:, None] >= r[None, :]) & (r[:, None] // c == r[None, :] // c), BF16)


def _hgrn_prompt_parts(proj, p, lb, bp, lp):
    nc = lp // HG_T
    const = lambda b, n: (0, 0)
    return dict(
        in_specs=[
            pl.BlockSpec((HG_T, HG_COLS), lambda b, n: (b * nc + n, 0)),
            pl.BlockSpec((1, HG_WIDTH), const),
            pl.BlockSpec((1, HG_WIDTH), const),
            pl.BlockSpec((HG_T, HG_T), const),
        ],
        out_specs=[
            pl.BlockSpec((HG_T, HG_WIDTH), lambda b, n: (b * nc + n, 0)),
            pl.BlockSpec((None, HG_HEADS, HG_DK, HG_DV), lambda b, n: (b, 0, 0, 0)),
        ],
        out_shape=[
            jax.ShapeDtypeStruct((bp * lp, HG_WIDTH), BF16),
            jax.ShapeDtypeStruct((bp, HG_HEADS, HG_DK, HG_DV), F32),
        ],
        scratch_shapes=[pltpu.VMEM((HG_HEADS, HG_DV, HG_DK), F32)],
        args=[proj, lb.reshape(1, HG_WIDTH), p['hg_norm_w'].reshape(1, HG_WIDTH), _tri_chunks(HG_T, HG_CHUNK)],
    )


RG_T = 256


def _scan_rows(a, u, h_in):
    t, w = a.shape
    rows = lax.broadcasted_iota(jnp.int32, a.shape, 0) % SUBLANES
    s = 1
    while s < SUBLANES:
        keep = rows >= s
        a_sh = jnp.where(keep, pltpu.roll(a, s, axis=0), 1.0)
        u_sh = jnp.where(keep, pltpu.roll(u, s, axis=0), 0.0)
        u = u + a * u_sh
        a = a * a_sh
        s *= 2
    hs = []
    for g in range(t // SUBLANES):
        sl = slice(g * SUBLANES, (g + 1) * SUBLANES)
        h = u[sl, :] + a[sl, :] * h_in
        hs.append(h)
        h_in = h[SUBLANES - 1:, :]
    return jnp.concatenate(hs, axis=0)


def _rg_gates(xc, wa_ref, ba_ref, wx_ref, bx_ref, lam_ref):
    xb = xc.astype(BF16)
    ra, ri = [], []
    for h in range(RG_HEADS):
        sl = slice(h * RG_HEAD_DIM, (h + 1) * RG_HEAD_DIM)
        ra.append(jnp.dot(xb[:, sl], wa_ref[h], preferred_element_type=F32))
        ri.append(jnp.dot(xb[:, sl], wx_ref[h], preferred_element_type=F32))
    r = jax.nn.sigmoid(jnp.concatenate(ra, axis=1) + ba_ref[...])
    ig = jax.nn.sigmoid(jnp.concatenate(ri, axis=1) + bx_ref[...])
    log_a = -RG_C * r * _softplus(-lam_ref[...])
    a = jnp.exp(log_a)
    mult = jnp.sqrt(_neg_expm1(2.0 * log_a, a * a))
    return a, mult, ig


def _gelu_tanh(x):
    return 0.5 * x * (1.0 + jnp.tanh(math.sqrt(2.0 / math.pi) * (x + 0.044715 * (x * x * x))))


N_RG = RG_WIDTH // PIECE


RG_N_IN, RG_N_OUT, RG_N_SCRATCH = 2 * N_RG + 7, 3, 3


def _rglru_prompt_body(ins, outs, scratch):
    x_refs, gate_refs = ins[:N_RG], ins[N_RG:2 * N_RG]
    cw_ref, cb_ref, wa_ref, ba_ref, wx_ref, bx_ref, lam_ref = ins[2 * N_RG:]
    o_ref, h_ref, cst_ref = outs
    xp_ref, tail_ref, hprev_ref = scratch
    n = pl.program_id(1)
    t = RG_T

    x = _cat(x_refs)
    xc = _conv4(xp_ref, tail_ref, x, cw_ref, cb_ref)
    cst_ref[...] = _cat(x_refs, slice(t - (CONV_W - 1), t))
    a, mult, ig = _rg_gates(xc, wa_ref, ba_ref, wx_ref, bx_ref, lam_ref)
    rows = lax.broadcasted_iota(jnp.int32, a.shape, 0)
    mult = jnp.where((rows == 0) & (n == 0), 1.0, mult)
    h = _scan_rows(a, mult * ig * xc, hprev_ref[...])
    hprev_ref[...] = h[t - 1:t, :]
    h_ref[...] = h[t - 1:t, :]
    o_ref[...] = (h * _gelu_tanh(_cat(gate_refs))).astype(BF16)


def _rglru_prompt_parts(proj, p, bp, lp):
    nc = lp // RG_T
    full2 = lambda b, n: (0, 0)
    full3 = lambda b, n: (0, 0, 0)
    rows = lambda b, n: b * nc + n
    row = lambda a: a.reshape(1, RG_WIDTH)
    return dict(
        in_specs=_pieces(RGX_OFF, RG_WIDTH, RG_T, rows) + _pieces(RGG_OFF, RG_WIDTH, RG_T, rows) + [
            pl.BlockSpec((CONV_W, RG_WIDTH), full2),
            pl.BlockSpec((1, RG_WIDTH), full2),
            pl.BlockSpec((RG_HEADS, RG_HEAD_DIM, RG_HEAD_DIM), full3),
            pl.BlockSpec((1, RG_WIDTH), full2),
            pl.BlockSpec((RG_HEADS, RG_HEAD_DIM, RG_HEAD_DIM), full3),
            pl.BlockSpec((1, RG_WIDTH), full2),
            pl.BlockSpec((1, RG_WIDTH), full2),
        ],
        out_specs=[
            pl.BlockSpec((RG_T, RG_WIDTH), lambda b, n: (b * nc + n, 0)),
            pl.BlockSpec((None, 1, RG_WIDTH), lambda b, n: (b, 0, 0)),
            pl.BlockSpec((None, CONV_W - 1, RG_WIDTH), lambda b, n: (b, 0, 0)),
        ],
        out_shape=[
            jax.ShapeDtypeStruct((bp * lp, RG_WIDTH), BF16),
            jax.ShapeDtypeStruct((bp, 1, RG_WIDTH), F32),
            jax.ShapeDtypeStruct((bp, CONV_W - 1, RG_WIDTH), F32),
        ],
        scratch_shapes=[
            pltpu.VMEM((RG_T + SUBLANES, RG_WIDTH), F32),
            pltpu.VMEM((SUBLANES, RG_WIDTH), F32),
            pltpu.VMEM((1, RG_WIDTH), F32),
        ],
        args=[proj] * (2 * N_RG) + [
            p['rg_conv_w'], row(p['rg_conv_b']), p['rg_wa'].astype(BF16), row(p['rg_ba']),
            p['rg_wx'].astype(BF16), row(p['rg_bx']), row(p['rg_lambda'])],
    )


PROMPT_T = 256
assert HG_T == RG_T == SSD_T == PROMPT_T


def _prompt_mix_kernel(*refs):
    counts = ((HG_N_IN, HG_N_OUT, HG_N_SCRATCH), (RG_N_IN, RG_N_OUT, RG_N_SCRATCH),
              (SSD_N_IN, SSD_N_OUT, SSD_N_SCRATCH))
    refs = list(refs)
    ins = [[refs.pop(0) for _ in range(c[0])] for c in counts]
    outs = [[refs.pop(0) for _ in range(c[1])] for c in counts]
    scratch = [[refs.pop(0) for _ in range(c[2])] for c in counts]
    assert not refs
    n = pl.program_id(1)
    hg_s, = scratch[0]
    _, rg_tail, rg_h = scratch[1]
    _, ssd_tail, ssd_s, _ = scratch[2]

    @pl.when(n == 0)
    def _():
        for r in (hg_s, rg_tail, rg_h, ssd_tail, ssd_s):
            r[...] = jnp.zeros_like(r)

    _hgrn_prompt_body(ins[0], outs[0], scratch[0])
    _rglru_prompt_body(ins[1], outs[1], scratch[1])
    _ssd_prompt_body(ins[2], outs[2], scratch[2])

    @pl.when(n == pl.num_programs(1) - 1)
    def _():
        for h in range(HG_HEADS):
            outs[0][1][h] = hg_s[h].T
        outs[2][1][...] = ssd_s[...]


def _prompt_mix(proj, p, lb, bp, lp):
    parts = (_hgrn_prompt_parts(proj, p, lb, bp, lp), _rglru_prompt_parts(proj, p, bp, lp),
             _ssd_prompt_parts(proj, p, bp, lp))
    cat = lambda key: [v for part in parts for v in part[key]]
    return pl.pallas_call(
        _prompt_mix_kernel,
        grid=(bp, lp // PROMPT_T),
        in_specs=cat('in_specs'),
        out_specs=cat('out_specs'),
        out_shape=cat('out_shape'),
        scratch_shapes=cat('scratch_shapes'),
        compiler_params=pltpu.CompilerParams(
            dimension_semantics=("parallel", "arbitrary"), vmem_limit_bytes=VMEM_LIMIT),
        name="prompt_mix",
    )(*cat('args'))


SB = 8


def _col_bcast(row):
    return jnp.broadcast_to(row, (LANES, LANES)).T


N_SAMPLE_IN = 22


def _sample_kernel(*refs, n_prev):
    (p_ref, hs_ref, rh_ref, rc_ref, ss_ref, sc_ref, lb_ref, hnw_ref,
     rcw_ref, rcb_ref, wa_ref, ba_ref, wx_ref, bx_ref, lam_ref,
     scw_ref, scb_ref, dtb_ref, aexp_ref, dexp_ref, snw_ref, expand_ref) = refs[:N_SAMPLE_IN]
    prev = refs[N_SAMPLE_IN:N_SAMPLE_IN + 2 * n_prev]
    (ohg_ref, org_ref, ossm_ref, hs_all, rh_out, rc_out, ss_all, sc_out,
     f_sc, kk_sc, q_sc, o_sc, adec_sc, xdt_sc, b_sc, c_sc, y_sc) = refs[N_SAMPLE_IN + 2 * n_prev:]
    if n_prev:
        for l in range(n_prev):
            hs_all[l] = prev[2 * l][...]
            ss_all[l] = prev[2 * l + 1][...]
        hs_out, ss_out = hs_all.at[n_prev], ss_all.at[n_prev]
    else:
        hs_out, ss_out = hs_all, ss_all

    x = p_ref[:, RGX_OFF:RGX_OFF + RG_WIDTH]
    xc = rcb_ref[...] + rcw_ref[CONV_W - 1:CONV_W, :] * x
    for k in range(CONV_W - 1):
        xc = xc + rcw_ref[k:k + 1, :] * rc_ref[k]
        rc_out[k] = x if k == CONV_W - 2 else rc_ref[k + 1]
    a, mult, ig = _rg_gates(xc, wa_ref, ba_ref, wx_ref, bx_ref, lam_ref)
    h = a * rh_ref[...] + mult * ig * xc
    rh_out[...] = h
    org_ref[...] = (h * _gelu_tanh(p_ref[:, RGG_OFF:RGG_OFF + RG_WIDTH])).astype(BF16)

    z = p_ref[:, SSZ_OFF:SSZ_OFF + SSM_WIDTH]
    xbc_raw = p_ref[:, SSX_OFF:SSX_OFF + SSM_CONV_DIM]
    dt_raw = p_ref[:, SSDT_OFF:SSDT_OFF + LANES]
    xbc = scb_ref[...] + scw_ref[CONV_W - 1:CONV_W, :] * xbc_raw
    for k in range(CONV_W - 1):
        xbc = xbc + scw_ref[k:k + 1, :] * sc_ref[k]
        sc_out[k] = xbc_raw if k == CONV_W - 2 else sc_ref[k + 1]
    xbc = _silu(xbc)
    xs = xbc[:, :SSM_WIDTH]
    gs = SSM_GROUPS * SSM_DSTATE
    b_sc[...] = xbc[:, SSM_WIDTH:SSM_WIDTH + gs]
    c_sc[...] = xbc[:, SSM_WIDTH + gs:]
    dt = _softplus(dt_raw + dtb_ref[...])
    parts = jnp.concatenate(_split3(dt), axis=0)
    r = jnp.dot(parts, expand_ref[...], preferred_element_type=F32)
    dt_exp = r[0:SB] + r[SB:2 * SB] + r[2 * SB:]
    xdt_sc[...] = xs * dt_exp
    adec_sc[...] = jnp.exp(dt_exp * aexp_ref[...])

    hg_tile = functools.partial(_hg_tile, p_ref)
    for hd in range(HG_HEADS):
        sl = slice(hd * LANES, (hd + 1) * LANES)
        lb = lb_ref[:, sl]
        fz = hg_tile(HGF_OFF, hd)
        f_sc[:, sl] = lb + (1.0 - lb) * jax.nn.sigmoid(fz)
        kk_sc[:, sl] = (1.0 - lb) * jax.nn.sigmoid(-fz)
        q_sc[:, sl] = _silu(hg_tile(HGQ_OFF, hd))

    for j in range(SB):
        row = slice(j, j + 1)
        for hd in range(HG_HEADS):
            sl = slice(hd * LANES, (hd + 1) * LANES)
            v_row = hg_tile(HGV_OFF, hd, row)
            s_new = _col_bcast(f_sc[row, sl]) * hs_ref[j, hd] + _col_bcast(kk_sc[row, sl]) * v_row
            hs_out[j, hd] = s_new
            q8 = jnp.broadcast_to(q_sc[row, sl], (SB, LANES)).astype(BF16)
            o_sc[row, sl] = jnp.dot(q8, s_new.astype(BF16), preferred_element_type=F32)[0:1]
        for hp in range(SSM_HEADS // 2):
            sl = slice(hp * LANES, (hp + 1) * LANES)
            g = (2 * hp) // HEADS_PER_GROUP
            gsl = slice(g * SSM_DSTATE, (g + 1) * SSM_DSTATE)
            s_old = ss_ref[j, 2 * hp:2 * hp + 2].reshape(LANES, SSM_DSTATE)
            s_new = _col_bcast(adec_sc[row, sl]) * s_old + _col_bcast(xdt_sc[row, sl]) * b_sc[row, gsl]
            ss_out[j, 2 * hp:2 * hp + 2] = s_new.reshape(2, SSM_HEADDIM, SSM_DSTATE)
            c8 = jnp.broadcast_to(c_sc[row, gsl], (SB, SSM_DSTATE)).astype(BF16)
            y_sc[row, sl] = lax.dot_general(c8, s_new.astype(BF16), _NT, preferred_element_type=F32)[0:1]

    for hd in range(HG_HEADS):
        sl = slice(hd * LANES, (hd + 1) * LANES)
        o = o_sc[:, sl]
        o = o * lax.rsqrt(jnp.mean(o * o, axis=-1, keepdims=True) + EPS)
        ohg_ref[:, sl] = (o * hnw_ref[:, sl] * _silu(hg_tile(HGG_OFF, hd))).astype(BF16)
    y = (y_sc[...] + dexp_ref[...] * xs) * _silu(z)
    gw = SSM_WIDTH // SSM_GROUPS
    for g in range(SSM_GROUPS):
        yg = y[:, g * gw:(g + 1) * gw]
        yg = yg * lax.rsqrt(jnp.mean(yg * yg, axis=-1, keepdims=True) + EPS)
        ossm_ref[:, g * gw:(g + 1) * gw] = (yg * snw_ref[:, g * gw:(g + 1) * gw]).astype(BF16)


def _sample_mix(proj, states, layer, p, lb, mp, prev=()):
    hg_s, rg_h, rg_c_t, ssm_s, ssm_c_t = states
    bs = hg_s.shape[1]
    rb = mp // SB
    const2 = lambda i: (0, 0)
    const3 = lambda i: (0, 0, 0)
    row = lambda a, w: a.reshape(1, w)
    dtb = jnp.pad(p['ssm_dt_bias'].reshape(1, SSM_HEADS), ((0, 0), (0, LANES - SSM_HEADS)))
    aexp = jnp.repeat(-jnp.exp(p['ssm_a_log']), SSM_HEADDIM).reshape(1, SSM_WIDTH)
    dexp = jnp.repeat(p['ssm_d'], SSM_HEADDIM).reshape(1, SSM_WIDTH)
    expand = _head_to_lanes()
    in_specs = [
        pl.BlockSpec((SB, D_IN_PAD), lambda i: (rb + i, 0)),
        pl.BlockSpec((None, SB, HG_HEADS, HG_DK, HG_DV), lambda i: (layer, i, 0, 0, 0)),
        pl.BlockSpec((None, SB, RG_WIDTH), lambda i: (layer, i, 0)),
        pl.BlockSpec((None, CONV_W - 1, SB, RG_WIDTH), lambda i: (layer, 0, i, 0)),
        pl.BlockSpec((None, SB, SSM_HEADS, SSM_HEADDIM, SSM_DSTATE), lambda i: (layer, i, 0, 0, 0)),
        pl.BlockSpec((None, CONV_W - 1, SB, SSM_CONV_DIM), lambda i: (layer, 0, i, 0)),
        pl.BlockSpec((1, HG_WIDTH), const2),
        pl.BlockSpec((1, HG_WIDTH), const2),
        pl.BlockSpec((CONV_W, RG_WIDTH), const2),
        pl.BlockSpec((1, RG_WIDTH), const2),
        pl.BlockSpec((RG_HEADS, RG_HEAD_DIM, RG_HEAD_DIM), const3),
        pl.BlockSpec((1, RG_WIDTH), const2),
        pl.BlockSpec((RG_HEADS, RG_HEAD_DIM, RG_HEAD_DIM), const3),
        pl.BlockSpec((1, RG_WIDTH), const2),
        pl.BlockSpec((1, RG_WIDTH), const2),
        pl.BlockSpec((CONV_W, SSM_CONV_DIM), const2),
        pl.BlockSpec((1, SSM_CONV_DIM), const2),
        pl.BlockSpec((1, LANES), const2),
        pl.BlockSpec((1, SSM_WIDTH), const2),
        pl.BlockSpec((1, SSM_WIDTH), const2),
        pl.BlockSpec((1, SSM_WIDTH), const2),
        pl.BlockSpec((LANES, SSM_WIDTH), const2),
    ]
    hg_blk, ssm_blk = (SB, HG_HEADS, HG_DK, HG_DV), (SB, SSM_HEADS, SSM_HEADDIM, SSM_DSTATE)
    n_prev = len(prev)
    for _ in prev:
        in_specs += [pl.BlockSpec(hg_blk, lambda i: (i, 0, 0, 0)), pl.BlockSpec(ssm_blk, lambda i: (i, 0, 0, 0))]
    stacked = lambda blk: (pl.BlockSpec((n_prev + 1,) + blk, lambda i: (0, i, 0, 0, 0)) if n_prev
                           else pl.BlockSpec(blk, lambda i: (i, 0, 0, 0)))
    lead = (n_prev + 1,) if n_prev else ()
    out_specs = [
        pl.BlockSpec((SB, HG_WIDTH), lambda i: (i, 0)),
        pl.BlockSpec((SB, RG_WIDTH), lambda i: (i, 0)),
        pl.BlockSpec((SB, SSM_WIDTH), lambda i: (i, 0)),
        stacked(hg_blk),
        pl.BlockSpec((SB, RG_WIDTH), lambda i: (i, 0)),
        pl.BlockSpec((CONV_W - 1, SB, RG_WIDTH), lambda i: (0, i, 0)),
        stacked(ssm_blk),
        pl.BlockSpec((CONV_W - 1, SB, SSM_CONV_DIM), lambda i: (0, i, 0)),
    ]
    out_shape = [
        jax.ShapeDtypeStruct((bs, HG_WIDTH), BF16),
        jax.ShapeDtypeStruct((bs, RG_WIDTH), BF16),
        jax.ShapeDtypeStruct((bs, SSM_WIDTH), BF16),
        jax.ShapeDtypeStruct(lead + hg_s.shape[1:], F32),
        jax.ShapeDtypeStruct(rg_h.shape[1:], F32),
        jax.ShapeDtypeStruct(rg_c_t.shape[1:], F32),
        jax.ShapeDtypeStruct(lead + ssm_s.shape[1:], F32),
        jax.ShapeDtypeStruct(ssm_c_t.shape[1:], F32),
    ]
    outs = pl.pallas_call(
        functools.partial(_sample_kernel, n_prev=n_prev),
        grid=(bs // SB,),
        in_specs=in_specs,
        out_specs=out_specs,
        out_shape=out_shape,
        scratch_shapes=[
            pltpu.VMEM((SB, HG_WIDTH), F32), pltpu.VMEM((SB, HG_WIDTH), F32), pltpu.VMEM((SB, HG_WIDTH), F32),
            pltpu.VMEM((SB, HG_WIDTH), F32),
            pltpu.VMEM((SB, SSM_WIDTH), F32), pltpu.VMEM((SB, SSM_WIDTH), F32),
            pltpu.VMEM((SB, SSM_GROUPS * SSM_DSTATE), F32), pltpu.VMEM((SB, SSM_GROUPS * SSM_DSTATE), F32),
            pltpu.VMEM((SB, SSM_WIDTH), F32),
        ],
        compiler_params=pltpu.CompilerParams(
            dimension_semantics=("parallel",), vmem_limit_bytes=VMEM_LIMIT),
        name="sample_mix",
    )(proj, hg_s, rg_h, rg_c_t, ssm_s, ssm_c_t,
      lb.reshape(1, HG_WIDTH), row(p['hg_norm_w'], HG_WIDTH),
      p['rg_conv_w'], row(p['rg_conv_b'], RG_WIDTH), p['rg_wa'].astype(BF16), row(p['rg_ba'], RG_WIDTH),
      p['rg_wx'].astype(BF16), row(p['rg_bx'], RG_WIDTH), row(p['rg_lambda'], RG_WIDTH),
      p['ssm_conv_w'], row(p['ssm_conv_b'], SSM_CONV_DIM), dtb, aexp, dexp, row(p['ssm_norm_w'], SSM_WIDTH),
      expand, *(a for pair in prev for a in pair))
    return tuple(outs[:3]), tuple(outs[3:])


def kernel(x_prompt, x_sample, state_hgrn, state_rglru, state_rglru_conv, state_ssm, state_ssm_conv, norm_g, ffn1_w_gate, ffn1_w_up, ffn1_w_down, ffn2_w_gate, ffn2_w_up, ffn2_w_down, w_in, w_out, hg_lb_logits, hg_norm_w, rg_conv_w, rg_conv_b, rg_wa, rg_ba, rg_wx, rg_bx, rg_lambda, ssm_conv_w, ssm_conv_b, ssm_dt_bias, ssm_a_log, ssm_d, ssm_norm_w):
    bp, lp, _ = x_prompt.shape
    bs, ls, _ = x_sample.shape
    mp = bp * lp
    ms = bs * ls
    assert (mp + ms) % IN_ROWS == 0 and D_FF % FF_TILE == 0 and D_IN_PAD % IN_TILE == 0

    lw = {
        'hg_norm_w': hg_norm_w, 'rg_conv_w': rg_conv_w, 'rg_conv_b': rg_conv_b, 'rg_wa': rg_wa,
        'rg_ba': rg_ba, 'rg_wx': rg_wx, 'rg_bx': rg_bx, 'rg_lambda': rg_lambda,
        'ssm_conv_w': ssm_conv_w, 'ssm_conv_b': ssm_conv_b, 'ssm_dt_bias': ssm_dt_bias,
        'ssm_a_log': ssm_a_log, 'ssm_d': ssm_d, 'ssm_norm_w': ssm_norm_w,
    }
    lb_cum = jnp.cumsum(jax.nn.softmax(hg_lb_logits.astype(F32), axis=0), axis=0)
    lower_bounds = lb_cum - lb_cum[:1]

    taps_first = lambda a: jnp.transpose(a, (0, 2, 1, 3))
    sample_init = (state_hgrn, state_rglru, taps_first(state_rglru_conv), state_ssm, taps_first(state_ssm_conv))

    w_in_b = jnp.pad(w_in.astype(BF16), ((0, 0), (0, 0), (0, D_IN_PAD - D_IN_PROJ)))
    x = (x_prompt.reshape(mp, D_MODEL), x_sample.reshape(ms, D_MODEL))
    new_p = ([], [], [], [], [])
    new_s = ([], [], [], [], [])
    for l in range(DEPTH):
        g = norm_g[l].reshape(6, 1, D_MODEL)
        p = {name: arr[l] for name, arr in lw.items()}
        x = _ffn(x, g[0], g[1], ffn1_w_gate, ffn1_w_up, ffn1_w_down, l, mp, ms, split_in=(l == 0))
        proj = _inproj(x, g[2], w_in_b, l)
        o_hg, hg_new, o_rg, rg_h_new, rg_c_new, o_ssm, ssm_new, ssm_c_new = _prompt_mix(
            proj, p, lower_bounds[l], bp, lp)
        st_p = (hg_new, rg_h_new.reshape(bp, RG_WIDTH), rg_c_new, ssm_new, ssm_c_new)
        last = l == DEPTH - 1
        prev = tuple(zip(new_s[0], new_s[3])) if last else ()
        o_s, st_s = _sample_mix(proj, sample_init, l, p, lower_bounds[l], mp, prev)
        x = _outproj(x, (o_hg, o_rg, o_ssm), o_s, g[3], w_out, l)
        x = _ffn(x, g[4], g[5], ffn2_w_gate, ffn2_w_up, ffn2_w_down, l, mp, ms, split_out=last)
        for acc, s in zip(new_p, st_p):
            acc.append(s)
        for acc, s in zip(new_s, st_s):
            acc.append(s)
    hg_p, rg_p, rgc_p, ssm_p, ssmc_p = (jnp.stack(a) for a in new_p)
    hg_s, ssm_s = new_s[0][-1], new_s[3][-1]
    rg_s, rgc_s, ssmc_s = (jnp.stack(new_s[k]) for k in (1, 2, 4))
    rgc_s, ssmc_s = taps_first(rgc_s), taps_first(ssmc_s)
    y_prompt = x[0].reshape(bp, lp, D_MODEL)
    y_sample = x[1].reshape(bs, ls, D_MODEL)
    return (y_prompt, y_sample, hg_p, hg_s, rg_p, rg_s, rgc_p, rgc_s, ssm_p, ssm_s, ssmc_p, ssmc_s)
```

```python
import functools
import math

import jax
import jax.numpy as jnp
import numpy as np
from jax import lax
from jax.experimental import pallas as pl
from jax.experimental.pallas import tpu as pltpu

F32 = jnp.float32
BF16 = jnp.bfloat16

D_MODEL = 2048
DEPTH = 2
EPS = 1e-6
CONV_W = 4
HG_HEADS = 4
HG_DK = 128
HG_DV = 128
HG_WIDTH = HG_HEADS * HG_DV
HG_CHUNK = 32
RG_HEADS = 6
RG_HEAD_DIM = 128
RG_WIDTH = RG_HEADS * RG_HEAD_DIM
RG_C = 8.0
SSM_HEADS = 12
SSM_HEADDIM = 64
SSM_WIDTH = SSM_HEADS * SSM_HEADDIM
SSM_GROUPS = 2
HEADS_PER_GROUP = SSM_HEADS // SSM_GROUPS
SSM_DSTATE = 128
SSM_CHUNK = 64
SSM_CONV_DIM = SSM_WIDTH + 2 * SSM_GROUPS * SSM_DSTATE
D_MIX = HG_WIDTH + RG_WIDTH + SSM_WIDTH
IN_SIZES = (HG_HEADS * HG_DK, HG_HEADS * HG_DK, HG_WIDTH, HG_WIDTH, RG_WIDTH, RG_WIDTH,
            SSM_WIDTH, SSM_CONV_DIM, SSM_HEADS)
D_IN_PROJ = sum(IN_SIZES)
D_FF = 5632

LANES = 128
D_IN_PAD = -(-D_IN_PROJ // LANES) * LANES
FFN_ROWS = 1040
FF_TILE = 256
IN_ROWS = 640
IN_TILE = 1920
VMEM_LIMIT = 56 * 1024 * 1024
NORM_ROWS = 16
FFN_VMEM_LIMIT = 62 * 1024 * 1024


def _rms(x, g):
    return x * lax.rsqrt(jnp.mean(x * x, axis=-1, keepdims=True) + EPS) * g


def _ffn_kernel(*refs, split_in, split_out, n_prompt_last):
    refs = list(refs)
    x_ref = refs.pop(0)
    xs_ref = refs.pop(0) if split_in else None
    gin_ref, gout_ref, wg_ref, wu_ref, wd_ref = (refs.pop(0) for _ in range(5))
    o_ref = refs.pop(0)
    ys_ref = refs.pop(0) if split_out else None
    xn_ref = refs.pop(0)
    i, j = pl.program_id(0), pl.program_id(1)
    last_i = pl.num_programs(0) - 1
    npl = n_prompt_last

    @pl.when(j == 0)
    def _():
        o_ref[...] = jnp.zeros_like(o_ref)
        if split_in:
            @pl.when(i < last_i)
            def _():
                xn_ref[...] = _rms(x_ref[...], gin_ref[...]).astype(BF16)

            @pl.when(i == last_i)
            def _():
                xn_ref[0:npl, :] = _rms(x_ref[0:npl, :], gin_ref[...]).astype(BF16)
                xn_ref[npl:, :] = _rms(xs_ref[...], gin_ref[...]).astype(BF16)
        else:
            xn_ref[...] = _rms(x_ref[...], gin_ref[...]).astype(BF16)

    xn = xn_ref[...]
    g = jnp.dot(xn, wg_ref[...].astype(BF16), preferred_element_type=F32)
    u = jnp.dot(xn, wu_ref[...].astype(BF16), preferred_element_type=F32)
    h = (g * jax.nn.sigmoid(g) * u).astype(BF16)
    o_ref[...] += jnp.dot(h, wd_ref[...].astype(BF16), preferred_element_type=F32)

    def residual_out(src_ref, src0, dst0, n_rows):
        for r in range(0, n_rows, NORM_ROWS):
            dst = slice(dst0 + r, dst0 + r + NORM_ROWS)
            o_ref[dst, :] = (src_ref[src0 + r:src0 + r + NORM_ROWS, :]
                             + 0.5 * _rms(o_ref[dst, :], gout_ref[...]))

    @pl.when(j == pl.num_programs(1) - 1)
    def _():
        if split_in:
            @pl.when(i < last_i)
            def _():
                residual_out(x_ref, 0, 0, FFN_ROWS)

            @pl.when(i == last_i)
            def _():
                residual_out(x_ref, 0, 0, npl)
                residual_out(xs_ref, 0, npl, FFN_ROWS - npl)
        else:
            residual_out(x_ref, 0, 0, FFN_ROWS)
        if split_out:
            @pl.when(i == last_i)
            def _():
                ys_ref[...] = o_ref[npl:, :]


def _ffn(x, g_in, g_out, wg, wu, wd, layer, mp, ms, split_in=False, split_out=False):
    m = mp + ms
    n_tiles = m // FFN_ROWS
    n_prompt_last = mp - (n_tiles - 1) * FFN_ROWS
    assert m % FFN_ROWS == 0 and n_prompt_last + ms == FFN_ROWS and n_prompt_last % 16 == 0
    rows = pl.BlockSpec((FFN_ROWS, D_MODEL), lambda i, j: (i, 0))
    sample = pl.BlockSpec((ms, D_MODEL), lambda i, j: (0, 0))
    vec = pl.BlockSpec((1, D_MODEL), lambda i, j: (0, 0))
    xs = tuple(x) if split_in else (x,)
    if split_out:
        out_specs = [rows, sample]
        out_shape = [jax.ShapeDtypeStruct((mp, D_MODEL), F32), jax.ShapeDtypeStruct((ms, D_MODEL), F32)]
    else:
        out_specs = rows
        out_shape = jax.ShapeDtypeStruct((m, D_MODEL), F32)
    return pl.pallas_call(
        functools.partial(_ffn_kernel, split_in=split_in, split_out=split_out, n_prompt_last=n_prompt_last),
        grid=(n_tiles, D_FF // FF_TILE),
        in_specs=[rows] + ([sample] if split_in else []) + [
            vec, vec,
            pl.BlockSpec((None, D_MODEL, FF_TILE), lambda i, j: (layer, 0, j)),
            pl.BlockSpec((None, D_MODEL, FF_TILE), lambda i, j: (layer, 0, j)),
            pl.BlockSpec((None, FF_TILE, D_MODEL), lambda i, j: (layer, j, 0)),
        ],
        out_specs=out_specs,
        out_shape=out_shape,
        scratch_shapes=[pltpu.VMEM((FFN_ROWS, D_MODEL), BF16)],
        compiler_params=pltpu.CompilerParams(
            dimension_semantics=("parallel", "arbitrary"), vmem_limit_bytes=FFN_VMEM_LIMIT),
        name="ffn",
    )(*xs, g_in, g_out, wg, wu, wd)


def _inproj_kernel(x_ref, g_ref, w_ref, o_ref, xn_ref):
    @pl.when(pl.program_id(1) == 0)
    def _():
        xn_ref[...] = _rms(x_ref[...], g_ref[...]).astype(BF16)

    o_ref[...] = jnp.dot(xn_ref[...], w_ref[...], preferred_element_type=F32)


def _inproj(x, g, w_in, layer):
    m = x.shape[0]
    return pl.pallas_call(
        _inproj_kernel,
        grid=(m // IN_ROWS, D_IN_PAD // IN_TILE),
        in_specs=[
            pl.BlockSpec((IN_ROWS, D_MODEL), lambda i, j: (i, 0)),
            pl.BlockSpec((1, D_MODEL), lambda i, j: (0, 0)),
            pl.BlockSpec((None, D_MODEL, IN_TILE), lambda i, j: (layer, 0, j)),
        ],
        out_specs=pl.BlockSpec((IN_ROWS, IN_TILE), lambda i, j: (i, j)),
        out_shape=jax.ShapeDtypeStruct((m, D_IN_PAD), F32),
        scratch_shapes=[pltpu.VMEM((IN_ROWS, D_MODEL), BF16)],
        compiler_params=pltpu.CompilerParams(
            dimension_semantics=("parallel", "arbitrary"), vmem_limit_bytes=VMEM_LIMIT),
        name="inproj",
    )(x, g, w_in)


OUT_TILE = 512


def _outproj_kernel(x_ref, php_ref, prg_ref, pss_ref, shg_ref, srg_ref, sss_ref, g_ref, w_ref, y_ref, wb_ref):
    i = pl.program_id(0)
    last = pl.num_programs(0) - 1
    r0, r1 = HG_WIDTH, HG_WIDTH + RG_WIDTH

    @pl.when(i == 0)
    def _():
        wb_ref[...] = w_ref[...].astype(BF16)

    def mixed(ohg, org, oss):
        m = jnp.dot(ohg, wb_ref[0:r0, :], preferred_element_type=F32)
        m += jnp.dot(org, wb_ref[r0:r1, :], preferred_element_type=F32)
        m += jnp.dot(oss, wb_ref[r1:, :], preferred_element_type=F32)
        return _rms(m, g_ref[...])

    @pl.when(i < last)
    def _():
        y_ref[...] = x_ref[...] + mixed(php_ref[...], prg_ref[...], pss_ref[...])

    @pl.when(i == last)
    def _():
        ns = shg_ref.shape[0]
        y_ref[0:ns, :] = x_ref[0:ns, :] + mixed(shg_ref[...], srg_ref[...], sss_ref[...])


def _outproj(x, o_prompt, o_sample, g, w_out, layer):
    mp, ms = o_prompt[0].shape[0], o_sample[0].shape[0]
    assert mp % OUT_TILE == 0 and ms <= OUT_TILE and x.shape[0] == mp + ms
    n_p = mp // OUT_TILE
    widths = (HG_WIDTH, RG_WIDTH, SSM_WIDTH)
    return pl.pallas_call(
        _outproj_kernel,
        grid=(n_p + 1,),
        in_specs=[pl.BlockSpec((OUT_TILE, D_MODEL), lambda i: (i, 0))]
        + [pl.BlockSpec((OUT_TILE, w), lambda i: (jnp.minimum(i, n_p - 1), 0)) for w in widths]
        + [pl.BlockSpec((ms, w), lambda i: (0, 0)) for w in widths]
        + [pl.BlockSpec((1, D_MODEL), lambda i: (0, 0)),
           pl.BlockSpec((None, D_MIX, D_MODEL), lambda i: (layer, 0, 0), pipeline_mode=pl.Buffered(1))],
        out_specs=pl.BlockSpec((OUT_TILE, D_MODEL), lambda i: (i, 0)),
        out_shape=jax.ShapeDtypeStruct((mp + ms, D_MODEL), F32),
        scratch_shapes=[pltpu.VMEM((D_MIX, D_MODEL), BF16)],
        compiler_params=pltpu.CompilerParams(
            dimension_semantics=("arbitrary",), vmem_limit_bytes=VMEM_LIMIT),
        name="outproj",
    )(x, *o_prompt, *o_sample, g, w_out)


_COL = np.cumsum((0,) + IN_SIZES)
HGQ_OFF, HGF_OFF, HGV_OFF, HGG_OFF, RGX_OFF, RGG_OFF, SSZ_OFF, SSX_OFF, SSDT_OFF = (int(c) for c in _COL[:9])
PIECE = 256
assert all(off % PIECE == 0 for off in (RGX_OFF, RGG_OFF, SSZ_OFF, SSX_OFF)) and SSDT_OFF % LANES == 0
assert all(w % PIECE == 0 for w in (RG_WIDTH, SSM_WIDTH, SSM_CONV_DIM))


def _pieces(off, width, rows, row_map):
    return [pl.BlockSpec((rows, PIECE), functools.partial(lambda k, *g: (row_map(*g), k), off // PIECE + k))
            for k in range(width // PIECE)]


def _cat(refs, rows=slice(None)):
    return jnp.concatenate([r[rows, :] for r in refs], axis=1)


SUBLANES = 8
SSD_T = 256
NEG_BIG = -1e30


def _split3(x):
    hi = x.astype(BF16)
    r = x - hi.astype(F32)
    mid = r.astype(BF16)
    lo = (r - mid.astype(F32)).astype(BF16)
    return hi, mid, lo


def _cumsum_rows(tri, x):
    w = x.shape[1]
    parts = jnp.concatenate(_split3(x), axis=1)
    r = jnp.dot(tri, parts, preferred_element_type=F32)
    return r[:, :w] + r[:, w:2 * w] + r[:, 2 * w:]


def _silu(x):
    return x * jax.nn.sigmoid(x)


def _neg_expm1(x, exp_x):
    series = -x * (1.0 + x * (1 / 2 + x * (1 / 6 + x * (1 / 24 + x * (1 / 120)))))
    return jnp.where(x > -1 / 16, series, 1.0 - exp_x)


def _softplus(x):
    return jnp.maximum(x, 0.0) + jnp.log(1.0 + jnp.exp(-jnp.abs(x)))


def _conv4(xp_ref, tail_ref, x, w_ref, b_ref):
    t = x.shape[0]
    xp_ref[0:SUBLANES, :] = tail_ref[...]
    xp_ref[SUBLANES:, :] = x
    tail_ref[...] = x[t - SUBLANES:, :]
    y = b_ref[...] + w_ref[CONV_W - 1:CONV_W, :] * x
    for k in range(CONV_W - 1):
        off = SUBLANES - (CONV_W - 1) + k
        y = y + w_ref[k:k + 1, :] * xp_ref[off:off + t, :]
    return y


N_Z, N_XBC = SSM_WIDTH // PIECE, SSM_CONV_DIM // PIECE


SSD_N_IN = N_Z + N_XBC + 9
SSD_N_OUT, SSD_N_SCRATCH = 3, 4


def _ssd_prompt_body(ins, outs, scratch):
    z_refs, xbc_refs = ins[:N_Z], ins[N_Z:N_Z + N_XBC]
    dt_ref, tri_ref, expand_ref, cw_ref, cb_ref, dtb_ref, alog_ref, dexp_ref, nw_ref = ins[N_Z + N_XBC:]
    o_ref, _, cst_ref = outs
    xp_ref, tail_ref, s_ref, y_ref = scratch
    t = SSD_T
    z = _cat(z_refs)
    xbc_raw = _cat(xbc_refs)
    dt_raw = dt_ref[...]

    xbc = _silu(_conv4(xp_ref, tail_ref, xbc_raw, cw_ref, cb_ref))
    cst_ref[...] = _cat(xbc_refs, slice(t - (CONV_W - 1), t))
    xs = xbc[:, :SSM_WIDTH]
    gs = SSM_GROUPS * SSM_DSTATE
    bm = xbc[:, SSM_WIDTH:SSM_WIDTH + gs].astype(BF16)
    cm = xbc[:, SSM_WIDTH + gs:].astype(BF16)

    dt = _softplus(dt_raw + dtb_ref[...])
    a = dt * -jnp.exp(alog_ref[...])
    cs = _cumsum_rows(tri_ref[...], a)
    cs_t = cs.T
    cs_last = cs[t - 1:t, :]
    e_last = jnp.exp(cs_last)
    row = lax.broadcasted_iota(jnp.int32, (t, t), 0)
    col = lax.broadcasted_iota(jnp.int32, (t, t), 1)
    causal = row >= col

    def per_lane(v):
        parts = jnp.concatenate(_split3(v), axis=0)
        r = jnp.dot(parts, expand_ref[...], preferred_element_type=F32)
        return r[0:t] + r[t:2 * t] + r[2 * t:]

    dt_w = per_lane(dt)
    cs_w = per_lane(cs)
    e_cs_w = jnp.exp(cs_w)
    xdt = xs * dt_w
    xdt_b = xdt.astype(BF16)
    w_b = (xdt * jnp.exp(cs_w[t - 1:t, :] - cs_w)).astype(BF16)
    pair_lane = lax.broadcasted_iota(jnp.int32, (t, LANES), 1) < SSM_HEADDIM
    gw = SSM_WIDTH // SSM_GROUPS

    for g in range(SSM_GROUPS):
        heads = range(g * HEADS_PER_GROUP, (g + 1) * HEADS_PER_GROUP)
        bg = bm[:, g * SSM_DSTATE:(g + 1) * SSM_DSTATE]
        cg = cm[:, g * SSM_DSTATE:(g + 1) * SSM_DSTATE]
        cb = lax.dot_general(cg, bg, _NT, preferred_element_type=F32)
        s_prev = s_ref[heads.start:heads.stop].reshape(gw, SSM_DSTATE)
        y_in = lax.dot_general(cg, s_prev.astype(BF16), _NT, preferred_element_type=F32)
        st = lax.dot_general(w_b[:, g * gw:(g + 1) * gw], bg, _TN, preferred_element_type=F32)
        decay = jnp.concatenate([jnp.broadcast_to(e_last[:, h:h + 1], (SSM_HEADDIM, SSM_DSTATE)) for h in heads],
                                axis=0)
        s_ref[heads.start:heads.stop] = (decay * s_prev + st).reshape(HEADS_PER_GROUP, SSM_HEADDIM, SSM_DSTATE)

        for k in range(HEADS_PER_GROUP // 2):
            lanes = slice(g * gw + k * LANES, g * gw + (k + 1) * LANES)
            res = []
            for h in (heads.start + 2 * k, heads.start + 2 * k + 1):
                seg = jnp.where(causal, cs[:, h:h + 1] - cs_t[h:h + 1, :], NEG_BIG)
                scores = (cb * jnp.exp(seg)).astype(BF16)
                res.append(jnp.dot(scores, xdt_b[:, lanes], preferred_element_type=F32))
            y_ref[:, lanes] = (jnp.where(pair_lane, res[0], res[1])
                               + e_cs_w[:, lanes] * y_in[:, k * LANES:(k + 1) * LANES])

    y = (y_ref[...] + dexp_ref[...] * xs) * _silu(z)
    gw = SSM_WIDTH // SSM_GROUPS
    for g in range(SSM_GROUPS):
        yg = y[:, g * gw:(g + 1) * gw]
        yg = yg * lax.rsqrt(jnp.mean(yg * yg, axis=-1, keepdims=True) + EPS)
        o_ref[:, g * gw:(g + 1) * gw] = (yg * nw_ref[:, g * gw:(g + 1) * gw]).astype(BF16)


def _tri(t):
    return jnp.tril(jnp.ones((t, t), F32)).astype(BF16)


def _head_to_lanes():
    return jnp.asarray(np.arange(LANES)[:, None] == (np.arange(SSM_WIDTH)[None, :] // SSM_HEADDIM), BF16)


def _ssd_prompt_parts(proj, p, bp, lp):
    nc = lp // SSD_T
    full = lambda b, n: (0, 0)
    rows = lambda b, n: b * nc + n
    dtb = jnp.pad(p['ssm_dt_bias'].reshape(1, SSM_HEADS), ((0, 0), (0, LANES - SSM_HEADS)))
    alog = jnp.pad(p['ssm_a_log'].reshape(1, SSM_HEADS), ((0, 0), (0, LANES - SSM_HEADS)))
    dexp = jnp.repeat(p['ssm_d'], SSM_HEADDIM).reshape(1, SSM_WIDTH)
    return dict(
        in_specs=_pieces(SSZ_OFF, SSM_WIDTH, SSD_T, rows) + _pieces(SSX_OFF, SSM_CONV_DIM, SSD_T, rows) + [
            pl.BlockSpec((SSD_T, LANES), lambda b, n: (rows(b, n), SSDT_OFF // LANES)),
            pl.BlockSpec((SSD_T, SSD_T), full),
            pl.BlockSpec((LANES, SSM_WIDTH), full),
            pl.BlockSpec((CONV_W, SSM_CONV_DIM), full),
            pl.BlockSpec((1, SSM_CONV_DIM), full),
            pl.BlockSpec((1, LANES), full),
            pl.BlockSpec((1, LANES), full),
            pl.BlockSpec((1, SSM_WIDTH), full),
            pl.BlockSpec((1, SSM_WIDTH), full),
        ],
        out_specs=[
            pl.BlockSpec((SSD_T, SSM_WIDTH), lambda b, n: (b * nc + n, 0)),
            pl.BlockSpec((None, SSM_HEADS, SSM_HEADDIM, SSM_DSTATE), lambda b, n: (b, 0, 0, 0)),
            pl.BlockSpec((None, CONV_W - 1, SSM_CONV_DIM), lambda b, n: (b, 0, 0)),
        ],
        out_shape=[
            jax.ShapeDtypeStruct((bp * lp, SSM_WIDTH), BF16),
            jax.ShapeDtypeStruct((bp, SSM_HEADS, SSM_HEADDIM, SSM_DSTATE), F32),
            jax.ShapeDtypeStruct((bp, CONV_W - 1, SSM_CONV_DIM), F32),
        ],
        scratch_shapes=[
            pltpu.VMEM((SSD_T + SUBLANES, SSM_CONV_DIM), F32),
            pltpu.VMEM((SUBLANES, SSM_CONV_DIM), F32),
            pltpu.VMEM((SSM_HEADS, SSM_HEADDIM, SSM_DSTATE), F32),
            pltpu.VMEM((SSD_T, SSM_WIDTH), F32),
        ],
        args=[proj] * (N_Z + N_XBC + 1) + [
            _tri(SSD_T), _head_to_lanes(), p['ssm_conv_w'], p['ssm_conv_b'].reshape(1, SSM_CONV_DIM), dtb, alog,
            dexp, p['ssm_norm_w'].reshape(1, SSM_WIDTH)],
    )


HG_T = 256

_NT = (((1,), (1,)), ((), ()))
_TN = (((0,), (0,)), ((), ()))


HG_COLS = 4 * HG_WIDTH
assert (HGQ_OFF, HGF_OFF, HGV_OFF, HGG_OFF) == (0, HG_WIDTH, 2 * HG_WIDTH, 3 * HG_WIDTH)


def _hg_tile(p_ref, off, head, rows=slice(None)):
    return p_ref[rows, off + head * LANES:off + (head + 1) * LANES]


HG_N_IN, HG_N_OUT, HG_N_SCRATCH = 4, 2, 1


def _hgrn_prompt_body(ins, outs, scratch):
    hg_ref, lb_ref, nw_ref, tri_ref = ins
    o_ref, _ = outs
    s_ref, = scratch
    t, c = HG_T, HG_CHUNK
    nch = t // c

    row = lax.broadcasted_iota(jnp.int32, (t, t), 0)
    col = lax.broadcasted_iota(jnp.int32, (t, t), 1)
    keep = (row >= col) & (row // c == col // c)
    tri = tri_ref[...]

    v, o_intra, kd, qe, decay = [], [], [], [], []
    for h in range(HG_HEADS):
        sl = slice(h * LANES, (h + 1) * LANES)
        lb = lb_ref[:, sl]
        fz = _hg_tile(hg_ref, HGF_OFF, h)
        logf = jnp.log(lb + (1.0 - lb) * jax.nn.sigmoid(fz))
        kk3 = ((1.0 - lb) * jax.nn.sigmoid(-fz)).reshape(nch, c, HG_DK)
        qh3 = _silu(_hg_tile(hg_ref, HGQ_OFF, h)).reshape(nch, c, HG_DK)
        vh = _hg_tile(hg_ref, HGV_OFF, h).astype(BF16)
        b3 = _cumsum_rows(tri, logf).reshape(nch, c, HG_DK)
        b_mid = b3[:, c // 2:c // 2 + 1, :]
        b_last = b3[:, c - 1:c, :]
        q_in = (qh3 * jnp.exp(b3 - b_mid)).reshape(t, HG_DK).astype(BF16)
        k_in = (kk3 * jnp.exp(b_mid - b3)).reshape(t, HG_DK).astype(BF16)
        a = jnp.where(keep, lax.dot_general(q_in, k_in, _NT, preferred_element_type=F32), 0.0)
        v.append(vh)
        o_intra.append(jnp.dot(a.astype(BF16), vh, preferred_element_type=F32))
        kd.append((kk3 * jnp.exp(b_last - b3)).astype(BF16))
        qe.append((qh3 * jnp.exp(b3)).astype(BF16))
        decay.append(jnp.exp(b_last))

    outs = [[] for _ in range(HG_HEADS)]
    for ci in range(nch):
        rows = slice(ci * c, (ci + 1) * c)
        for h in range(HG_HEADS):
            s_t = s_ref[h]
            o_inter = lax.dot_general(qe[h][ci], s_t.astype(BF16), _NT, preferred_element_type=F32)
            outs[h].append(o_intra[h][rows, :] + o_inter)
            ds_t = lax.dot_general(v[h][rows, :], kd[h][ci], _TN, preferred_element_type=F32)
            s_ref[h] = s_t * decay[h][ci] + ds_t

    for h in range(HG_HEADS):
        sl = slice(h * LANES, (h + 1) * LANES)
        o = jnp.concatenate(outs[h], axis=0)
        o = o * lax.rsqrt(jnp.mean(o * o, axis=-1, keepdims=True) + EPS)
        g = _hg_tile(hg_ref, HGG_OFF, h)
        o_ref[:, sl] = (o * nw_ref[:, sl] * _silu(g)).astype(BF16)


def _tri_chunks(t, c):
    r = np.arange(t)
    return jnp.asarray((r[:, None] >= r[None, :]) & (r[:, None] // c == r[None, :] // c), BF16)


def _hgrn_prompt_parts(proj, p, lb, bp, lp):
    nc = lp // HG_T
    const = lambda b, n: (0, 0)
    return dict(
        in_specs=[
            pl.BlockSpec((HG_T, HG_COLS), lambda b, n: (b * nc + n, 0)),
            pl.BlockSpec((1, HG_WIDTH), const),
            pl.BlockSpec((1, HG_WIDTH), const),
            pl.BlockSpec((HG_T, HG_T), const),
        ],
        out_specs=[
            pl.BlockSpec((HG_T, HG_WIDTH), lambda b, n: (b * nc + n, 0)),
            pl.BlockSpec((None, HG_HEADS, HG_DK, HG_DV), lambda b, n: (b, 0, 0, 0)),
        ],
        out_shape=[
            jax.ShapeDtypeStruct((bp * lp, HG_WIDTH), BF16),
            jax.ShapeDtypeStruct((bp, HG_HEADS, HG_DK, HG_DV), F32),
        ],
        scratch_shapes=[pltpu.VMEM((HG_HEADS, HG_DV, HG_DK), F32)],
        args=[proj, lb.reshape(1, HG_WIDTH), p['hg_norm_w'].reshape(1, HG_WIDTH), _tri_chunks(HG_T, HG_CHUNK)],
    )


RG_T = 256


def _scan_rows(a, u, h_in):
    t, w = a.shape
    rows = lax.broadcasted_iota(jnp.int32, a.shape, 0) % SUBLANES
    s = 1
    while s < SUBLANES:
        keep = rows >= s
        a_sh = jnp.where(keep, pltpu.roll(a, s, axis=0), 1.0)
        u_sh = jnp.where(keep, pltpu.roll(u, s, axis=0), 0.0)
        u = u + a * u_sh
        a = a * a_sh
        s *= 2
    hs = []
    for g in range(t // SUBLANES):
        sl = slice(g * SUBLANES, (g + 1) * SUBLANES)
        h = u[sl, :] + a[sl, :] * h_in
        hs.append(h)
        h_in = h[SUBLANES - 1:, :]
    return jnp.concatenate(hs, axis=0)


def _rg_gates(xc, wa_ref, ba_ref, wx_ref, bx_ref, lam_ref):
    xb = xc.astype(BF16)
    ra, ri = [], []
    for h in range(RG_HEADS):
        sl = slice(h * RG_HEAD_DIM, (h + 1) * RG_HEAD_DIM)
        ra.append(jnp.dot(xb[:, sl], wa_ref[h], preferred_element_type=F32))
        ri.append(jnp.dot(xb[:, sl], wx_ref[h], preferred_element_type=F32))
    r = jax.nn.sigmoid(jnp.concatenate(ra, axis=1) + ba_ref[...])
    ig = jax.nn.sigmoid(jnp.concatenate(ri, axis=1) + bx_ref[...])
    log_a = -RG_C * r * _softplus(-lam_ref[...])
    a = jnp.exp(log_a)
    mult = jnp.sqrt(_neg_expm1(2.0 * log_a, a * a))
    return a, mult, ig


def _gelu_tanh(x):
    return 0.5 * x * (1.0 + jnp.tanh(math.sqrt(2.0 / math.pi) * (x + 0.044715 * (x * x * x))))


N_RG = RG_WIDTH // PIECE


RG_N_IN, RG_N_OUT, RG_N_SCRATCH = 2 * N_RG + 7, 3, 3


def _rglru_prompt_body(ins, outs, scratch):
    x_refs, gate_refs = ins[:N_RG], ins[N_RG:2 * N_RG]
    cw_ref, cb_ref, wa_ref, ba_ref, wx_ref, bx_ref, lam_ref = ins[2 * N_RG:]
    o_ref, h_ref, cst_ref = outs
    xp_ref, tail_ref, hprev_ref = scratch
    n = pl.program_id(1)
    t = RG_T

    x = _cat(x_refs)
    xc = _conv4(xp_ref, tail_ref, x, cw_ref, cb_ref)
    cst_ref[...] = _cat(x_refs, slice(t - (CONV_W - 1), t))
    a, mult, ig = _rg_gates(xc, wa_ref, ba_ref, wx_ref, bx_ref, lam_ref)
    rows = lax.broadcasted_iota(jnp.int32, a.shape, 0)
    mult = jnp.where((rows == 0) & (n == 0), 1.0, mult)
    h = _scan_rows(a, mult * ig * xc, hprev_ref[...])
    hprev_ref[...] = h[t - 1:t, :]
    h_ref[...] = h[t - 1:t, :]
    o_ref[...] = (h * _gelu_tanh(_cat(gate_refs))).astype(BF16)


def _rglru_prompt_parts(proj, p, bp, lp):
    nc = lp // RG_T
    full2 = lambda b, n: (0, 0)
    full3 = lambda b, n: (0, 0, 0)
    rows = lambda b, n: b * nc + n
    row = lambda a: a.reshape(1, RG_WIDTH)
    return dict(
        in_specs=_pieces(RGX_OFF, RG_WIDTH, RG_T, rows) + _pieces(RGG_OFF, RG_WIDTH, RG_T, rows) + [
            pl.BlockSpec((CONV_W, RG_WIDTH), full2),
            pl.BlockSpec((1, RG_WIDTH), full2),
            pl.BlockSpec((RG_HEADS, RG_HEAD_DIM, RG_HEAD_DIM), full3),
            pl.BlockSpec((1, RG_WIDTH), full2),
            pl.BlockSpec((RG_HEADS, RG_HEAD_DIM, RG_HEAD_DIM), full3),
            pl.BlockSpec((1, RG_WIDTH), full2),
            pl.BlockSpec((1, RG_WIDTH), full2),
        ],
        out_specs=[
            pl.BlockSpec((RG_T, RG_WIDTH), lambda b, n: (b * nc + n, 0)),
            pl.BlockSpec((None, 1, RG_WIDTH), lambda b, n: (b, 0, 0)),
            pl.BlockSpec((None, CONV_W - 1, RG_WIDTH), lambda b, n: (b, 0, 0)),
        ],
        out_shape=[
            jax.ShapeDtypeStruct((bp * lp, RG_WIDTH), BF16),
            jax.ShapeDtypeStruct((bp, 1, RG_WIDTH), F32),
            jax.ShapeDtypeStruct((bp, CONV_W - 1, RG_WIDTH), F32),
        ],
        scratch_shapes=[
            pltpu.VMEM((RG_T + SUBLANES, RG_WIDTH), F32),
            pltpu.VMEM((SUBLANES, RG_WIDTH), F32),
            pltpu.VMEM((1, RG_WIDTH), F32),
        ],
        args=[proj] * (2 * N_RG) + [
            p['rg_conv_w'], row(p['rg_conv_b']), p['rg_wa'].astype(BF16), row(p['rg_ba']),
            p['rg_wx'].astype(BF16), row(p['rg_bx']), row(p['rg_lambda'])],
    )


PROMPT_T = 256
assert HG_T == RG_T == SSD_T == PROMPT_T


def _prompt_mix_kernel(*refs):
    counts = ((HG_N_IN, HG_N_OUT, HG_N_SCRATCH), (RG_N_IN, RG_N_OUT, RG_N_SCRATCH),
              (SSD_N_IN, SSD_N_OUT, SSD_N_SCRATCH))
    refs = list(refs)
    ins = [[refs.pop(0) for _ in range(c[0])] for c in counts]
    outs = [[refs.pop(0) for _ in range(c[1])] for c in counts]
    scratch = [[refs.pop(0) for _ in range(c[2])] for c in counts]
    assert not refs
    n = pl.program_id(1)
    hg_s, = scratch[0]
    _, rg_tail, rg_h = scratch[1]
    _, ssd_tail, ssd_s, _ = scratch[2]

    @pl.when(n == 0)
    def _():
        for r in (hg_s, rg_tail, rg_h, ssd_tail, ssd_s):
            r[...] = jnp.zeros_like(r)

    _rglru_prompt_body(ins[1], outs[1], scratch[1])
    _ssd_prompt_body(ins[2], outs[2], scratch[2])
    _hgrn_prompt_body(ins[0], outs[0], scratch[0])

    @pl.when(n == pl.num_programs(1) - 1)
    def _():
        for h in range(HG_HEADS):
            outs[0][1][h] = hg_s[h].T
        outs[2][1][...] = ssd_s[...]


def _prompt_mix(proj, p, lb, bp, lp):
    parts = (_hgrn_prompt_parts(proj, p, lb, bp, lp), _rglru_prompt_parts(proj, p, bp, lp),
             _ssd_prompt_parts(proj, p, bp, lp))
    cat = lambda key: [v for part in parts for v in part[key]]
    return pl.pallas_call(
        _prompt_mix_kernel,
        grid=(bp, lp // PROMPT_T),
        in_specs=cat('in_specs'),
        out_specs=cat('out_specs'),
        out_shape=cat('out_shape'),
        scratch_shapes=cat('scratch_shapes'),
        compiler_params=pltpu.CompilerParams(
            dimension_semantics=("parallel", "arbitrary"), vmem_limit_bytes=VMEM_LIMIT),
        name="prompt_mix",
    )(*cat('args'))


SB = 8


def _col_bcast(row):
    return jnp.broadcast_to(row, (LANES, LANES)).T


N_SAMPLE_IN = 22


def _sample_kernel(*refs, n_prev):
    (p_ref, hs_ref, rh_ref, rc_ref, ss_ref, sc_ref, lb_ref, hnw_ref,
     rcw_ref, rcb_ref, wa_ref, ba_ref, wx_ref, bx_ref, lam_ref,
     scw_ref, scb_ref, dtb_ref, aexp_ref, dexp_ref, snw_ref, expand_ref) = refs[:N_SAMPLE_IN]
    prev = refs[N_SAMPLE_IN:N_SAMPLE_IN + 2 * n_prev]
    (ohg_ref, org_ref, ossm_ref, hs_all, rh_out, rc_out, ss_all, sc_out,
     f_sc, kk_sc, q_sc, o_sc, adec_sc, xdt_sc, b_sc, c_sc, y_sc) = refs[N_SAMPLE_IN + 2 * n_prev:]
    if n_prev:
        for l in range(n_prev):
            hs_all[l] = prev[2 * l][...]
            ss_all[l] = prev[2 * l + 1][...]
        hs_out, ss_out = hs_all.at[n_prev], ss_all.at[n_prev]
    else:
        hs_out, ss_out = hs_all, ss_all

    x = p_ref[:, RGX_OFF:RGX_OFF + RG_WIDTH]
    xc = rcb_ref[...] + rcw_ref[CONV_W - 1:CONV_W, :] * x
    for k in range(CONV_W - 1):
        xc = xc + rcw_ref[k:k + 1, :] * rc_ref[k]
        rc_out[k] = x if k == CONV_W - 2 else rc_ref[k + 1]
    a, mult, ig = _rg_gates(xc, wa_ref, ba_ref, wx_ref, bx_ref, lam_ref)
    h = a * rh_ref[...] + mult * ig * xc
    rh_out[...] = h
    org_ref[...] = (h * _gelu_tanh(p_ref[:, RGG_OFF:RGG_OFF + RG_WIDTH])).astype(BF16)

    z = p_ref[:, SSZ_OFF:SSZ_OFF + SSM_WIDTH]
    xbc_raw = p_ref[:, SSX_OFF:SSX_OFF + SSM_CONV_DIM]
    dt_raw = p_ref[:, SSDT_OFF:SSDT_OFF + LANES]
    xbc = scb_ref[...] + scw_ref[CONV_W - 1:CONV_W, :] * xbc_raw
    for k in range(CONV_W - 1):
        xbc = xbc + scw_ref[k:k + 1, :] * sc_ref[k]
        sc_out[k] = xbc_raw if k == CONV_W - 2 else sc_ref[k + 1]
    xbc = _silu(xbc)
    xs = xbc[:, :SSM_WIDTH]
    gs = SSM_GROUPS * SSM_DSTATE
    b_sc[...] = xbc[:, SSM_WIDTH:SSM_WIDTH + gs]
    c_sc[...] = xbc[:, SSM_WIDTH + gs:]
    dt = _softplus(dt_raw + dtb_ref[...])
    parts = jnp.concatenate(_split3(dt), axis=0)
    r = jnp.dot(parts, expand_ref[...], preferred_element_type=F32)
    dt_exp = r[0:SB] + r[SB:2 * SB] + r[2 * SB:]
    xdt_sc[...] = xs * dt_exp
    adec_sc[...] = jnp.exp(dt_exp * aexp_ref[...])

    hg_tile = functools.partial(_hg_tile, p_ref)
    for hd in range(HG_HEADS):
        sl = slice(hd * LANES, (hd + 1) * LANES)
        lb = lb_ref[:, sl]
        fz = hg_tile(HGF_OFF, hd)
        f_sc[:, sl] = lb + (1.0 - lb) * jax.nn.sigmoid(fz)
        kk_sc[:, sl] = (1.0 - lb) * jax.nn.sigmoid(-fz)
        q_sc[:, sl] = _silu(hg_tile(HGQ_OFF, hd))

    for j in range(SB):
        row = slice(j, j + 1)
        for hd in range(HG_HEADS):
            sl = slice(hd * LANES, (hd + 1) * LANES)
            v_row = hg_tile(HGV_OFF, hd, row)
            s_new = _col_bcast(f_sc[row, sl]) * hs_ref[j, hd] + _col_bcast(kk_sc[row, sl]) * v_row
            hs_out[j, hd] = s_new
            q8 = jnp.broadcast_to(q_sc[row, sl], (SB, LANES)).astype(BF16)
            o_sc[row, sl] = jnp.dot(q8, s_new.astype(BF16), preferred_element_type=F32)[0:1]
        for hp in range(SSM_HEADS // 2):
            sl = slice(hp * LANES, (hp + 1) * LANES)
            g = (2 * hp) // HEADS_PER_GROUP
            gsl = slice(g * SSM_DSTATE, (g + 1) * SSM_DSTATE)
            s_old = ss_ref[j, 2 * hp:2 * hp + 2].reshape(LANES, SSM_DSTATE)
            s_new = _col_bcast(adec_sc[row, sl]) * s_old + _col_bcast(xdt_sc[row, sl]) * b_sc[row, gsl]
            ss_out[j, 2 * hp:2 * hp + 2] = s_new.reshape(2, SSM_HEADDIM, SSM_DSTATE)
            c8 = jnp.broadcast_to(c_sc[row, gsl], (SB, SSM_DSTATE)).astype(BF16)
            y_sc[row, sl] = lax.dot_general(c8, s_new.astype(BF16), _NT, preferred_element_type=F32)[0:1]

    for hd in range(HG_HEADS):
        sl = slice(hd * LANES, (hd + 1) * LANES)
        o = o_sc[:, sl]
        o = o * lax.rsqrt(jnp.mean(o * o, axis=-1, keepdims=True) + EPS)
        ohg_ref[:, sl] = (o * hnw_ref[:, sl] * _silu(hg_tile(HGG_OFF, hd))).astype(BF16)
    y = (y_sc[...] + dexp_ref[...] * xs) * _silu(z)
    gw = SSM_WIDTH // SSM_GROUPS
    for g in range(SSM_GROUPS):
        yg = y[:, g * gw:(g + 1) * gw]
        yg = yg * lax.rsqrt(jnp.mean(yg * yg, axis=-1, keepdims=True) + EPS)
        ossm_ref[:, g * gw:(g + 1) * gw] = (yg * snw_ref[:, g * gw:(g + 1) * gw]).astype(BF16)


def _sample_mix(proj, states, layer, p, lb, mp, prev=()):
    hg_s, rg_h, rg_c_t, ssm_s, ssm_c_t = states
    bs = hg_s.shape[1]
    rb = mp // SB
    const2 = lambda i: (0, 0)
    const3 = lambda i: (0, 0, 0)
    row = lambda a, w: a.reshape(1, w)
    dtb = jnp.pad(p['ssm_dt_bias'].reshape(1, SSM_HEADS), ((0, 0), (0, LANES - SSM_HEADS)))
    aexp = jnp.repeat(-jnp.exp(p['ssm_a_log']), SSM_HEADDIM).reshape(1, SSM_WIDTH)
    dexp = jnp.repeat(p['ssm_d'], SSM_HEADDIM).reshape(1, SSM_WIDTH)
    expand = _head_to_lanes()
    in_specs = [
        pl.BlockSpec((SB, D_IN_PAD), lambda i: (rb + i, 0)),
        pl.BlockSpec((None, SB, HG_HEADS, HG_DK, HG_DV), lambda i: (layer, i, 0, 0, 0)),
        pl.BlockSpec((None, SB, RG_WIDTH), lambda i: (layer, i, 0)),
        pl.BlockSpec((None, CONV_W - 1, SB, RG_WIDTH), lambda i: (layer, 0, i, 0)),
        pl.BlockSpec((None, SB, SSM_HEADS, SSM_HEADDIM, SSM_DSTATE), lambda i: (layer, i, 0, 0, 0)),
        pl.BlockSpec((None, CONV_W - 1, SB, SSM_CONV_DIM), lambda i: (layer, 0, i, 0)),
        pl.BlockSpec((1, HG_WIDTH), const2),
        pl.BlockSpec((1, HG_WIDTH), const2),
        pl.BlockSpec((CONV_W, RG_WIDTH), const2),
        pl.BlockSpec((1, RG_WIDTH), const2),
        pl.BlockSpec((RG_HEADS, RG_HEAD_DIM, RG_HEAD_DIM), const3),
        pl.BlockSpec((1, RG_WIDTH), const2),
        pl.BlockSpec((RG_HEADS, RG_HEAD_DIM, RG_HEAD_DIM), const3),
        pl.BlockSpec((1, RG_WIDTH), const2),
        pl.BlockSpec((1, RG_WIDTH), const2),
        pl.BlockSpec((CONV_W, SSM_CONV_DIM), const2),
        pl.BlockSpec((1, SSM_CONV_DIM), const2),
        pl.BlockSpec((1, LANES), const2),
        pl.BlockSpec((1, SSM_WIDTH), const2),
        pl.BlockSpec((1, SSM_WIDTH), const2),
        pl.BlockSpec((1, SSM_WIDTH), const2),
        pl.BlockSpec((LANES, SSM_WIDTH), const2),
    ]
    hg_blk, ssm_blk = (SB, HG_HEADS, HG_DK, HG_DV), (SB, SSM_HEADS, SSM_HEADDIM, SSM_DSTATE)
    n_prev = len(prev)
    for _ in prev:
        in_specs += [pl.BlockSpec(hg_blk, lambda i: (i, 0, 0, 0)), pl.BlockSpec(ssm_blk, lambda i: (i, 0, 0, 0))]
    stacked = lambda blk: (pl.BlockSpec((n_prev + 1,) + blk, lambda i: (0, i, 0, 0, 0)) if n_prev
                           else pl.BlockSpec(blk, lambda i: (i, 0, 0, 0)))
    lead = (n_prev + 1,) if n_prev else ()
    out_specs = [
        pl.BlockSpec((SB, HG_WIDTH), lambda i: (i, 0)),
        pl.BlockSpec((SB, RG_WIDTH), lambda i: (i, 0)),
        pl.BlockSpec((SB, SSM_WIDTH), lambda i: (i, 0)),
        stacked(hg_blk),
        pl.BlockSpec((SB, RG_WIDTH), lambda i: (i, 0)),
        pl.BlockSpec((CONV_W - 1, SB, RG_WIDTH), lambda i: (0, i, 0)),
        stacked(ssm_blk),
        pl.BlockSpec((CONV_W - 1, SB, SSM_CONV_DIM), lambda i: (0, i, 0)),
    ]
    out_shape = [
        jax.ShapeDtypeStruct((bs, HG_WIDTH), BF16),
        jax.ShapeDtypeStruct((bs, RG_WIDTH), BF16),
        jax.ShapeDtypeStruct((bs, SSM_WIDTH), BF16),
        jax.ShapeDtypeStruct(lead + hg_s.shape[1:], F32),
        jax.ShapeDtypeStruct(rg_h.shape[1:], F32),
        jax.ShapeDtypeStruct(rg_c_t.shape[1:], F32),
        jax.ShapeDtypeStruct(lead + ssm_s.shape[1:], F32),
        jax.ShapeDtypeStruct(ssm_c_t.shape[1:], F32),
    ]
    outs = pl.pallas_call(
        functools.partial(_sample_kernel, n_prev=n_prev),
        grid=(bs // SB,),
        in_specs=in_specs,
        out_specs=out_specs,
        out_shape=out_shape,
        scratch_shapes=[
            pltpu.VMEM((SB, HG_WIDTH), F32), pltpu.VMEM((SB, HG_WIDTH), F32), pltpu.VMEM((SB, HG_WIDTH), F32),
            pltpu.VMEM((SB, HG_WIDTH), F32),
            pltpu.VMEM((SB, SSM_WIDTH), F32), pltpu.VMEM((SB, SSM_WIDTH), F32),
            pltpu.VMEM((SB, SSM_GROUPS * SSM_DSTATE), F32), pltpu.VMEM((SB, SSM_GROUPS * SSM_DSTATE), F32),
            pltpu.VMEM((SB, SSM_WIDTH), F32),
        ],
        compiler_params=pltpu.CompilerParams(
            dimension_semantics=("parallel",), vmem_limit_bytes=VMEM_LIMIT),
        name="sample_mix",
    )(proj, hg_s, rg_h, rg_c_t, ssm_s, ssm_c_t,
      lb.reshape(1, HG_WIDTH), row(p['hg_norm_w'], HG_WIDTH),
      p['rg_conv_w'], row(p['rg_conv_b'], RG_WIDTH), p['rg_wa'].astype(BF16), row(p['rg_ba'], RG_WIDTH),
      p['rg_wx'].astype(BF16), row(p['rg_bx'], RG_WIDTH), row(p['rg_lambda'], RG_WIDTH),
      p['ssm_conv_w'], row(p['ssm_conv_b'], SSM_CONV_DIM), dtb, aexp, dexp, row(p['ssm_norm_w'], SSM_WIDTH),
      expand, *(a for pair in prev for a in pair))
    return tuple(outs[:3]), tuple(outs[3:])


def kernel(x_prompt, x_sample, state_hgrn, state_rglru, state_rglru_conv, state_ssm, state_ssm_conv, norm_g, ffn1_w_gate, ffn1_w_up, ffn1_w_down, ffn2_w_gate, ffn2_w_up, ffn2_w_down, w_in, w_out, hg_lb_logits, hg_norm_w, rg_conv_w, rg_conv_b, rg_wa, rg_ba, rg_wx, rg_bx, rg_lambda, ssm_conv_w, ssm_conv_b, ssm_dt_bias, ssm_a_log, ssm_d, ssm_norm_w):
    bp, lp, _ = x_prompt.shape
    bs, ls, _ = x_sample.shape
    mp = bp * lp
    ms = bs * ls
    assert (mp + ms) % IN_ROWS == 0 and D_FF % FF_TILE == 0 and D_IN_PAD % IN_TILE == 0

    lw = {
        'hg_norm_w': hg_norm_w, 'rg_conv_w': rg_conv_w, 'rg_conv_b': rg_conv_b, 'rg_wa': rg_wa,
        'rg_ba': rg_ba, 'rg_wx': rg_wx, 'rg_bx': rg_bx, 'rg_lambda': rg_lambda,
        'ssm_conv_w': ssm_conv_w, 'ssm_conv_b': ssm_conv_b, 'ssm_dt_bias': ssm_dt_bias,
        'ssm_a_log': ssm_a_log, 'ssm_d': ssm_d, 'ssm_norm_w': ssm_norm_w,
    }
    lb_cum = jnp.cumsum(jax.nn.softmax(hg_lb_logits.astype(F32), axis=0), axis=0)
    lower_bounds = lb_cum - lb_cum[:1]

    taps_first = lambda a: jnp.transpose(a, (0, 2, 1, 3))
    sample_init = (state_hgrn, state_rglru, taps_first(state_rglru_conv), state_ssm, taps_first(state_ssm_conv))

    w_in_b = jnp.pad(w_in.astype(BF16), ((0, 0), (0, 0), (0, D_IN_PAD - D_IN_PROJ)))
    x = (x_prompt.reshape(mp, D_MODEL), x_sample.reshape(ms, D_MODEL))
    new_p = ([], [], [], [], [])
    new_s = ([], [], [], [], [])
    for l in range(DEPTH):
        g = norm_g[l].reshape(6, 1, D_MODEL)
        p = {name: arr[l] for name, arr in lw.items()}
        x = _ffn(x, g[0], g[1], ffn1_w_gate, ffn1_w_up, ffn1_w_down, l, mp, ms, split_in=(l == 0))
        proj = _inproj(x, g[2], w_in_b, l)
        o_hg, hg_new, o_rg, rg_h_new, rg_c_new, o_ssm, ssm_new, ssm_c_new = _prompt_mix(
            proj, p, lower_bounds[l], bp, lp)
        st_p = (hg_new, rg_h_new.reshape(bp, RG_WIDTH), rg_c_new, ssm_new, ssm_c_new)
        last = l == DEPTH - 1
        prev = tuple(zip(new_s[0], new_s[3])) if last else ()
        o_s, st_s = _sample_mix(proj, sample_init, l, p, lower_bounds[l], mp, prev)
        x = _outproj(x, (o_hg, o_rg, o_ssm), o_s, g[3], w_out, l)
        x = _ffn(x, g[4], g[5], ffn2_w_gate, ffn2_w_up, ffn2_w_down, l, mp, ms, split_out=last)
        for acc, s in zip(new_p, st_p):
            acc.append(s)
        for acc, s in zip(new_s, st_s):
            acc.append(s)
    hg_p, rg_p, rgc_p, ssm_p, ssmc_p = (jnp.stack(a) for a in new_p)
    hg_s, ssm_s = new_s[0][-1], new_s[3][-1]
    rg_s, rgc_s, ssmc_s = (jnp.stack(new_s[k]) for k in (1, 2, 4))
    rgc_s, ssmc_s = taps_first(rgc_s), taps_first(ssmc_s)
    y_prompt = x[0].reshape(bp, lp, D_MODEL)
    y_sample = x[1].reshape(bs, ls, D_MODEL)
    return (y_prompt, y_sample, hg_p, hg_s, rg_p, rg_s, rgc_p, rgc_s, ssm_p, ssm_s, ssmc_p, ssmc_s)
```

```python
import functools
import math

import jax
import jax.numpy as jnp
import numpy as np
from jax import lax
from jax.experimental import pallas as pl
from jax.experimental.pallas import tpu as pltpu

F32 = jnp.float32
BF16 = jnp.bfloat16

D_MODEL = 2048
DEPTH = 2
EPS = 1e-6
CONV_W = 4
HG_HEADS = 4
HG_DK = 128
HG_DV = 128
HG_WIDTH = HG_HEADS * HG_DV
HG_CHUNK = 32
RG_HEADS = 6
RG_HEAD_DIM = 128
RG_WIDTH = RG_HEADS * RG_HEAD_DIM
RG_C = 8.0
SSM_HEADS = 12
SSM_HEADDIM = 64
SSM_WIDTH = SSM_HEADS * SSM_HEADDIM
SSM_GROUPS = 2
HEADS_PER_GROUP = SSM_HEADS // SSM_GROUPS
SSM_DSTATE = 128
SSM_CHUNK = 64
SSM_CONV_DIM = SSM_WIDTH + 2 * SSM_GROUPS * SSM_DSTATE
D_MIX = HG_WIDTH + RG_WIDTH + SSM_WIDTH
IN_SIZES = (HG_HEADS * HG_DK, HG_HEADS * HG_DK, HG_WIDTH, HG_WIDTH, RG_WIDTH, RG_WIDTH,
            SSM_WIDTH, SSM_CONV_DIM, SSM_HEADS)
D_IN_PROJ = sum(IN_SIZES)
D_FF = 5632

LANES = 128
D_IN_PAD = -(-D_IN_PROJ // LANES) * LANES
FFN_ROWS = 1040
FF_TILE = 256
IN_ROWS = 640
IN_TILE = 1920
VMEM_LIMIT = 56 * 1024 * 1024
K_CHUNK = 1024
NORM_ROWS = 16
FFN_VMEM_LIMIT = 62 * 1024 * 1024


def _rms(x, g):
    return x * lax.rsqrt(jnp.mean(x * x, axis=-1, keepdims=True) + EPS) * g


def _ffn_kernel(*refs, split_in, split_out, n_prompt_last):
    refs = list(refs)
    x_ref = refs.pop(0)
    xs_ref = refs.pop(0) if split_in else None
    gin_ref, gout_ref, wg_ref, wu_ref, wd_ref = (refs.pop(0) for _ in range(5))
    o_ref = refs.pop(0)
    ys_ref = refs.pop(0) if split_out else None
    xn_ref = refs.pop(0)
    i, j = pl.program_id(0), pl.program_id(1)
    last_i = pl.num_programs(0) - 1
    npl = n_prompt_last

    @pl.when(j == 0)
    def _():
        o_ref[...] = jnp.zeros_like(o_ref)
        if split_in:
            @pl.when(i < last_i)
            def _():
                xn_ref[...] = _rms(x_ref[...], gin_ref[...]).astype(BF16)

            @pl.when(i == last_i)
            def _():
                xn_ref[0:npl, :] = _rms(x_ref[0:npl, :], gin_ref[...]).astype(BF16)
                xn_ref[npl:, :] = _rms(xs_ref[...], gin_ref[...]).astype(BF16)
        else:
            xn_ref[...] = _rms(x_ref[...], gin_ref[...]).astype(BF16)

    g = u = None
    for k in range(0, D_MODEL, K_CHUNK):
        xk = xn_ref[:, k:k + K_CHUNK]
        gk = jnp.dot(xk, wg_ref[k:k + K_CHUNK, :].astype(BF16), preferred_element_type=F32)
        uk = jnp.dot(xk, wu_ref[k:k + K_CHUNK, :].astype(BF16), preferred_element_type=F32)
        g, u = (gk, uk) if g is None else (g + gk, u + uk)
    h = (g * jax.nn.sigmoid(g) * u).astype(BF16)
    o_ref[...] += jnp.dot(h, wd_ref[...].astype(BF16), preferred_element_type=F32)

    def residual_out(src_ref, src0, dst0, n_rows):
        for r in range(0, n_rows, NORM_ROWS):
            dst = slice(dst0 + r, dst0 + r + NORM_ROWS)
            o_ref[dst, :] = (src_ref[src0 + r:src0 + r + NORM_ROWS, :]
                             + 0.5 * _rms(o_ref[dst, :], gout_ref[...]))

    @pl.when(j == pl.num_programs(1) - 1)
    def _():
        if split_in:
            @pl.when(i < last_i)
            def _():
                residual_out(x_ref, 0, 0, FFN_ROWS)

            @pl.when(i == last_i)
            def _():
                residual_out(x_ref, 0, 0, npl)
                residual_out(xs_ref, 0, npl, FFN_ROWS - npl)
        else:
            residual_out(x_ref, 0, 0, FFN_ROWS)
        if split_out:
            @pl.when(i == last_i)
            def _():
                ys_ref[...] = o_ref[npl:, :]


def _ffn(x, g_in, g_out, wg, wu, wd, layer, mp, ms, split_in=False, split_out=False):
    m = mp + ms
    n_tiles = m // FFN_ROWS
    n_prompt_last = mp - (n_tiles - 1) * FFN_ROWS
    assert m % FFN_ROWS == 0 and n_prompt_last + ms == FFN_ROWS and n_prompt_last % 16 == 0
    rows = pl.BlockSpec((FFN_ROWS, D_MODEL), lambda i, j: (i, 0))
    sample = pl.BlockSpec((ms, D_MODEL), lambda i, j: (0, 0))
    vec = pl.BlockSpec((1, D_MODEL), lambda i, j: (0, 0))
    xs = tuple(x) if split_in else (x,)
    if split_out:
        out_specs = [rows, sample]
        out_shape = [jax.ShapeDtypeStruct((mp, D_MODEL), F32), jax.ShapeDtypeStruct((ms, D_MODEL), F32)]
    else:
        out_specs = rows
        out_shape = jax.ShapeDtypeStruct((m, D_MODEL), F32)
    return pl.pallas_call(
        functools.partial(_ffn_kernel, split_in=split_in, split_out=split_out, n_prompt_last=n_prompt_last),
        grid=(n_tiles, D_FF // FF_TILE),
        in_specs=[rows] + ([sample] if split_in else []) + [
            vec, vec,
            pl.BlockSpec((None, D_MODEL, FF_TILE), lambda i, j: (layer, 0, j)),
            pl.BlockSpec((None, D_MODEL, FF_TILE), lambda i, j: (layer, 0, j)),
            pl.BlockSpec((None, FF_TILE, D_MODEL), lambda i, j: (layer, j, 0)),
        ],
        out_specs=out_specs,
        out_shape=out_shape,
        scratch_shapes=[pltpu.VMEM((FFN_ROWS, D_MODEL), BF16)],
        compiler_params=pltpu.CompilerParams(
            dimension_semantics=("parallel", "arbitrary"), vmem_limit_bytes=FFN_VMEM_LIMIT),
        name="ffn",
    )(*xs, g_in, g_out, wg, wu, wd)


def _inproj_kernel(x_ref, g_ref, w_ref, o_ref, xn_ref):
    @pl.when(pl.program_id(1) == 0)
    def _():
        xn_ref[...] = _rms(x_ref[...], g_ref[...]).astype(BF16)

    o_ref[...] = jnp.dot(xn_ref[...], w_ref[...], preferred_element_type=F32)


def _inproj(x, g, w_in, layer):
    m = x.shape[0]
    return pl.pallas_call(
        _inproj_kernel,
        grid=(m // IN_ROWS, D_IN_PAD // IN_TILE),
        in_specs=[
            pl.BlockSpec((IN_ROWS, D_MODEL), lambda i, j: (i, 0)),
            pl.BlockSpec((1, D_MODEL), lambda i, j: (0, 0)),
            pl.BlockSpec((None, D_MODEL, IN_TILE), lambda i, j: (layer, 0, j)),
        ],
        out_specs=pl.BlockSpec((IN_ROWS, IN_TILE), lambda i, j: (i, j)),
        out_shape=jax.ShapeDtypeStruct((m, D_IN_PAD), F32),
        scratch_shapes=[pltpu.VMEM((IN_ROWS, D_MODEL), BF16)],
        compiler_params=pltpu.CompilerParams(
            dimension_semantics=("parallel", "arbitrary"), vmem_limit_bytes=VMEM_LIMIT),
        name="inproj",
    )(x, g, w_in)


OUT_TILE = 512


def _outproj_kernel(x_ref, php_ref, prg_ref, pss_ref, shg_ref, srg_ref, sss_ref, g_ref, w_ref, y_ref, wb_ref):
    i = pl.program_id(0)
    last = pl.num_programs(0) - 1
    r0, r1 = HG_WIDTH, HG_WIDTH + RG_WIDTH

    @pl.when(i == 0)
    def _():
        wb_ref[...] = w_ref[...].astype(BF16)

    def mixed(ohg, org, oss):
        m = jnp.dot(ohg, wb_ref[0:r0, :], preferred_element_type=F32)
        m += jnp.dot(org, wb_ref[r0:r1, :], preferred_element_type=F32)
        m += jnp.dot(oss, wb_ref[r1:, :], preferred_element_type=F32)
        return _rms(m, g_ref[...])

    @pl.when(i < last)
    def _():
        y_ref[...] = x_ref[...] + mixed(php_ref[...], prg_ref[...], pss_ref[...])

    @pl.when(i == last)
    def _():
        ns = shg_ref.shape[0]
        y_ref[0:ns, :] = x_ref[0:ns, :] + mixed(shg_ref[...], srg_ref[...], sss_ref[...])


def _outproj(x, o_prompt, o_sample, g, w_out, layer):
    mp, ms = o_prompt[0].shape[0], o_sample[0].shape[0]
    assert mp % OUT_TILE == 0 and ms <= OUT_TILE and x.shape[0] == mp + ms
    n_p = mp // OUT_TILE
    widths = (HG_WIDTH, RG_WIDTH, SSM_WIDTH)
    return pl.pallas_call(
        _outproj_kernel,
        grid=(n_p + 1,),
        in_specs=[pl.BlockSpec((OUT_TILE, D_MODEL), lambda i: (i, 0))]
        + [pl.BlockSpec((OUT_TILE, w), lambda i: (jnp.minimum(i, n_p - 1), 0)) for w in widths]
        + [pl.BlockSpec((ms, w), lambda i: (0, 0)) for w in widths]
        + [pl.BlockSpec((1, D_MODEL), lambda i: (0, 0)),
           pl.BlockSpec((None, D_MIX, D_MODEL), lambda i: (layer, 0, 0), pipeline_mode=pl.Buffered(1))],
        out_specs=pl.BlockSpec((OUT_TILE, D_MODEL), lambda i: (i, 0)),
        out_shape=jax.ShapeDtypeStruct((mp + ms, D_MODEL), F32),
        scratch_shapes=[pltpu.VMEM((D_MIX, D_MODEL), BF16)],
        compiler_params=pltpu.CompilerParams(
            dimension_semantics=("arbitrary",), vmem_limit_bytes=VMEM_LIMIT),
        name="outproj",
    )(x, *o_prompt, *o_sample, g, w_out)


_COL = np.cumsum((0,) + IN_SIZES)
HGQ_OFF, HGF_OFF, HGV_OFF, HGG_OFF, RGX_OFF, RGG_OFF, SSZ_OFF, SSX_OFF, SSDT_OFF = (int(c) for c in _COL[:9])
PIECE = 256
assert all(off % PIECE == 0 for off in (RGX_OFF, RGG_OFF, SSZ_OFF, SSX_OFF)) and SSDT_OFF % LANES == 0
assert all(w % PIECE == 0 for w in (RG_WIDTH, SSM_WIDTH, SSM_CONV_DIM))


def _pieces(off, width, rows, row_map):
    return [pl.BlockSpec((rows, PIECE), functools.partial(lambda k, *g: (row_map(*g), k), off // PIECE + k))
            for k in range(width // PIECE)]


def _cat(refs, rows=slice(None)):
    return jnp.concatenate([r[rows, :] for r in refs], axis=1)


SUBLANES = 8
SSD_T = 256
NEG_BIG = -1e30


def _split3(x):
    hi = x.astype(BF16)
    r = x - hi.astype(F32)
    mid = r.astype(BF16)
    lo = (r - mid.astype(F32)).astype(BF16)
    return hi, mid, lo


def _cumsum_rows(tri, x):
    w = x.shape[1]
    parts = jnp.concatenate(_split3(x), axis=1)
    r = jnp.dot(tri, parts, preferred_element_type=F32)
    return r[:, :w] + r[:, w:2 * w] + r[:, 2 * w:]


def _silu(x):
    return x * jax.nn.sigmoid(x)


def _neg_expm1(x, exp_x):
    series = -x * (1.0 + x * (1 / 2 + x * (1 / 6 + x * (1 / 24 + x * (1 / 120)))))
    return jnp.where(x > -1 / 16, series, 1.0 - exp_x)


def _softplus(x):
    return jnp.maximum(x, 0.0) + jnp.log(1.0 + jnp.exp(-jnp.abs(x)))


def _conv4(xp_ref, tail_ref, x, w_ref, b_ref):
    t = x.shape[0]
    xp_ref[0:SUBLANES, :] = tail_ref[...]
    xp_ref[SUBLANES:, :] = x
    tail_ref[...] = x[t - SUBLANES:, :]
    y = b_ref[...] + w_ref[CONV_W - 1:CONV_W, :] * x
    for k in range(CONV_W - 1):
        off = SUBLANES - (CONV_W - 1) + k
        y = y + w_ref[k:k + 1, :] * xp_ref[off:off + t, :]
    return y


N_Z, N_XBC = SSM_WIDTH // PIECE, SSM_CONV_DIM // PIECE


SSD_N_IN = N_Z + N_XBC + 9
SSD_N_OUT, SSD_N_SCRATCH = 3, 4


def _ssd_prompt_body(ins, outs, scratch):
    z_refs, xbc_refs = ins[:N_Z], ins[N_Z:N_Z + N_XBC]
    dt_ref, tri_ref, expand_ref, cw_ref, cb_ref, dtb_ref, alog_ref, dexp_ref, nw_ref = ins[N_Z + N_XBC:]
    o_ref, _, cst_ref = outs
    xp_ref, tail_ref, s_ref, y_ref = scratch
    t = SSD_T
    z = _cat(z_refs)
    xbc_raw = _cat(xbc_refs)
    dt_raw = dt_ref[...]

    xbc = _silu(_conv4(xp_ref, tail_ref, xbc_raw, cw_ref, cb_ref))
    cst_ref[...] = _cat(xbc_refs, slice(t - (CONV_W - 1), t))
    xs = xbc[:, :SSM_WIDTH]
    gs = SSM_GROUPS * SSM_DSTATE
    bm = xbc[:, SSM_WIDTH:SSM_WIDTH + gs].astype(BF16)
    cm = xbc[:, SSM_WIDTH + gs:].astype(BF16)

    dt = _softplus(dt_raw + dtb_ref[...])
    a = dt * -jnp.exp(alog_ref[...])
    cs = _cumsum_rows(tri_ref[...], a)
    cs_t = cs.T
    cs_last = cs[t - 1:t, :]
    e_last = jnp.exp(cs_last)
    row = lax.broadcasted_iota(jnp.int32, (t, t), 0)
    col = lax.broadcasted_iota(jnp.int32, (t, t), 1)
    causal = row >= col

    def per_lane(v):
        parts = jnp.concatenate(_split3(v), axis=0)
        r = jnp.dot(parts, expand_ref[...], preferred_element_type=F32)
        return r[0:t] + r[t:2 * t] + r[2 * t:]

    dt_w = per_lane(dt)
    cs_w = per_lane(cs)
    e_cs_w = jnp.exp(cs_w)
    xdt = xs * dt_w
    xdt_b = xdt.astype(BF16)
    w_b = (xdt * jnp.exp(cs_w[t - 1:t, :] - cs_w)).astype(BF16)
    pair_lane = lax.broadcasted_iota(jnp.int32, (t, LANES), 1) < SSM_HEADDIM
    gw = SSM_WIDTH // SSM_GROUPS

    for g in range(SSM_GROUPS):
        heads = range(g * HEADS_PER_GROUP, (g + 1) * HEADS_PER_GROUP)
        bg = bm[:, g * SSM_DSTATE:(g + 1) * SSM_DSTATE]
        cg = cm[:, g * SSM_DSTATE:(g + 1) * SSM_DSTATE]
        cb = lax.dot_general(cg, bg, _NT, preferred_element_type=F32)
        s_prev = s_ref[heads.start:heads.stop].reshape(gw, SSM_DSTATE)
        y_in = lax.dot_general(cg, s_prev.astype(BF16), _NT, preferred_element_type=F32)
        st = lax.dot_general(w_b[:, g * gw:(g + 1) * gw], bg, _TN, preferred_element_type=F32)
        decay = jnp.concatenate([jnp.broadcast_to(e_last[:, h:h + 1], (SSM_HEADDIM, SSM_DSTATE)) for h in heads],
                                axis=0)
        s_ref[heads.start:heads.stop] = (decay * s_prev + st).reshape(HEADS_PER_GROUP, SSM_HEADDIM, SSM_DSTATE)

        for k in range(HEADS_PER_GROUP // 2):
            lanes = slice(g * gw + k * LANES, g * gw + (k + 1) * LANES)
            res = []
            for h in (heads.start + 2 * k, heads.start + 2 * k + 1):
                seg = jnp.where(causal, cs[:, h:h + 1] - cs_t[h:h + 1, :], NEG_BIG)
                scores = (cb * jnp.exp(seg)).astype(BF16)
                res.append(jnp.dot(scores, xdt_b[:, lanes], preferred_element_type=F32))
            y_ref[:, lanes] = (jnp.where(pair_lane, res[0], res[1])
                               + e_cs_w[:, lanes] * y_in[:, k * LANES:(k + 1) * LANES])

    y = (y_ref[...] + dexp_ref[...] * xs) * _silu(z)
    gw = SSM_WIDTH // SSM_GROUPS
    for g in range(SSM_GROUPS):
        yg = y[:, g * gw:(g + 1) * gw]
        yg = yg * lax.rsqrt(jnp.mean(yg * yg, axis=-1, keepdims=True) + EPS)
        o_ref[:, g * gw:(g + 1) * gw] = (yg * nw_ref[:, g * gw:(g + 1) * gw]).astype(BF16)


def _tri(t):
    return jnp.tril(jnp.ones((t, t), F32)).astype(BF16)


def _head_to_lanes():
    return jnp.asarray(np.arange(LANES)[:, None] == (np.arange(SSM_WIDTH)[None, :] // SSM_HEADDIM), BF16)


def _ssd_prompt_parts(proj, p, bp, lp):
    nc = lp // SSD_T
    full = lambda b, n: (0, 0)
    rows = lambda b, n: b * nc + n
    dtb = jnp.pad(p['ssm_dt_bias'].reshape(1, SSM_HEADS), ((0, 0), (0, LANES - SSM_HEADS)))
    alog = jnp.pad(p['ssm_a_log'].reshape(1, SSM_HEADS), ((0, 0), (0, LANES - SSM_HEADS)))
    dexp = jnp.repeat(p['ssm_d'], SSM_HEADDIM).reshape(1, SSM_WIDTH)
    return dict(
        in_specs=_pieces(SSZ_OFF, SSM_WIDTH, SSD_T, rows) + _pieces(SSX_OFF, SSM_CONV_DIM, SSD_T, rows) + [
            pl.BlockSpec((SSD_T, LANES), lambda b, n: (rows(b, n), SSDT_OFF // LANES)),
            pl.BlockSpec((SSD_T, SSD_T), full),
            pl.BlockSpec((LANES, SSM_WIDTH), full),
            pl.BlockSpec((CONV_W, SSM_CONV_DIM), full),
            pl.BlockSpec((1, SSM_CONV_DIM), full),
            pl.BlockSpec((1, LANES), full),
            pl.BlockSpec((1, LANES), full),
            pl.BlockSpec((1, SSM_WIDTH), full),
            pl.BlockSpec((1, SSM_WIDTH), full),
        ],
        out_specs=[
            pl.BlockSpec((SSD_T, SSM_WIDTH), lambda b, n: (b * nc + n, 0)),
            pl.BlockSpec((None, SSM_HEADS, SSM_HEADDIM, SSM_DSTATE), lambda b, n: (b, 0, 0, 0)),
            pl.BlockSpec((None, CONV_W - 1, SSM_CONV_DIM), lambda b, n: (b, 0, 0)),
        ],
        out_shape=[
            jax.ShapeDtypeStruct((bp * lp, SSM_WIDTH), BF16),
            jax.ShapeDtypeStruct((bp, SSM_HEADS, SSM_HEADDIM, SSM_DSTATE), F32),
            jax.ShapeDtypeStruct((bp, CONV_W - 1, SSM_CONV_DIM), F32),
        ],
        scratch_shapes=[
            pltpu.VMEM((SSD_T + SUBLANES, SSM_CONV_DIM), F32),
            pltpu.VMEM((SUBLANES, SSM_CONV_DIM), F32),
            pltpu.VMEM((SSM_HEADS, SSM_HEADDIM, SSM_DSTATE), F32),
            pltpu.VMEM((SSD_T, SSM_WIDTH), F32),
        ],
        args=[proj] * (N_Z + N_XBC + 1) + [
            _tri(SSD_T), _head_to_lanes(), p['ssm_conv_w'], p['ssm_conv_b'].reshape(1, SSM_CONV_DIM), dtb, alog,
            dexp, p['ssm_norm_w'].reshape(1, SSM_WIDTH)],
    )


HG_T = 256

_NT = (((1,), (1,)), ((), ()))
_TN = (((0,), (0,)), ((), ()))


HG_COLS = 4 * HG_WIDTH
assert (HGQ_OFF, HGF_OFF, HGV_OFF, HGG_OFF) == (0, HG_WIDTH, 2 * HG_WIDTH, 3 * HG_WIDTH)


def _hg_tile(p_ref, off, head, rows=slice(None)):
    return p_ref[rows, off + head * LANES:off + (head + 1) * LANES]


HG_N_IN, HG_N_OUT, HG_N_SCRATCH = 4, 2, 1


def _hgrn_prompt_body(ins, outs, scratch):
    hg_ref, lb_ref, nw_ref, tri_ref = ins
    o_ref, _ = outs
    s_ref, = scratch
    t, c = HG_T, HG_CHUNK
    nch = t // c

    row = lax.broadcasted_iota(jnp.int32, (t, t), 0)
    col = lax.broadcasted_iota(jnp.int32, (t, t), 1)
    keep = (row >= col) & (row // c == col // c)
    tri = tri_ref[...]

    v, o_intra, kd, qe, decay = [], [], [], [], []
    for h in range(HG_HEADS):
        sl = slice(h * LANES, (h + 1) * LANES)
        lb = lb_ref[:, sl]
        fz = _hg_tile(hg_ref, HGF_OFF, h)
        logf = jnp.log(lb + (1.0 - lb) * jax.nn.sigmoid(fz))
        kk3 = ((1.0 - lb) * jax.nn.sigmoid(-fz)).reshape(nch, c, HG_DK)
        qh3 = _silu(_hg_tile(hg_ref, HGQ_OFF, h)).reshape(nch, c, HG_DK)
        vh = _hg_tile(hg_ref, HGV_OFF, h).astype(BF16)
        b3 = _cumsum_rows(tri, logf).reshape(nch, c, HG_DK)
        b_mid = b3[:, c // 2:c // 2 + 1, :]
        b_last = b3[:, c - 1:c, :]
        q_in = (qh3 * jnp.exp(b3 - b_mid)).reshape(t, HG_DK).astype(BF16)
        k_in = (kk3 * jnp.exp(b_mid - b3)).reshape(t, HG_DK).astype(BF16)
        a = jnp.where(keep, lax.dot_general(q_in, k_in, _NT, preferred_element_type=F32), 0.0)
        v.append(vh)
        o_intra.append(jnp.dot(a.astype(BF16), vh, preferred_element_type=F32))
        kd.append((kk3 * jnp.exp(b_last - b3)).astype(BF16))
        qe.append((qh3 * jnp.exp(b3)).astype(BF16))
        decay.append(jnp.exp(b_last))

    outs = [[] for _ in range(HG_HEADS)]
    for ci in range(nch):
        rows = slice(ci * c, (ci + 1) * c)
        for h in range(HG_HEADS):
            s_t = s_ref[h]
            o_inter = lax.dot_general(qe[h][ci], s_t.astype(BF16), _NT, preferred_element_type=F32)
            outs[h].append(o_intra[h][rows, :] + o_inter)
            ds_t = lax.dot_general(v[h][rows, :], kd[h][ci], _TN, preferred_element_type=F32)
            s_ref[h] = s_t * decay[h][ci] + ds_t

    for h in range(HG_HEADS):
        sl = slice(h * LANES, (h + 1) * LANES)
        o = jnp.concatenate(outs[h], axis=0)
        o = o * lax.rsqrt(jnp.mean(o * o, axis=-1, keepdims=True) + EPS)
        g = _hg_tile(hg_ref, HGG_OFF, h)
        o_ref[:, sl] = (o * nw_ref[:, sl] * _silu(g)).astype(BF16)


def _tri_chunks(t, c):
    r = np.arange(t)
    return jnp.asarray((r[:, None] >= r[None, :]) & (r[:, None] // c == r[None, :] // c), BF16)


def _hgrn_prompt_parts(proj, p, lb, bp, lp):
    nc = lp // HG_T
    const = lambda b, n: (0, 0)
    return dict(
        in_specs=[
            pl.BlockSpec((HG_T, HG_COLS), lambda b, n: (b * nc + n, 0)),
            pl.BlockSpec((1, HG_WIDTH), const),
            pl.BlockSpec((1, HG_WIDTH), const),
            pl.BlockSpec((HG_T, HG_T), const),
        ],
        out_specs=[
            pl.BlockSpec((HG_T, HG_WIDTH), lambda b, n: (b * nc + n, 0)),
            pl.BlockSpec((None, HG_HEADS, HG_DK, HG_DV), lambda b, n: (b, 0, 0, 0)),
        ],
        out_shape=[
            jax.ShapeDtypeStruct((bp * lp, HG_WIDTH), BF16),
            jax.ShapeDtypeStruct((bp, HG_HEADS, HG_DK, HG_DV), F32),
        ],
        scratch_shapes=[pltpu.VMEM((HG_HEADS, HG_DV, HG_DK), F32)],
        args=[proj, lb.reshape(1, HG_WIDTH), p['hg_norm_w'].reshape(1, HG_WIDTH), _tri_chunks(HG_T, HG_CHUNK)],
    )


RG_T = 256


def _scan_rows(a, u, h_in):
    t, w = a.shape
    rows = lax.broadcasted_iota(jnp.int32, a.shape, 0) % SUBLANES
    s = 1
    while s < SUBLANES:
        keep = rows >= s
        a_sh = jnp.where(keep, pltpu.roll(a, s, axis=0), 1.0)
        u_sh = jnp.where(keep, pltpu.roll(u, s, axis=0), 0.0)
        u = u + a * u_sh
        a = a * a_sh
        s *= 2
    hs = []
    for g in range(t // SUBLANES):
        sl = slice(g * SUBLANES, (g + 1) * SUBLANES)
        h = u[sl, :] + a[sl, :] * h_in
        hs.append(h)
        h_in = h[SUBLANES - 1:, :]
    return jnp.concatenate(hs, axis=0)


def _rg_gates(xc, wa_ref, ba_ref, wx_ref, bx_ref, lam_ref):
    xb = xc.astype(BF16)
    ra, ri = [], []
    for h in range(RG_HEADS):
        sl = slice(h * RG_HEAD_DIM, (h + 1) * RG_HEAD_DIM)
        ra.append(jnp.dot(xb[:, sl], wa_ref[h], preferred_element_type=F32))
        ri.append(jnp.dot(xb[:, sl], wx_ref[h], preferred_element_type=F32))
    r = jax.nn.sigmoid(jnp.concatenate(ra, axis=1) + ba_ref[...])
    ig = jax.nn.sigmoid(jnp.concatenate(ri, axis=1) + bx_ref[...])
    log_a = -RG_C * r * _softplus(-lam_ref[...])
    a = jnp.exp(log_a)
    mult = jnp.sqrt(_neg_expm1(2.0 * log_a, a * a))
    return a, mult, ig


def _gelu_tanh(x):
    return 0.5 * x * (1.0 + jnp.tanh(math.sqrt(2.0 / math.pi) * (x + 0.044715 * (x * x * x))))


N_RG = RG_WIDTH // PIECE


RG_N_IN, RG_N_OUT, RG_N_SCRATCH = 2 * N_RG + 7, 3, 3


def _rglru_prompt_body(ins, outs, scratch):
    x_refs, gate_refs = ins[:N_RG], ins[N_RG:2 * N_RG]
    cw_ref, cb_ref, wa_ref, ba_ref, wx_ref, bx_ref, lam_ref = ins[2 * N_RG:]
    o_ref, h_ref, cst_ref = outs
    xp_ref, tail_ref, hprev_ref = scratch
    n = pl.program_id(1)
    t = RG_T

    x = _cat(x_refs)
    xc = _conv4(xp_ref, tail_ref, x, cw_ref, cb_ref)
    cst_ref[...] = _cat(x_refs, slice(t - (CONV_W - 1), t))
    a, mult, ig = _rg_gates(xc, wa_ref, ba_ref, wx_ref, bx_ref, lam_ref)
    rows = lax.broadcasted_iota(jnp.int32, a.shape, 0)
    mult = jnp.where((rows == 0) & (n == 0), 1.0, mult)
    h = _scan_rows(a, mult * ig * xc, hprev_ref[...])
    hprev_ref[...] = h[t - 1:t, :]
    h_ref[...] = h[t - 1:t, :]
    o_ref[...] = (h * _gelu_tanh(_cat(gate_refs))).astype(BF16)


def _rglru_prompt_parts(proj, p, bp, lp):
    nc = lp // RG_T
    full2 = lambda b, n: (0, 0)
    full3 = lambda b, n: (0, 0, 0)
    rows = lambda b, n: b * nc + n
    row = lambda a: a.reshape(1, RG_WIDTH)
    return dict(
        in_specs=_pieces(RGX_OFF, RG_WIDTH, RG_T, rows) + _pieces(RGG_OFF, RG_WIDTH, RG_T, rows) + [
            pl.BlockSpec((CONV_W, RG_WIDTH), full2),
            pl.BlockSpec((1, RG_WIDTH), full2),
            pl.BlockSpec((RG_HEADS, RG_HEAD_DIM, RG_HEAD_DIM), full3),
            pl.BlockSpec((1, RG_WIDTH), full2),
            pl.BlockSpec((RG_HEADS, RG_HEAD_DIM, RG_HEAD_DIM), full3),
            pl.BlockSpec((1, RG_WIDTH), full2),
            pl.BlockSpec((1, RG_WIDTH), full2),
        ],
        out_specs=[
            pl.BlockSpec((RG_T, RG_WIDTH), lambda b, n: (b * nc + n, 0)),
            pl.BlockSpec((None, 1, RG_WIDTH), lambda b, n: (b, 0, 0)),
            pl.BlockSpec((None, CONV_W - 1, RG_WIDTH), lambda b, n: (b, 0, 0)),
        ],
        out_shape=[
            jax.ShapeDtypeStruct((bp * lp, RG_WIDTH), BF16),
            jax.ShapeDtypeStruct((bp, 1, RG_WIDTH), F32),
            jax.ShapeDtypeStruct((bp, CONV_W - 1, RG_WIDTH), F32),
        ],
        scratch_shapes=[
            pltpu.VMEM((RG_T + SUBLANES, RG_WIDTH), F32),
            pltpu.VMEM((SUBLANES, RG_WIDTH), F32),
            pltpu.VMEM((1, RG_WIDTH), F32),
        ],
        args=[proj] * (2 * N_RG) + [
            p['rg_conv_w'], row(p['rg_conv_b']), p['rg_wa'].astype(BF16), row(p['rg_ba']),
            p['rg_wx'].astype(BF16), row(p['rg_bx']), row(p['rg_lambda'])],
    )


PROMPT_T = 256
assert HG_T == RG_T == SSD_T == PROMPT_T


def _prompt_mix_kernel(*refs):
    counts = ((HG_N_IN, HG_N_OUT, HG_N_SCRATCH), (RG_N_IN, RG_N_OUT, RG_N_SCRATCH),
              (SSD_N_IN, SSD_N_OUT, SSD_N_SCRATCH))
    refs = list(refs)
    ins = [[refs.pop(0) for _ in range(c[0])] for c in counts]
    outs = [[refs.pop(0) for _ in range(c[1])] for c in counts]
    scratch = [[refs.pop(0) for _ in range(c[2])] for c in counts]
    assert not refs
    n = pl.program_id(1)
    hg_s, = scratch[0]
    _, rg_tail, rg_h = scratch[1]
    _, ssd_tail, ssd_s, _ = scratch[2]

    @pl.when(n == 0)
    def _():
        for r in (hg_s, rg_tail, rg_h, ssd_tail, ssd_s):
            r[...] = jnp.zeros_like(r)

    _rglru_prompt_body(ins[1], outs[1], scratch[1])
    _ssd_prompt_body(ins[2], outs[2], scratch[2])
    _hgrn_prompt_body(ins[0], outs[0], scratch[0])

    @pl.when(n == pl.num_programs(1) - 1)
    def _():
        for h in range(HG_HEADS):
            outs[0][1][h] = hg_s[h].T
        outs[2][1][...] = ssd_s[...]


def _prompt_mix(proj, p, lb, bp, lp):
    parts = (_hgrn_prompt_parts(proj, p, lb, bp, lp), _rglru_prompt_parts(proj, p, bp, lp),
             _ssd_prompt_parts(proj, p, bp, lp))
    cat = lambda key: [v for part in parts for v in part[key]]
    return pl.pallas_call(
        _prompt_mix_kernel,
        grid=(bp, lp // PROMPT_T),
        in_specs=cat('in_specs'),
        out_specs=cat('out_specs'),
        out_shape=cat('out_shape'),
        scratch_shapes=cat('scratch_shapes'),
        compiler_params=pltpu.CompilerParams(
            dimension_semantics=("parallel", "arbitrary"), vmem_limit_bytes=VMEM_LIMIT),
        name="prompt_mix",
    )(*cat('args'))


SB = 8


def _col_bcast(row):
    return jnp.broadcast_to(row, (LANES, LANES)).T


N_SAMPLE_IN = 22


def _sample_kernel(*refs, n_prev):
    (p_ref, hs_ref, rh_ref, rc_ref, ss_ref, sc_ref, lb_ref, hnw_ref,
     rcw_ref, rcb_ref, wa_ref, ba_ref, wx_ref, bx_ref, lam_ref,
     scw_ref, scb_ref, dtb_ref, aexp_ref, dexp_ref, snw_ref, expand_ref) = refs[:N_SAMPLE_IN]
    prev = refs[N_SAMPLE_IN:N_SAMPLE_IN + 2 * n_prev]
    (ohg_ref, org_ref, ossm_ref, hs_all, rh_out, rc_out, ss_all, sc_out,
     f_sc, kk_sc, q_sc, o_sc, adec_sc, xdt_sc, b_sc, c_sc, y_sc) = refs[N_SAMPLE_IN + 2 * n_prev:]
    if n_prev:
        for l in range(n_prev):
            hs_all[l] = prev[2 * l][...]
            ss_all[l] = prev[2 * l + 1][...]
        hs_out, ss_out = hs_all.at[n_prev], ss_all.at[n_prev]
    else:
        hs_out, ss_out = hs_all, ss_all

    x = p_ref[:, RGX_OFF:RGX_OFF + RG_WIDTH]
    xc = rcb_ref[...] + rcw_ref[CONV_W - 1:CONV_W, :] * x
    for k in range(CONV_W - 1):
        xc = xc + rcw_ref[k:k + 1, :] * rc_ref[k]
        rc_out[k] = x if k == CONV_W - 2 else rc_ref[k + 1]
    a, mult, ig = _rg_gates(xc, wa_ref, ba_ref, wx_ref, bx_ref, lam_ref)
    h = a * rh_ref[...] + mult * ig * xc
    rh_out[...] = h
    org_ref[...] = (h * _gelu_tanh(p_ref[:, RGG_OFF:RGG_OFF + RG_WIDTH])).astype(BF16)

    z = p_ref[:, SSZ_OFF:SSZ_OFF + SSM_WIDTH]
    xbc_raw = p_ref[:, SSX_OFF:SSX_OFF + SSM_CONV_DIM]
    dt_raw = p_ref[:, SSDT_OFF:SSDT_OFF + LANES]
    xbc = scb_ref[...] + scw_ref[CONV_W - 1:CONV_W, :] * xbc_raw
    for k in range(CONV_W - 1):
        xbc = xbc + scw_ref[k:k + 1, :] * sc_ref[k]
        sc_out[k] = xbc_raw if k == CONV_W - 2 else sc_ref[k + 1]
    xbc = _silu(xbc)
    xs = xbc[:, :SSM_WIDTH]
    gs = SSM_GROUPS * SSM_DSTATE
    b_sc[...] = xbc[:, SSM_WIDTH:SSM_WIDTH + gs]
    c_sc[...] = xbc[:, SSM_WIDTH + gs:]
    dt = _softplus(dt_raw + dtb_ref[...])
    parts = jnp.concatenate(_split3(dt), axis=0)
    r = jnp.dot(parts, expand_ref[...], preferred_element_type=F32)
    dt_exp = r[0:SB] + r[SB:2 * SB] + r[2 * SB:]
    xdt_sc[...] = xs * dt_exp
    adec_sc[...] = jnp.exp(dt_exp * aexp_ref[...])

    hg_tile = functools.partial(_hg_tile, p_ref)
    for hd in range(HG_HEADS):
        sl = slice(hd * LANES, (hd + 1) * LANES)
        lb = lb_ref[:, sl]
        fz = hg_tile(HGF_OFF, hd)
        f_sc[:, sl] = lb + (1.0 - lb) * jax.nn.sigmoid(fz)
        kk_sc[:, sl] = (1.0 - lb) * jax.nn.sigmoid(-fz)
        q_sc[:, sl] = _silu(hg_tile(HGQ_OFF, hd))

    for j in range(SB):
        row = slice(j, j + 1)
        for hd in range(HG_HEADS):
            sl = slice(hd * LANES, (hd + 1) * LANES)
            v_row = hg_tile(HGV_OFF, hd, row)
            s_new = _col_bcast(f_sc[row, sl]) * hs_ref[j, hd] + _col_bcast(kk_sc[row, sl]) * v_row
            hs_out[j, hd] = s_new
            q8 = jnp.broadcast_to(q_sc[row, sl], (SB, LANES)).astype(BF16)
            o_sc[row, sl] = jnp.dot(q8, s_new.astype(BF16), preferred_element_type=F32)[0:1]
        for hp in range(SSM_HEADS // 2):
            sl = slice(hp * LANES, (hp + 1) * LANES)
            g = (2 * hp) // HEADS_PER_GROUP
            gsl = slice(g * SSM_DSTATE, (g + 1) * SSM_DSTATE)
            s_old = ss_ref[j, 2 * hp:2 * hp + 2].reshape(LANES, SSM_DSTATE)
            s_new = _col_bcast(adec_sc[row, sl]) * s_old + _col_bcast(xdt_sc[row, sl]) * b_sc[row, gsl]
            ss_out[j, 2 * hp:2 * hp + 2] = s_new.reshape(2, SSM_HEADDIM, SSM_DSTATE)
            c8 = jnp.broadcast_to(c_sc[row, gsl], (SB, SSM_DSTATE)).astype(BF16)
            y_sc[row, sl] = lax.dot_general(c8, s_new.astype(BF16), _NT, preferred_element_type=F32)[0:1]

    for hd in range(HG_HEADS):
        sl = slice(hd * LANES, (hd + 1) * LANES)
        o = o_sc[:, sl]
        o = o * lax.rsqrt(jnp.mean(o * o, axis=-1, keepdims=True) + EPS)
        ohg_ref[:, sl] = (o * hnw_ref[:, sl] * _silu(hg_tile(HGG_OFF, hd))).astype(BF16)
    y = (y_sc[...] + dexp_ref[...] * xs) * _silu(z)
    gw = SSM_WIDTH // SSM_GROUPS
    for g in range(SSM_GROUPS):
        yg = y[:, g * gw:(g + 1) * gw]
        yg = yg * lax.rsqrt(jnp.mean(yg * yg, axis=-1, keepdims=True) + EPS)
        ossm_ref[:, g * gw:(g + 1) * gw] = (yg * snw_ref[:, g * gw:(g + 1) * gw]).astype(BF16)


def _sample_mix(proj, states, layer, p, lb, mp, prev=()):
    hg_s, rg_h, rg_c_t, ssm_s, ssm_c_t = states
    bs = hg_s.shape[1]
    rb = mp // SB
    const2 = lambda i: (0, 0)
    const3 = lambda i: (0, 0, 0)
    row = lambda a, w: a.reshape(1, w)
    dtb = jnp.pad(p['ssm_dt_bias'].reshape(1, SSM_HEADS), ((0, 0), (0, LANES - SSM_HEADS)))
    aexp = jnp.repeat(-jnp.exp(p['ssm_a_log']), SSM_HEADDIM).reshape(1, SSM_WIDTH)
    dexp = jnp.repeat(p['ssm_d'], SSM_HEADDIM).reshape(1, SSM_WIDTH)
    expand = _head_to_lanes()
    in_specs = [
        pl.BlockSpec((SB, D_IN_PAD), lambda i: (rb + i, 0)),
        pl.BlockSpec((None, SB, HG_HEADS, HG_DK, HG_DV), lambda i: (layer, i, 0, 0, 0)),
        pl.BlockSpec((None, SB, RG_WIDTH), lambda i: (layer, i, 0)),
        pl.BlockSpec((None, CONV_W - 1, SB, RG_WIDTH), lambda i: (layer, 0, i, 0)),
        pl.BlockSpec((None, SB, SSM_HEADS, SSM_HEADDIM, SSM_DSTATE), lambda i: (layer, i, 0, 0, 0)),
        pl.BlockSpec((None, CONV_W - 1, SB, SSM_CONV_DIM), lambda i: (layer, 0, i, 0)),
        pl.BlockSpec((1, HG_WIDTH), const2),
        pl.BlockSpec((1, HG_WIDTH), const2),
        pl.BlockSpec((CONV_W, RG_WIDTH), const2),
        pl.BlockSpec((1, RG_WIDTH), const2),
        pl.BlockSpec((RG_HEADS, RG_HEAD_DIM, RG_HEAD_DIM), const3),
        pl.BlockSpec((1, RG_WIDTH), const2),
        pl.BlockSpec((RG_HEADS, RG_HEAD_DIM, RG_HEAD_DIM), const3),
        pl.BlockSpec((1, RG_WIDTH), const2),
        pl.BlockSpec((1, RG_WIDTH), const2),
        pl.BlockSpec((CONV_W, SSM_CONV_DIM), const2),
        pl.BlockSpec((1, SSM_CONV_DIM), const2),
        pl.BlockSpec((1, LANES), const2),
        pl.BlockSpec((1, SSM_WIDTH), const2),
        pl.BlockSpec((1, SSM_WIDTH), const2),
        pl.BlockSpec((1, SSM_WIDTH), const2),
        pl.BlockSpec((LANES, SSM_WIDTH), const2),
    ]
    hg_blk, ssm_blk = (SB, HG_HEADS, HG_DK, HG_DV), (SB, SSM_HEADS, SSM_HEADDIM, SSM_DSTATE)
    n_prev = len(prev)
    for _ in prev:
        in_specs += [pl.BlockSpec(hg_blk, lambda i: (i, 0, 0, 0)), pl.BlockSpec(ssm_blk, lambda i: (i, 0, 0, 0))]
    stacked = lambda blk: (pl.BlockSpec((n_prev + 1,) + blk, lambda i: (0, i, 0, 0, 0)) if n_prev
                           else pl.BlockSpec(blk, lambda i: (i, 0, 0, 0)))
    lead = (n_prev + 1,) if n_prev else ()
    out_specs = [
        pl.BlockSpec((SB, HG_WIDTH), lambda i: (i, 0)),
        pl.BlockSpec((SB, RG_WIDTH), lambda i: (i, 0)),
        pl.BlockSpec((SB, SSM_WIDTH), lambda i: (i, 0)),
        stacked(hg_blk),
        pl.BlockSpec((SB, RG_WIDTH), lambda i: (i, 0)),
        pl.BlockSpec((CONV_W - 1, SB, RG_WIDTH), lambda i: (0, i, 0)),
        stacked(ssm_blk),
        pl.BlockSpec((CONV_W - 1, SB, SSM_CONV_DIM), lambda i: (0, i, 0)),
    ]
    out_shape = [
        jax.ShapeDtypeStruct((bs, HG_WIDTH), BF16),
        jax.ShapeDtypeStruct((bs, RG_WIDTH), BF16),
        jax.ShapeDtypeStruct((bs, SSM_WIDTH), BF16),
        jax.ShapeDtypeStruct(lead + hg_s.shape[1:], F32),
        jax.ShapeDtypeStruct(rg_h.shape[1:], F32),
        jax.ShapeDtypeStruct(rg_c_t.shape[1:], F32),
        jax.ShapeDtypeStruct(lead + ssm_s.shape[1:], F32),
        jax.ShapeDtypeStruct(ssm_c_t.shape[1:], F32),
    ]
    outs = pl.pallas_call(
        functools.partial(_sample_kernel, n_prev=n_prev),
        grid=(bs // SB,),
        in_specs=in_specs,
        out_specs=out_specs,
        out_shape=out_shape,
        scratch_shapes=[
            pltpu.VMEM((SB, HG_WIDTH), F32), pltpu.VMEM((SB, HG_WIDTH), F32), pltpu.VMEM((SB, HG_WIDTH), F32),
            pltpu.VMEM((SB, HG_WIDTH), F32),
            pltpu.VMEM((SB, SSM_WIDTH), F32), pltpu.VMEM((SB, SSM_WIDTH), F32),
            pltpu.VMEM((SB, SSM_GROUPS * SSM_DSTATE), F32), pltpu.VMEM((SB, SSM_GROUPS * SSM_DSTATE), F32),
            pltpu.VMEM((SB, SSM_WIDTH), F32),
        ],
        compiler_params=pltpu.CompilerParams(
            dimension_semantics=("parallel",), vmem_limit_bytes=VMEM_LIMIT),
        name="sample_mix",
    )(proj, hg_s, rg_h, rg_c_t, ssm_s, ssm_c_t,
      lb.reshape(1, HG_WIDTH), row(p['hg_norm_w'], HG_WIDTH),
      p['rg_conv_w'], row(p['rg_conv_b'], RG_WIDTH), p['rg_wa'].astype(BF16), row(p['rg_ba'], RG_WIDTH),
      p['rg_wx'].astype(BF16), row(p['rg_bx'], RG_WIDTH), row(p['rg_lambda'], RG_WIDTH),
      p['ssm_conv_w'], row(p['ssm_conv_b'], SSM_CONV_DIM), dtb, aexp, dexp, row(p['ssm_norm_w'], SSM_WIDTH),
      expand, *(a for pair in prev for a in pair))
    return tuple(outs[:3]), tuple(outs[3:])


def kernel(x_prompt, x_sample, state_hgrn, state_rglru, state_rglru_conv, state_ssm, state_ssm_conv, norm_g, ffn1_w_gate, ffn1_w_up, ffn1_w_down, ffn2_w_gate, ffn2_w_up, ffn2_w_down, w_in, w_out, hg_lb_logits, hg_norm_w, rg_conv_w, rg_conv_b, rg_wa, rg_ba, rg_wx, rg_bx, rg_lambda, ssm_conv_w, ssm_conv_b, ssm_dt_bias, ssm_a_log, ssm_d, ssm_norm_w):
    bp, lp, _ = x_prompt.shape
    bs, ls, _ = x_sample.shape
    mp = bp * lp
    ms = bs * ls
    assert (mp + ms) % IN_ROWS == 0 and D_FF % FF_TILE == 0 and D_IN_PAD % IN_TILE == 0

    lw = {
        'hg_norm_w': hg_norm_w, 'rg_conv_w': rg_conv_w, 'rg_conv_b': rg_conv_b, 'rg_wa': rg_wa,
        'rg_ba': rg_ba, 'rg_wx': rg_wx, 'rg_bx': rg_bx, 'rg_lambda': rg_lambda,
        'ssm_conv_w': ssm_conv_w, 'ssm_conv_b': ssm_conv_b, 'ssm_dt_bias': ssm_dt_bias,
        'ssm_a_log': ssm_a_log, 'ssm_d': ssm_d, 'ssm_norm_w': ssm_norm_w,
    }
    lb_cum = jnp.cumsum(jax.nn.softmax(hg_lb_logits.astype(F32), axis=0), axis=0)
    lower_bounds = lb_cum - lb_cum[:1]

    taps_first = lambda a: jnp.transpose(a, (0, 2, 1, 3))
    sample_init = (state_hgrn, state_rglru, taps_first(state_rglru_conv), state_ssm, taps_first(state_ssm_conv))

    w_in_b = jnp.pad(w_in.astype(BF16), ((0, 0), (0, 0), (0, D_IN_PAD - D_IN_PROJ)))
    x = (x_prompt.reshape(mp, D_MODEL), x_sample.reshape(ms, D_MODEL))
    new_p = ([], [], [], [], [])
    new_s = ([], [], [], [], [])
    for l in range(DEPTH):
        g = norm_g[l].reshape(6, 1, D_MODEL)
        p = {name: arr[l] for name, arr in lw.items()}
        x = _ffn(x, g[0], g[1], ffn1_w_gate, ffn1_w_up, ffn1_w_down, l, mp, ms, split_in=(l == 0))
        proj = _inproj(x, g[2], w_in_b, l)
        o_hg, hg_new, o_rg, rg_h_new, rg_c_new, o_ssm, ssm_new, ssm_c_new = _prompt_mix(
            proj, p, lower_bounds[l], bp, lp)
        st_p = (hg_new, rg_h_new.reshape(bp, RG_WIDTH), rg_c_new, ssm_new, ssm_c_new)
        last = l == DEPTH - 1
        prev = tuple(zip(new_s[0], new_s[3])) if last else ()
        o_s, st_s = _sample_mix(proj, sample_init, l, p, lower_bounds[l], mp, prev)
        x = _outproj(x, (o_hg, o_rg, o_ssm), o_s, g[3], w_out, l)
        x = _ffn(x, g[4], g[5], ffn2_w_gate, ffn2_w_up, ffn2_w_down, l, mp, ms, split_out=last)
        for acc, s in zip(new_p, st_p):
            acc.append(s)
        for acc, s in zip(new_s, st_s):
            acc.append(s)
    hg_p, rg_p, rgc_p, ssm_p, ssmc_p = (jnp.stack(a) for a in new_p)
    hg_s, ssm_s = new_s[0][-1], new_s[3][-1]
    rg_s, rgc_s, ssmc_s = (jnp.stack(new_s[k]) for k in (1, 2, 4))
    rgc_s, ssmc_s = taps_first(rgc_s), taps_first(ssmc_s)
    y_prompt = x[0].reshape(bp, lp, D_MODEL)
    y_sample = x[1].reshape(bs, ls, D_MODEL)
    return (y_prompt, y_sample, hg_p, hg_s, rg_p, rg_s, rgc_p, rgc_s, ssm_p, ssm_s, ssmc_p, ssmc_s)
```
